```python
import math
import jax
import jax.numpy as jnp
from jax import lax
import numpy as np

D_MODEL = 1024
BATCH = 8
SEQ = 4096
DEPTH = 1
DEC_BATCH = 128
DEC_SEQ = 4
PAST_LEN = 8192
PAGE_SIZE = 128

HEAD_DIM = 64
N_HEADS = 8
N_KV_HEADS = 2
GROUP = N_HEADS // N_KV_HEADS
ATTN_WIDTH = N_HEADS * HEAD_DIM
KV_WIDTH = N_KV_HEADS * HEAD_DIM
CMP_BLOCK = 64
CMP_HIDDEN = 128
SEL_BLOCK = CMP_BLOCK
N_SEL = 16
WINDOW = 512
WIN_QBLOCK = 128
SEL_QBLOCK = 64
ROT_DIM = HEAD_DIM // 4
ROPE_THETA = 500000.0
RNN_WIDTH = D_MODEL - ATTN_WIDTH
RNN_BLOCKS = 8
RNN_BLOCK_W = RNN_WIDTH // RNN_BLOCKS
CONV_W = 4
LRU_C = 8.0
D_FF = ((8 * D_MODEL + 3 * 256 - 1) // (3 * 256)) * 256
IN_COLS = 2 * RNN_WIDTH + ATTN_WIDTH + 6 * KV_WIDTH + 3 * N_HEADS
ATTN_SCALE = HEAD_DIM ** -0.5
EPS = 1e-6
NEG_INF = -1e30
FORCE_SCORE = 1e4

kernel_name = 'hymba_rglru_nsa_adaln_step'


def rmsnorm(x, g):
    xf = x.astype(jnp.float32)
    y = xf * lax.rsqrt(jnp.mean(xf * xf, axis=-1, keepdims=True) + EPS)
    return (y * g.astype(jnp.float32)).astype(x.dtype)


def rope_partial(x, pos):
    half = ROT_DIM // 2
    inv = jnp.exp(jnp.arange(half, dtype=jnp.float32) * (-math.log(ROPE_THETA) / half))
    ang = pos.astype(jnp.float32)[:, None] * inv[None, :]
    cos = jnp.cos(ang)[:, None, :]
    sin = jnp.sin(ang)[:, None, :]
    xf = x.astype(jnp.float32)
    x1, x2 = xf[..., :half], xf[..., half:ROT_DIM]
    out = jnp.concatenate([x1 * cos - x2 * sin, x1 * sin + x2 * cos, xf[..., ROT_DIM:]], axis=-1)
    return out.astype(x.dtype)


def rope_keys(kv, pos):
    return jnp.stack([rope_partial(kv[:, :, 0], pos), kv[:, :, 1]], axis=2)


def to_blocks(kv, n_blocks):
    S = kv.shape[1]
    kv = jnp.pad(kv, ((0, 0), (0, n_blocks * CMP_BLOCK - S), (0, 0), (0, 0), (0, 0)))
    return kv.reshape((kv.shape[0], n_blocks, CMP_BLOCK) + kv.shape[2:])


def split_proj(z):
    B, S, _ = z.shape
    cuts = np.cumsum([RNN_WIDTH, RNN_WIDTH, ATTN_WIDTH, 2 * KV_WIDTH, 2 * KV_WIDTH, 2 * KV_WIDTH]).tolist()
    xr, yr, q, kvc, kvs, kvw, g = jnp.split(z, cuts, axis=-1)
    kv_shape = (B, S, 2, N_KV_HEADS, HEAD_DIM)
    return (xr, yr, q.reshape(B, S, N_HEADS, HEAD_DIM), kvc.reshape(kv_shape), kvs.reshape(kv_shape),
            kvw.reshape(kv_shape), jax.nn.sigmoid(g.reshape(B, S, 3, N_HEADS)))


def lru_combine(left, right):
    a1, b1 = left
    a2, b2 = right
    return a1 * a2, a2 * b1 + b2


def rglru(xr, yr, conv_prev, h0, lp):
    B, S, _ = xr.shape
    xpad = jnp.concatenate([conv_prev.astype(xr.dtype), xr], axis=1)
    xc = lp['conv_b']
    for k in range(CONV_W):
        xc = xc + xpad[:, k:k + S] * lp['conv_w'][k]
    xb = xc.reshape(B, S, RNN_BLOCKS, RNN_BLOCK_W)
    r = jax.nn.sigmoid(jnp.einsum('bsnd,nde->bsne', xb, lp['lru_wa']).reshape(B, S, RNN_WIDTH) + lp['lru_ba'])
    i = jax.nn.sigmoid(jnp.einsum('bsnd,nde->bsne', xb, lp['lru_wx']).reshape(B, S, RNN_WIDTH) + lp['lru_bx'])
    log_a = -LRU_C * r.astype(jnp.float32) * jax.nn.softplus(-lp['lru_lambda'].astype(jnp.float32))
    a = jnp.exp(log_a)
    b = jnp.sqrt(-jnp.expm1(2.0 * log_a)) * (i * xc).astype(jnp.float32)
    b = b.at[:, 0].add(a[:, 0] * h0.astype(jnp.float32))
    _, h = lax.associative_scan(lru_combine, (a, b), axis=1)
    out = h.astype(xr.dtype) * jax.nn.gelu(yr)
    return out, xpad[:, S:], h[:, -1].astype(h0.dtype)


def compress(kv_blocks, lp):
    z = kv_blocks + jnp.transpose(lp['cmp_pe'], (1, 0, 2))[:, :, None, :]
    hid = jax.nn.gelu(jnp.einsum('bnlckd,cldf->bnckf', z, lp['cmp_w1']))
    return jnp.einsum('bnckf,cfd->bnckd', hid, lp['cmp_w2'])


def cmp_attend(q, ckv, q_pos):
    B, Sq = q.shape[:2]
    nb = ckv.shape[1]
    qg = q.reshape(B, Sq, N_KV_HEADS, GROUP, HEAD_DIM)
    s = jnp.einsum('bqkgd,bnkd->bkgqn', qg, ckv[:, :, 0]).astype(jnp.float32) * ATTN_SCALE
    blk_end = (jnp.arange(nb) + 1) * CMP_BLOCK - 1
    m = blk_end[None, :] <= q_pos[:, None]
    p = jnp.where(m, jax.nn.softmax(jnp.where(m, s, NEG_INF), axis=-1), 0.0)
    o = jnp.einsum('bkgqn,bnkd->bqkgd', p.astype(q.dtype), ckv[:, :, 1])
    imp = jnp.transpose(p.sum(axis=2), (0, 2, 1, 3))
    return o.reshape(B, Sq, N_HEADS, HEAD_DIM), imp


def select_blocks(imp, q_pos):
    nb = imp.shape[-1]
    j = jnp.arange(nb)[None, :]
    cur = (q_pos // SEL_BLOCK)[:, None]
    forced = ((j == 0) | (j == cur) | (j == cur - 1))[None, :, None, :]
    future = (j > cur)[None, :, None, :]
    score = jnp.where(future, -FORCE_SCORE, jnp.where(forced, FORCE_SCORE, imp))
    _, idx = lax.top_k(score, min(N_SEL, nb))
    valid = idx <= (q_pos // SEL_BLOCK)[None, :, None, None]
    return idx, valid


def sel_attend(q, q_pos, idx, valid, kg, vg):
    B, Sq = q.shape[:2]
    n_sel = idx.shape[-1]
    qg = q.reshape(B, Sq, N_KV_HEADS, GROUP, HEAD_DIM)
    s = jnp.einsum('bqkgd,bqknld->bqkgnl', qg, kg).astype(jnp.float32) * ATTN_SCALE
    kpos = idx[..., None] * SEL_BLOCK + jnp.arange(SEL_BLOCK)
    m = (valid[..., None] & (kpos <= q_pos[None, :, None, None, None]))[:, :, :, None]
    s = jnp.where(m, s, NEG_INF).reshape(B, Sq, N_KV_HEADS, GROUP, n_sel * SEL_BLOCK)
    p = jax.nn.softmax(s, axis=-1).astype(q.dtype)
    o = jnp.einsum('bqkgm,bqkmd->bqkgd', p, vg.reshape(B, Sq, N_KV_HEADS, n_sel * SEL_BLOCK, HEAD_DIM))
    return o.reshape(B, Sq, N_HEADS, HEAD_DIM)


def sel_prompt(q_rot, pos, idx, valid, kvs):
    B, S = q_rot.shape[:2]
    nq = S // SEL_QBLOCK
    kvb = kvs.reshape((B, S // SEL_BLOCK, SEL_BLOCK) + kvs.shape[2:])
    bi = jnp.arange(B)[:, None, None, None]
    hi = jnp.arange(N_KV_HEADS)[None, None, :, None]

    def blk(a):
        return jnp.moveaxis(a.reshape((B, nq, SEL_QBLOCK) + a.shape[2:]), 1, 0)

    def step(args):
        qb, pb, ib, vb = args
        g = kvb[bi, ib, :, :, hi, :]
        return sel_attend(qb, pb, ib, vb, g[..., 0, :], g[..., 1, :])

    out = lax.map(step, (blk(q_rot), pos.reshape(nq, SEL_QBLOCK), blk(idx), blk(valid)))
    return jnp.moveaxis(out, 0, 1).reshape(B, S, N_HEADS, HEAD_DIM)


def sel_sample(q_rot, pos, idx, valid, sel_pool, kvs_new, page_table):
    B, S = q_rot.shape[:2]
    bpp = PAGE_SIZE // SEL_BLOCK
    nbp = page_table.shape[1] * bpp
    nbn = -(-S // SEL_BLOCK)
    pool_b = sel_pool.reshape((-1, bpp, SEL_BLOCK) + sel_pool.shape[2:])
    new_b = to_blocks(kvs_new, nbn)
    bi = jnp.arange(B)[:, None, None, None]
    hi = jnp.arange(N_KV_HEADS)[None, None, :, None]
    jp = jnp.minimum(idx, nbp - 1)
    page = page_table[bi, jp // bpp]
    g_past = pool_b[page, jp % bpp, :, :, hi, :]
    g_new = new_b[bi, jnp.clip(idx - nbp, 0, nbn - 1), :, :, hi, :]
    g = jnp.where((idx < nbp)[..., None, None, None], g_past, g_new)
    return sel_attend(q_rot, pos, idx, valid, g[..., 0, :], g[..., 1, :])


def window_prompt(q, kv):
    B, S = q.shape[:2]
    nb, nw = S // WIN_QBLOCK, WINDOW // WIN_QBLOCK
    kvp = jnp.pad(kv, ((0, 0), (WINDOW, 0), (0, 0), (0, 0), (0, 0)))
    kvb = kvp.reshape((B, nb + nw, WIN_QBLOCK) + kv.shape[2:])
    band = jnp.stack([kvb[:, i:i + nb] for i in range(nw + 1)], axis=2)
    band = band.reshape((B, nb, (nw + 1) * WIN_QBLOCK) + kv.shape[2:])
    qb = q.reshape(B, nb, WIN_QBLOCK, N_KV_HEADS, GROUP, HEAD_DIM)
    s = jnp.einsum('bnqkgd,bnskd->bnkgqs', qb, band[:, :, :, 0]).astype(jnp.float32) * ATTN_SCALE
    qpos = jnp.arange(S).reshape(nb, WIN_QBLOCK)
    kpos = (jnp.arange(nb)[:, None] - nw) * WIN_QBLOCK + jnp.arange((nw + 1) * WIN_QBLOCK)[None, :]
    diff = qpos[:, :, None] - kpos[:, None, :]
    m = ((diff >= 0) & (diff < WINDOW) & (kpos[:, None, :] >= 0))[None, :, None, None]
    p = jax.nn.softmax(jnp.where(m, s, NEG_INF), axis=-1).astype(q.dtype)
    o = jnp.einsum('bnkgqs,bnskd->bnqkgd', p, band[:, :, :, 1])
    return o.reshape(B, S, N_HEADS, HEAD_DIM)


def window_sample(q, q_pos, buf, kv_new):
    B, Sq = q.shape[:2]
    wb = buf.shape[1]
    kv = jnp.concatenate([buf.astype(kv_new.dtype), kv_new], axis=1)
    kpos = q_pos[0] - wb + jnp.arange(wb + Sq)
    diff = q_pos[:, None] - kpos[None, :]
    m = (diff >= 0) & (diff < WINDOW)
    qg = q.reshape(B, Sq, N_KV_HEADS, GROUP, HEAD_DIM)
    s = jnp.einsum('bqkgd,bskd->bkgqs', qg, kv[:, :, 0]).astype(jnp.float32) * ATTN_SCALE
    p = jax.nn.softmax(jnp.where(m, s, NEG_INF), axis=-1).astype(q.dtype)
    o = jnp.einsum('bkgqs,bskd->bqkgd', p, kv[:, :, 1])
    return o.reshape(B, Sq, N_HEADS, HEAD_DIM), kv[:, Sq:]


def pre_mixer(x, c, lp):
    mods = jnp.split((jax.nn.silu(c) @ lp['w_ada'] + lp['b_ada'])[:, None, :], 6, axis=-1)
    h = rmsnorm(x, lp['ln1']) * (1.0 + mods[1]) + mods[0]
    return mods, split_proj(h @ lp['w_in'])


def post_mixer(x, mods, o_r, o_c, o_s, o_w, gates, lp):
    B, S = x.shape[:2]
    g = gates[..., None]
    attn = g[:, :, 0] * o_c + g[:, :, 1] * o_s + g[:, :, 2] * o_w
    merged = jnp.concatenate([rmsnorm(o_r, lp['gn_rnn']),
                              rmsnorm(attn.reshape(B, S, ATTN_WIDTH), lp['gn_attn'])], axis=-1)
    x = x + mods[2] * (merged @ lp['w_out'])
    h = rmsnorm(x, lp['ln2']) * (1.0 + mods[4]) + mods[3]
    ff = (jax.nn.silu(h @ lp['w_gate']) * (h @ lp['w_up'])) @ lp['w_down']
    return x + mods[5] * ff


def layer_prompt(x, c, lp):
    B, S, _ = x.shape
    mods, (xr, yr, q, kvc, kvs, kvw, gates) = pre_mixer(x, c, lp)
    pos = jnp.arange(S)
    o_r, conv_new, h_new = rglru(xr, yr, jnp.zeros((B, CONV_W - 1, RNN_WIDTH), x.dtype),
                                 jnp.zeros((B, RNN_WIDTH), x.dtype), lp)
    q_rot = rope_partial(q, pos)
    kvs = rope_keys(kvs, pos)
    kvw = rope_keys(kvw, pos)
    ckv = compress(to_blocks(kvc, S // CMP_BLOCK), lp)
    o_c, imp = cmp_attend(q, ckv, pos)
    idx, valid = select_blocks(imp, pos)
    o_s = sel_prompt(q_rot, pos, idx, valid, kvs)
    o_w = window_prompt(q_rot, kvw)
    y = post_mixer(x, mods, o_r, o_c, o_s, o_w, gates, lp)
    new_win = kvw[:, S - min(WINDOW, S):]
    return y, (kvc, kvs, new_win, conv_new, h_new)


def layer_sample(x, c, cmp_pool, sel_pool, win_buf, conv_buf, h0, page_table, lp):
    B, S, _ = x.shape
    past_len = page_table.shape[1] * PAGE_SIZE
    pos = past_len + jnp.arange(S)
    mods, (xr, yr, q, kvc, kvs, kvw, gates) = pre_mixer(x, c, lp)
    o_r, conv_new, h_new = rglru(xr, yr, conv_buf, h0, lp)
    q_rot = rope_partial(q, pos)
    kvs = rope_keys(kvs, pos)
    kvw = rope_keys(kvw, pos)
    past_c = cmp_pool[page_table].reshape((B, past_len // CMP_BLOCK, CMP_BLOCK) + cmp_pool.shape[2:])
    ckv = jnp.concatenate([compress(past_c, lp), compress(to_blocks(kvc, -(-S // CMP_BLOCK)), lp)], axis=1)
    o_c, imp = cmp_attend(q, ckv, pos)
    idx, valid = select_blocks(imp, pos)
    o_s = sel_sample(q_rot, pos, idx, valid, sel_pool, kvs, page_table)
    o_w, new_win = window_sample(q_rot, pos, win_buf, kvw)
    y = post_mixer(x, mods, o_r, o_c, o_s, o_w, gates, lp)
    return y, (kvc, kvs, new_win, conv_new, h_new)


def setup_inputs(seed: int = 0) -> dict:
    key = jax.random.key(seed)
    keys = list(jax.random.split(key, 40))

    def nrm(shape, scale=1.0):
        return jax.random.normal(keys.pop(), shape, jnp.float32) * scale

    n_pages = PAST_LEN // PAGE_SIZE
    n_used = DEC_BATCH * n_pages
    n_pool = (n_used * 5 + 3) // 4
    win_buf = min(WINDOW, PAST_LEN)
    kv_tail = (2, N_KV_HEADS, HEAD_DIM)
    dsc = D_MODEL ** -0.5
    x_prompt = nrm((BATCH, SEQ, D_MODEL))
    x_sample = nrm((DEC_BATCH, DEC_SEQ, D_MODEL))
    cache_cmp_kv = nrm((DEPTH, n_pool, PAGE_SIZE) + kv_tail)
    cache_sel_kv = nrm((DEPTH, n_pool, PAGE_SIZE) + kv_tail)
    state_win_kv = nrm((DEPTH, DEC_BATCH, win_buf) + kv_tail)
    state_conv = nrm((DEPTH, DEC_BATCH, CONV_W - 1, RNN_WIDTH))
    state_lru_h = nrm((DEPTH, DEC_BATCH, RNN_WIDTH), 0.5)
    page_table = jax.random.permutation(keys.pop(), n_pool)[:n_used].reshape(DEC_BATCH, n_pages).astype(jnp.int32)
    u = jax.random.uniform(keys.pop(), (DEPTH, RNN_WIDTH), jnp.float32, 0.9, 0.999)
    s = u ** (1.0 / LRU_C)
    lru_lambda = jnp.log(s) - jnp.log1p(-s)
    return {
        'x_prompt': x_prompt,
        'x_sample': x_sample,
        'cache_cmp_kv': cache_cmp_kv,
        'cache_sel_kv': cache_sel_kv,
        'state_win_kv': state_win_kv,
        'state_conv': state_conv,
        'state_lru_h': state_lru_h,
        'page_table': page_table,
        'c_prompt': nrm((BATCH, D_MODEL)),
        'c_sample': nrm((DEC_BATCH, D_MODEL)),
        'ln1_g': 1.0 + nrm((DEPTH, D_MODEL), 0.02),
        'ln2_g': 1.0 + nrm((DEPTH, D_MODEL), 0.02),
        'w_ada': nrm((DEPTH, D_MODEL, 6 * D_MODEL), 0.5 * dsc),
        'b_ada': nrm((DEPTH, 6 * D_MODEL), 0.02),
        'w_in': nrm((DEPTH, D_MODEL, IN_COLS), dsc),
        'conv_w': nrm((DEPTH, CONV_W, RNN_WIDTH), CONV_W ** -0.5),
        'conv_b': nrm((DEPTH, RNN_WIDTH), 0.02),
        'lru_wa': nrm((DEPTH, RNN_BLOCKS, RNN_BLOCK_W, RNN_BLOCK_W), RNN_BLOCK_W ** -0.5),
        'lru_ba': nrm((DEPTH, RNN_WIDTH), 0.02),
        'lru_wx': nrm((DEPTH, RNN_BLOCKS, RNN_BLOCK_W, RNN_BLOCK_W), RNN_BLOCK_W ** -0.5),
        'lru_bx': nrm((DEPTH, RNN_WIDTH), 0.02),
        'lru_lambda': lru_lambda,
        'cmp_pe': nrm((DEPTH, 2, CMP_BLOCK, HEAD_DIM), 0.1),
        'cmp_w1': nrm((DEPTH, 2, CMP_BLOCK, HEAD_DIM, CMP_HIDDEN), (CMP_BLOCK * HEAD_DIM) ** -0.5),
        'cmp_w2': nrm((DEPTH, 2, CMP_HIDDEN, HEAD_DIM), CMP_HIDDEN ** -0.5),
        'gn_rnn': 1.0 + nrm((DEPTH, RNN_WIDTH), 0.02),
        'gn_attn': 1.0 + nrm((DEPTH, ATTN_WIDTH), 0.02),
        'w_out': nrm((DEPTH, D_MODEL, D_MODEL), dsc),
        'w_gate': nrm((DEPTH, D_MODEL, D_FF), dsc),
        'w_up': nrm((DEPTH, D_MODEL, D_FF), dsc),
        'w_down': nrm((DEPTH, D_FF, D_MODEL), D_FF ** -0.5),
        'final_g': 1.0 + nrm((D_MODEL,), 0.02),
    }


def reference(x_prompt, x_sample, cache_cmp_kv, cache_sel_kv, state_win_kv, state_conv, state_lru_h,
              page_table, c_prompt, c_sample, ln1_g, ln2_g, w_ada, b_ada, w_in, conv_w, conv_b,
              lru_wa, lru_ba, lru_wx, lru_bx, lru_lambda, cmp_pe, cmp_w1, cmp_w2, gn_rnn, gn_attn,
              w_out, w_gate, w_up, w_down, final_g):
    xp, xs = x_prompt, x_sample
    st_p, st_s = [], []
    for l in range(DEPTH):
        lp = {'ln1': ln1_g[l], 'ln2': ln2_g[l], 'w_ada': w_ada[l], 'b_ada': b_ada[l], 'w_in': w_in[l],
              'conv_w': conv_w[l], 'conv_b': conv_b[l], 'lru_wa': lru_wa[l], 'lru_ba': lru_ba[l],
              'lru_wx': lru_wx[l], 'lru_bx': lru_bx[l], 'lru_lambda': lru_lambda[l],
              'cmp_pe': cmp_pe[l], 'cmp_w1': cmp_w1[l], 'cmp_w2': cmp_w2[l],
              'gn_rnn': gn_rnn[l], 'gn_attn': gn_attn[l], 'w_out': w_out[l],
              'w_gate': w_gate[l], 'w_up': w_up[l], 'w_down': w_down[l]}
        xp, new_p = layer_prompt(xp, c_prompt, lp)
        xs, new_s = layer_sample(xs, c_sample, cache_cmp_kv[l], cache_sel_kv[l], state_win_kv[l],
                                 state_conv[l], state_lru_h[l], page_table, lp)
        st_p.append(new_p)
        st_s.append(new_s)
    y_prompt = rmsnorm(xp, final_g)
    y_sample = rmsnorm(xs, final_g)
    new_cmp_kv_prompt = jnp.stack([s[0] for s in st_p], axis=0)
    new_cmp_kv_sample = jnp.stack([s[0] for s in st_s], axis=0)
    new_sel_kv_prompt = jnp.stack([s[1] for s in st_p], axis=0)
    new_sel_kv_sample = jnp.stack([s[1] for s in st_s], axis=0)
    new_win_kv_prompt = jnp.stack([s[2] for s in st_p], axis=0)
    new_win_kv_sample = jnp.stack([s[2] for s in st_s], axis=0)
    new_conv_prompt = jnp.stack([s[3] for s in st_p], axis=0)
    new_conv_sample = jnp.stack([s[3] for s in st_s], axis=0)
    new_lru_h_prompt = jnp.stack([s[4] for s in st_p], axis=0)
    new_lru_h_sample = jnp.stack([s[4] for s in st_s], axis=0)
    return (y_prompt, y_sample, new_cmp_kv_prompt, new_cmp_kv_sample, new_sel_kv_prompt, new_sel_kv_sample,
            new_win_kv_prompt, new_win_kv_sample, new_conv_prompt, new_conv_sample,
            new_lru_h_prompt, new_lru_h_sample)
```

```python
import functools
import math

import numpy as np
import jax
import jax.numpy as jnp
from jax import lax
from jax.experimental import pallas as pl
from jax.experimental.pallas import tpu as pltpu

F32 = jnp.float32
BF16 = jnp.bfloat16

D_MODEL = 1024
HEAD_DIM = 64
N_HEADS = 8
N_KV_HEADS = 2
GROUP = N_HEADS // N_KV_HEADS
ATTN_WIDTH = N_HEADS * HEAD_DIM
KV_WIDTH = N_KV_HEADS * HEAD_DIM
KV_ROW = 2 * KV_WIDTH
CMP_BLOCK = 64
CMP_HIDDEN = 128
N_SEL = 16
WINDOW = 512
ROT_DIM = HEAD_DIM // 4
ROT_HALF = ROT_DIM // 2
ROPE_THETA = 500000.0
RNN_WIDTH = D_MODEL - ATTN_WIDTH
CONV_W = 4
LRU_C = 8.0
D_FF = 2816
PAGE_SIZE = 128
BLOCKS_PER_PAGE = PAGE_SIZE // CMP_BLOCK
ATTN_SCALE = HEAD_DIM ** -0.5
EPS = 1e-6
NEG_INF = -1e30
M_INIT = -1e29
FORCE_SCORE = 1e4
LANES = 128
IN_COLS_PAD = 19 * LANES
GATE_COL = 2 * RNN_WIDTH + ATTN_WIDTH + 3 * KV_ROW
VMEM_LIMIT = 56 * 1024 * 1024


def _cp(sem, vmem=VMEM_LIMIT):
    return pltpu.CompilerParams(dimension_semantics=sem, vmem_limit_bytes=vmem)


def _sds(shape, dt=F32):
    return jax.ShapeDtypeStruct(shape, dt)


def _rms(x, g):
    return x * lax.rsqrt(jnp.mean(x * x, axis=-1, keepdims=True) + EPS) * g


def _bdot(a, b):
    return jnp.dot(a.astype(BF16), b, preferred_element_type=F32)


def _dot_nt(a, b, precision=None):
    return lax.dot_general(a, b, (((1,), (1,)), ((), ())), precision=precision, preferred_element_type=F32)


def _ada_kernel(c_ref, w_ref, b_ref, o_ref):
    c = c_ref[...]
    o_ref[...] = _bdot(c * jax.nn.sigmoid(c), w_ref[...]) + b_ref[...]


def _ada(c_all, w_bf, b):
    n = c_all.shape[0]
    return pl.pallas_call(
        _ada_kernel, grid=(6,),
        in_specs=[pl.BlockSpec((n, D_MODEL), lambda j: (0, 0)),
                  pl.BlockSpec((D_MODEL, D_MODEL), lambda j: (0, j)),
                  pl.BlockSpec((1, D_MODEL), lambda j: (0, j))],
        out_specs=pl.BlockSpec((n, D_MODEL), lambda j: (0, j)),
        out_shape=_sds((n, 6 * D_MODEL)), compiler_params=_cp(("arbitrary",)), name="ada")(c_all, w_bf, b)


def _rope(v, cos, s1, s2):
    w = v.shape[-1]
    return v * cos + pltpu.roll(v, w - ROT_HALF, 1) * s1 + pltpu.roll(v, ROT_HALF, 1) * s2


def _pre_kernel(x_ref, shift_ref, scale_ref, ln_ref, w_ref, cos_ref, s1_ref, s2_ref,
                xr_ref, yr_ref, q_ref, qrot_ref, kvc_ref, kvs_ref, kvw_ref, g_ref):
    x = x_ref[0]
    h = _rms(x, ln_ref[...]) * (1.0 + scale_ref[0]) + shift_ref[0]
    z = _bdot(h, w_ref[...])
    xr_ref[0] = z[:, 0:RNN_WIDTH]
    yr_ref[0] = z[:, RNN_WIDTH:2 * RNN_WIDTH]
    q = z[:, 2 * RNN_WIDTH:2 * RNN_WIDTH + ATTN_WIDTH]
    q_ref[0] = q
    cos, s1, s2 = cos_ref[...], s1_ref[...], s2_ref[...]
    rep = ATTN_WIDTH // LANES
    qrot_ref[0] = _rope(q, jnp.concatenate([cos] * rep, -1), jnp.concatenate([s1] * rep, -1),
                        jnp.concatenate([s2] * rep, -1))
    c0 = 2 * RNN_WIDTH + ATTN_WIDTH
    kvc_ref[0] = z[:, c0:c0 + KV_ROW]
    for i, ref in ((1, kvs_ref), (2, kvw_ref)):
        kv = z[:, c0 + i * KV_ROW:c0 + (i + 1) * KV_ROW]
        ref[0] = jnp.concatenate([_rope(kv[:, :KV_WIDTH], cos, s1, s2), kv[:, KV_WIDTH:]], axis=-1)
    g_ref[0] = jax.nn.sigmoid(z[:, GATE_COL:GATE_COL + LANES])


def _pre(x3, shift, scale, ln, w_in_bf, cos, s1, s2, tm):
    bx, sx, _ = x3.shape
    sm = shift.shape[1]
    if sm == 1:
        mod_spec = pl.BlockSpec((1, 1, D_MODEL), lambda b, t: (b, 0, 0))
    else:
        mod_spec = pl.BlockSpec((1, tm, D_MODEL), lambda b, t: (b, t, 0))
    tab = pl.BlockSpec((tm, LANES), lambda b, t: (t, 0))
    widths = (RNN_WIDTH, RNN_WIDTH, ATTN_WIDTH, ATTN_WIDTH, KV_ROW, KV_ROW, KV_ROW, LANES)
    return pl.pallas_call(
        _pre_kernel, grid=(bx, sx // tm),
        in_specs=[pl.BlockSpec((1, tm, D_MODEL), lambda b, t: (b, t, 0)), mod_spec, mod_spec,
                  pl.BlockSpec((1, D_MODEL), lambda b, t: (0, 0)),
                  pl.BlockSpec((D_MODEL, IN_COLS_PAD), lambda b, t: (0, 0)), tab, tab, tab],
        out_specs=[pl.BlockSpec((1, tm, w), lambda b, t: (b, t, 0)) for w in widths],
        out_shape=[_sds((bx, sx, w)) for w in widths],
        compiler_params=_cp(("arbitrary", "arbitrary")), name="pre")(x3, shift, scale, ln, w_in_bf, cos, s1, s2)


def _lru_gates(xc, wax_ref, bax_ref, lam_ref):
    ra = _bdot(xc, wax_ref[...]) + bax_ref[...]
    r = jax.nn.sigmoid(ra[:, :RNN_WIDTH])
    i = jax.nn.sigmoid(ra[:, RNN_WIDTH:])
    nl = -lam_ref[...]
    softplus = jnp.maximum(nl, 0.0) + jnp.log(1.0 + jnp.exp(-jnp.abs(nl)))
    a = jnp.exp(-LRU_C * r * softplus)
    return a, jnp.sqrt(1.0 - a * a) * (i * xc)


def _scan_rows(a, b):
    t, c = a.shape
    row = lax.broadcasted_iota(jnp.int32, a.shape, 0)
    k = 1
    while k < t:
        if k < 8:
            a_s = jnp.where(row >= k, pltpu.roll(a, k, 0), 1.0)
            b_s = jnp.where(row >= k, pltpu.roll(b, k, 0), 0.0)
        else:
            a_s = jnp.concatenate([jnp.ones((k, c), F32), a[:t - k]], axis=0)
            b_s = jnp.concatenate([jnp.zeros((k, c), F32), b[:t - k]], axis=0)
        b = a * b_s + b
        a = a * a_s
        k *= 2
    return a, b


def _rglru_kernel(xr_ref, yr_ref, cw_ref, cb_ref, wax_ref, bax_ref, lam_ref, o_ref, hl_ref, xbuf, hc, *, t):
    @pl.when(pl.program_id(1) == 0)
    def _():
        xbuf[0:8, :] = jnp.zeros((8, RNN_WIDTH), F32)
        hc[...] = jnp.zeros_like(hc)

    xbuf[8:8 + t, :] = xr_ref[0]
    xc = cb_ref[...]
    for k in range(CONV_W):
        xc = xc + xbuf[5 + k:5 + k + t, :] * cw_ref[k:k + 1, :]
    xbuf[0:8, :] = xbuf[t:t + 8, :]
    a, b = _lru_gates(xc, wax_ref, bax_ref, lam_ref)
    ac, bc = _scan_rows(a, b)
    h = ac * hc[...] + bc
    hc[...] = h[t - 1:t, :]
    hl_ref[0] = h[t - 1:t, :]
    o_ref[0] = h * jax.nn.gelu(yr_ref[0])


def _rglru_prompt(xr, yr, cw, cb, wax, bax, lam, t=256):
    b, s, _ = xr.shape
    tile = pl.BlockSpec((1, t, RNN_WIDTH), lambda i, j: (i, j, 0))
    full = lambda shp: pl.BlockSpec(shp, lambda i, j: (0,) * len(shp))
    return pl.pallas_call(
        functools.partial(_rglru_kernel, t=t), grid=(b, s // t),
        in_specs=[tile, tile, full((CONV_W, RNN_WIDTH)), full((1, RNN_WIDTH)),
                  full((RNN_WIDTH, 2 * RNN_WIDTH)), full((1, 2 * RNN_WIDTH)), full((1, RNN_WIDTH))],
        out_specs=[tile, pl.BlockSpec((1, 1, RNN_WIDTH), lambda i, j: (i, 0, 0))],
        out_shape=[_sds((b, s, RNN_WIDTH)), _sds((b, 1, RNN_WIDTH))],
        scratch_shapes=[pltpu.VMEM((t + 8, RNN_WIDTH), F32), pltpu.VMEM((1, RNN_WIDTH), F32)],
        compiler_params=_cp(("arbitrary", "arbitrary")), name="rglru_prompt")(xr, yr, cw, cb, wax, bax, lam)


def _rglru_sample_kernel(xr_ref, yr_ref, cp_ref, h0_ref, cw_ref, cb_ref, wax_ref, bax_ref, lam_ref,
                         o_ref, hl_ref, *, s, nb):
    xpad = [cp_ref[i] for i in range(CONV_W - 1)] + [xr_ref[i] for i in range(s)]
    xcs = []
    for j in range(s):
        xc = cb_ref[...]
        for k in range(CONV_W):
            xc = xc + xpad[j + k] * cw_ref[k:k + 1, :]
        xcs.append(xc)
    a, b = _lru_gates(jnp.concatenate(xcs, axis=0), wax_ref, bax_ref, lam_ref)
    h = h0_ref[...]
    for j in range(s):
        h = a[j * nb:(j + 1) * nb] * h + b[j * nb:(j + 1) * nb]
        o_ref[j] = h * jax.nn.gelu(yr_ref[j])
    hl_ref[...] = h


def _rglru_sample(xr_t, yr_t, cp_t, h0, cw, cb, wax, bax, lam):
    s, nb, _ = xr_t.shape
    return pl.pallas_call(
        functools.partial(_rglru_sample_kernel, s=s, nb=nb),
        out_shape=[_sds((s, nb, RNN_WIDTH)), _sds((nb, RNN_WIDTH))],
        compiler_params=pltpu.CompilerParams(vmem_limit_bytes=VMEM_LIMIT), name="rglru_sample")(
            xr_t, yr_t, cp_t, h0, cw, cb, wax, bax, lam)


def _build_w1bd(w1_ref, w1bd):
    w1bd[...] = jnp.zeros_like(w1bd)
    for c in range(2):
        for k in range(N_KV_HEADS):
            r0 = c * KV_WIDTH + k * HEAD_DIM
            f0 = (c * N_KV_HEADS + k) * CMP_HIDDEN
            w1bd[:, r0:r0 + HEAD_DIM, f0:f0 + CMP_HIDDEN] = w1_ref[c]


def _compress_blocks(get_rows, m, pe_ref, w1bd, w2_ref, acc):
    acc[...] = jnp.zeros_like(acc)

    def body(l, carry):
        z = get_rows(l) + pe_ref[pl.ds(l, 1), :]
        acc[...] += _bdot(z, w1bd[l])
        return carry

    lax.fori_loop(0, CMP_BLOCK, body, 0)
    return _bdot(jax.nn.gelu(acc[...]), w2_ref[...])


def _compress_prompt_kernel(kv_ref, pe_ref, w1_ref, w2_ref, o_ref, w1bd, acc, *, nblk):
    @pl.when(pl.program_id(0) == 0)
    def _():
        _build_w1bd(w1_ref, w1bd)

    o_ref[0] = _compress_blocks(lambda l: kv_ref[0, :, l, :], nblk, pe_ref, w1bd, w2_ref, acc)


def _compress_prompt(kvc, pe_row, w1_bf, w2bd_bf):
    b, s, _ = kvc.shape
    nblk = s // CMP_BLOCK
    kv4 = kvc.reshape(b, nblk, CMP_BLOCK, KV_ROW)
    return pl.pallas_call(
        functools.partial(_compress_prompt_kernel, nblk=nblk), grid=(b,),
        in_specs=[pl.BlockSpec((1, nblk, CMP_BLOCK, KV_ROW), lambda i: (i, 0, 0, 0)),
                  pl.BlockSpec((CMP_BLOCK, KV_ROW), lambda i: (0, 0)),
                  pl.BlockSpec((2, CMP_BLOCK, HEAD_DIM, CMP_HIDDEN), lambda i: (0, 0, 0, 0)),
                  pl.BlockSpec((4 * CMP_HIDDEN, KV_ROW), lambda i: (0, 0))],
        out_specs=pl.BlockSpec((1, nblk, KV_ROW), lambda i: (i, 0, 0)),
        out_shape=_sds((b, nblk, KV_ROW)),
        scratch_shapes=[pltpu.VMEM((CMP_BLOCK, KV_ROW, 4 * CMP_HIDDEN), BF16),
                        pltpu.VMEM((nblk, 4 * CMP_HIDDEN), F32)],
        compiler_params=_cp(("arbitrary",)), name="compress_prompt")(kv4, pe_row, w1_bf, w2bd_bf)


def _page_copies(pt_ref, pool_ref, buf, sem, b, slot, n_pages, rows_per_page):
    return [pltpu.make_async_copy(pool_ref.at[pt_ref[b, p]],
                                  buf.at[slot, pl.ds(p * rows_per_page, rows_per_page)], sem.at[slot])
            for p in range(n_pages)]


def _paged_prefetch(pt_ref, pool_ref, buf, sem, n_pages, rows_per_page):
    b = pl.program_id(0)
    nb = pl.num_programs(0)
    slot = lax.rem(b, 2)

    @pl.when(b == 0)
    def _():
        for cp in _page_copies(pt_ref, pool_ref, buf, sem, 0, 0, n_pages, rows_per_page):
            cp.start()

    @pl.when(b + 1 < nb)
    def _():
        for cp in _page_copies(pt_ref, pool_ref, buf, sem, b + 1, 1 - slot, n_pages, rows_per_page):
            cp.start()

    for cp in _page_copies(pt_ref, pool_ref, buf, sem, b, slot, n_pages, rows_per_page):
        cp.wait()
    return slot


def _compress_sample_kernel(pt_ref, pool_ref, new_ref, pe_ref, w1_ref, w2_ref, o_ref, buf, sem, w1bd, acc,
                            *, n_pages, m, nbp):
    nblk = n_pages * BLOCKS_PER_PAGE

    @pl.when(pl.program_id(0) == 0)
    def _():
        _build_w1bd(w1_ref, w1bd)
        for s in range(2):
            buf[s, nblk:m] = jnp.zeros((m - nblk, CMP_BLOCK, KV_ROW), F32)

    slot = _paged_prefetch(pt_ref, pool_ref, buf, sem, n_pages, BLOCKS_PER_PAGE)
    buf[slot, nblk, 0:8, :] = new_ref[0]
    ckv = _compress_blocks(lambda l: buf[slot, :, l, :], m, pe_ref, w1bd, w2_ref, acc)
    o_ref[0, 0:m, :] = ckv
    o_ref[0, m:nbp, :] = jnp.zeros((nbp - m, KV_ROW), F32)


def _compress_sample(page_table, pool4, kvc_new8, pe_row, w1_bf, w2bd_bf, nbp):
    b, n_pages = page_table.shape
    nblk = n_pages * BLOCKS_PER_PAGE
    m = nblk + 8
    grid_spec = pltpu.PrefetchScalarGridSpec(
        num_scalar_prefetch=1, grid=(b,),
        in_specs=[pl.BlockSpec(memory_space=pl.ANY),
                  pl.BlockSpec((1, 8, KV_ROW), lambda i, pt: (i, 0, 0)),
                  pl.BlockSpec((CMP_BLOCK, KV_ROW), lambda i, pt: (0, 0)),
                  pl.BlockSpec((2, CMP_BLOCK, HEAD_DIM, CMP_HIDDEN), lambda i, pt: (0, 0, 0, 0)),
                  pl.BlockSpec((4 * CMP_HIDDEN, KV_ROW), lambda i, pt: (0, 0))],
        out_specs=pl.BlockSpec((1, nbp, KV_ROW), lambda i, pt: (i, 0, 0)),
        scratch_shapes=[pltpu.VMEM((2, m, CMP_BLOCK, KV_ROW), F32), pltpu.SemaphoreType.DMA((2,)),
                        pltpu.VMEM((CMP_BLOCK, KV_ROW, 4 * CMP_HIDDEN), BF16),
                        pltpu.VMEM((m, 4 * CMP_HIDDEN), F32)])
    return pl.pallas_call(
        functools.partial(_compress_sample_kernel, n_pages=n_pages, m=m, nbp=nbp),
        grid_spec=grid_spec, out_shape=_sds((b, nbp, KV_ROW)),
        compiler_params=_cp(("arbitrary",)), name="compress_sample")(page_table, pool4, kvc_new8, pe_row, w1_bf, w2bd_bf)


def _arrange_q(q, h, scale):
    kvh = h // GROUP
    src = q[:, (h // 2) * LANES:(h // 2 + 1) * LANES]
    if (h % 2) != kvh:
        src = pltpu.roll(src, HEAD_DIM, 1)
    lane = lax.broadcasted_iota(jnp.int32, src.shape, 1)
    keep = (lane >= kvh * HEAD_DIM) & (lane < (kvh + 1) * HEAD_DIM)
    return jnp.where(keep, src * scale, 0.0)


def _gather_heads(o_list):
    parts = []
    for h, o in enumerate(o_list):
        kvh = h // GROUP
        parts.append(o[:, kvh * HEAD_DIM:(kvh + 1) * HEAD_DIM])
    return jnp.concatenate(parts, axis=-1)


def _select_mask(imp, qpos, nb):
    tq, nbp = imp.shape
    j = lax.broadcasted_iota(jnp.int32, imp.shape, 1)
    cur = qpos // CMP_BLOCK
    forced = (j == 0) | (j == cur) | (j == cur - 1)
    score = jnp.where(j > cur, -FORCE_SCORE, jnp.where(forced, FORCE_SCORE, imp))
    score = jnp.where(j < nb, score, -3e38)
    rank = jnp.zeros(imp.shape, F32)
    for i in range(nb):
        col = score[:, i:i + 1]
        ge = jnp.where(col >= score, 1.0, 0.0)
        gt = jnp.where(col > score, 1.0, 0.0)
        rank = rank + jnp.where(j > i, ge, gt)
    return jnp.where((rank < float(min(N_SEL, nb))) & (j <= cur) & (j < nb), 1.0, 0.0)


def _cmp_kernel(q_ref, ckv_ref, oc_ref, sel_ref, *, tq, nb, nbp, pos0):
    qpos = pos0 + pl.program_id(1) * tq + lax.broadcasted_iota(jnp.int32, (tq, nbp), 0)
    j = lax.broadcasted_iota(jnp.int32, (tq, nbp), 1)
    m = ((j + 1) * CMP_BLOCK - 1 <= qpos) & (j < nb)
    q = q_ref[0]
    ck = ckv_ref[0, :, 0:KV_WIDTH]
    cv = ckv_ref[0, :, KV_WIDTH:KV_ROW].astype(BF16)
    outs = []
    for kvh in range(N_KV_HEADS):
        imp = jnp.zeros((tq, nbp), F32)
        for g in range(GROUP):
            qa = _arrange_q(q, kvh * GROUP + g, ATTN_SCALE)
            s = jnp.where(m, _dot_nt(qa, ck, precision=lax.Precision.HIGHEST), NEG_INF)
            e = jnp.exp(s - jnp.max(s, axis=-1, keepdims=True))
            p = jnp.where(m, e / jnp.sum(e, axis=-1, keepdims=True), 0.0)
            imp = imp + p
            outs.append(jnp.dot(p.astype(BF16), cv, preferred_element_type=F32))
        sel_ref[0, kvh] = _select_mask(imp, qpos, nb)
    oc_ref[0] = _gather_heads(outs)


def _cmp_select(q, ckv_p, nb, pos0, tq):
    b, sq, _ = q.shape
    nbp = ckv_p.shape[1]
    return pl.pallas_call(
        functools.partial(_cmp_kernel, tq=tq, nb=nb, nbp=nbp, pos0=pos0), grid=(b, sq // tq),
        in_specs=[pl.BlockSpec((1, tq, ATTN_WIDTH), lambda i, t: (i, t, 0)),
                  pl.BlockSpec((1, nbp, KV_ROW), lambda i, t: (i, 0, 0))],
        out_specs=[pl.BlockSpec((1, tq, ATTN_WIDTH), lambda i, t: (i, t, 0)),
                   pl.BlockSpec((1, N_KV_HEADS, tq, nbp), lambda i, t: (i, 0, t, 0))],
        out_shape=[_sds((b, sq, ATTN_WIDTH)), _sds((b, N_KV_HEADS, sq, nbp))],
        compiler_params=_cp(("arbitrary", "arbitrary")), name="cmp_select")(q, ckv_p)


def _flash_kernel(q_ref, kv_ref, *rest, tq, tk, nk, mode):
    if mode == "sel":
        sel_ref, e_ref, o_ref, qall, m_s, l_s, acc = rest
    else:
        o_ref, qall, m_s, l_s, acc = rest
    qi = pl.program_id(1)
    kstep = pl.program_id(2)
    ki = kstep if mode == "sel" else qi - (nk - 1) + kstep

    @pl.when(kstep == 0)
    def _():
        q = q_ref[0]
        for h in range(N_HEADS):
            qall[h * tq:(h + 1) * tq, :] = _arrange_q(q, h, ATTN_SCALE).astype(BF16)
        m_s[...] = jnp.full_like(m_s, M_INIT)
        l_s[...] = jnp.zeros_like(l_s)
        acc[...] = jnp.zeros_like(acc)

    if mode == "sel":
        needed = ki * tk <= qi * tq + tq - 1
    else:
        needed = ki >= 0

    @pl.when(needed)
    def _():
        kmat = kv_ref[0, :, 0:KV_WIDTH].astype(BF16)
        vmat = kv_ref[0, :, KV_WIDTH:KV_ROW].astype(BF16)
        d = (lax.broadcasted_iota(jnp.int32, (tq, tk), 1) - lax.broadcasted_iota(jnp.int32, (tq, tk), 0)
             + (ki * tk - qi * tq))
        causal = d <= 0
        s_all = _dot_nt(qall[...], kmat)
        for kvh in range(N_KV_HEADS):
            if mode == "sel":
                ex = jnp.dot(sel_ref[0, kvh].astype(BF16), e_ref[0], preferred_element_type=F32)
                valid = causal & (ex > 0.5)
            else:
                valid = causal & (d > -WINDOW)
            for g in range(GROUP):
                h = kvh * GROUP + g
                rows = slice(h * tq, (h + 1) * tq)
                s = jnp.where(valid, s_all[rows], NEG_INF)
                m_old = m_s[rows]
                m_new = jnp.maximum(m_old, jnp.max(s, axis=-1, keepdims=True))
                alpha = jnp.exp(m_old - m_new)
                p = jnp.exp(s - m_new)
                l_s[rows] = alpha * l_s[rows] + jnp.sum(p, axis=-1, keepdims=True)
                acc[rows] = alpha * acc[rows] + jnp.dot(p.astype(BF16), vmat, preferred_element_type=F32)
                m_s[rows] = m_new

    @pl.when(kstep == nk - 1)
    def _():
        outs = [acc[h * tq:(h + 1) * tq] / l_s[h * tq:(h + 1) * tq] for h in range(N_HEADS)]
        o_ref[0] = _gather_heads(outs)


def _flash_prompt(qrot, kv, sel, emat, mode, tq, tk):
    b, s, _ = qrot.shape
    nq = s // tq
    if mode == "sel":
        nk = s // tk
        kv_idx = lambda i, qi, ks: (i, jnp.minimum(ks, (qi * tq + tq - 1) // tk), 0)
    else:
        assert tq == tk
        nk = WINDOW // tk + 1
        kv_idx = lambda i, qi, ks: (i, jnp.maximum(qi - (nk - 1) + ks, 0), 0)
    in_specs = [pl.BlockSpec((1, tq, ATTN_WIDTH), lambda i, qi, ks: (i, qi, 0)),
                pl.BlockSpec((1, tk, KV_ROW), kv_idx)]
    args = [qrot, kv]
    if mode == "sel":
        nbp = sel.shape[-1]
        in_specs += [pl.BlockSpec((1, N_KV_HEADS, tq, nbp), lambda i, qi, ks: (i, 0, qi, 0)),
                     pl.BlockSpec((1, nbp, tk), lambda i, qi, ks: (jnp.minimum(ks, (qi * tq + tq - 1) // tk), 0, 0))]
        args += [sel, emat]
    return pl.pallas_call(
        functools.partial(_flash_kernel, tq=tq, tk=tk, nk=nk, mode=mode), grid=(b, nq, nk),
        in_specs=in_specs,
        out_specs=pl.BlockSpec((1, tq, ATTN_WIDTH), lambda i, qi, ks: (i, qi, 0)),
        out_shape=_sds((b, s, ATTN_WIDTH)),
        scratch_shapes=[pltpu.VMEM((N_HEADS * tq, LANES), BF16), pltpu.VMEM((N_HEADS * tq, 1), F32),
                        pltpu.VMEM((N_HEADS * tq, 1), F32), pltpu.VMEM((N_HEADS * tq, LANES), F32)],
        compiler_params=_cp(("arbitrary", "arbitrary", "arbitrary")), name="flash_" + mode)(*args)


def _sample_q_rows(q8):
    return jnp.concatenate([_arrange_q(q8, h, ATTN_SCALE) for h in range(N_HEADS)], axis=0).astype(BF16)


def _sample_out(o_all):
    return _gather_heads([o_all[h * 8:(h + 1) * 8] for h in range(N_HEADS)])


def _sel_sample_kernel(pt_ref, pool_ref, q_ref, new_ref, sel_ref, e_ref, o_ref, buf, sem, *, n_pages, pos0, n_new):
    slot = _paged_prefetch(pt_ref, pool_ref, buf, sem, n_pages, PAGE_SIZE)
    nkeys = n_pages * PAGE_SIZE
    nblk = nkeys // CMP_BLOCK
    qall = _sample_q_rows(q_ref[0])
    selrows = jnp.concatenate([sel_ref[0, h // GROUP] for h in range(N_HEADS)], axis=0)
    kmat = buf[slot, :, 0:KV_WIDTH].astype(BF16)
    vmat = buf[slot, :, KV_WIDTH:KV_ROW].astype(BF16)
    qpos = pos0 + lax.rem(lax.broadcasted_iota(jnp.int32, (8 * N_HEADS, 1), 0), 8)
    kpos = lax.broadcasted_iota(jnp.int32, (8 * N_HEADS, nkeys), 1)
    ex = jnp.dot(selrows.astype(BF16), e_ref[...], preferred_element_type=F32)
    s_past = jnp.where((ex > 0.5) & (kpos <= qpos), _dot_nt(qall, kmat), NEG_INF)
    knew = new_ref[0, :, 0:KV_WIDTH].astype(BF16)
    vnew = new_ref[0, :, KV_WIDTH:KV_ROW].astype(BF16)
    inew = lax.broadcasted_iota(jnp.int32, (8 * N_HEADS, 8), 1)
    sel_new = selrows[:, nblk:nblk + 1] > 0.5
    s_new = jnp.where(sel_new & (pos0 + inew <= qpos) & (inew < n_new), _dot_nt(qall, knew), NEG_INF)
    mx = jnp.maximum(jnp.max(s_past, axis=-1, keepdims=True), jnp.max(s_new, axis=-1, keepdims=True))
    p_past = jnp.exp(s_past - mx)
    p_new = jnp.exp(s_new - mx)
    den = jnp.sum(p_past, axis=-1, keepdims=True) + jnp.sum(p_new, axis=-1, keepdims=True)
    o = (jnp.dot(p_past.astype(BF16), vmat, preferred_element_type=F32)
         + jnp.dot(p_new.astype(BF16), vnew, preferred_element_type=F32)) / den
    o_ref[0] = _sample_out(o)


def _sel_sample(page_table, pool3, qrot8, kvs_new8, sel, emat, pos0, n_new):
    b, n_pages = page_table.shape
    nbp = sel.shape[-1]
    nkeys = n_pages * PAGE_SIZE
    grid_spec = pltpu.PrefetchScalarGridSpec(
        num_scalar_prefetch=1, grid=(b,),
        in_specs=[pl.BlockSpec(memory_space=pl.ANY),
                  pl.BlockSpec((1, 8, ATTN_WIDTH), lambda i, pt: (i, 0, 0)),
                  pl.BlockSpec((1, 8, KV_ROW), lambda i, pt: (i, 0, 0)),
                  pl.BlockSpec((1, N_KV_HEADS, 8, nbp), lambda i, pt: (i, 0, 0, 0)),
                  pl.BlockSpec((nbp, nkeys), lambda i, pt: (0, 0))],
        out_specs=pl.BlockSpec((1, 8, ATTN_WIDTH), lambda i, pt: (i, 0, 0)),
        scratch_shapes=[pltpu.VMEM((2, nkeys, KV_ROW), F32), pltpu.SemaphoreType.DMA((2,))])
    return pl.pallas_call(
        functools.partial(_sel_sample_kernel, n_pages=n_pages, pos0=pos0, n_new=n_new),
        grid_spec=grid_spec, out_shape=_sds((b, 8, ATTN_WIDTH)),
        compiler_params=_cp(("arbitrary",)), name="sel_sample")(page_table, pool3, qrot8, kvs_new8, sel, emat)


def _win_sample_kernel(q_ref, buf_ref, new_ref, o_ref, nw_ref, *, wb, n_new):
    qall = _sample_q_rows(q_ref[0])
    buf = buf_ref[0]
    new8 = new_ref[0]
    srow = lax.rem(lax.broadcasted_iota(jnp.int32, (8 * N_HEADS, 1), 0), 8)
    i_past = lax.broadcasted_iota(jnp.int32, (8 * N_HEADS, wb), 1)
    dp = wb + srow - i_past
    s_past = jnp.where((dp >= 0) & (dp < WINDOW), _dot_nt(qall, buf[:, 0:KV_WIDTH].astype(BF16)), NEG_INF)
    i_new = lax.broadcasted_iota(jnp.int32, (8 * N_HEADS, 8), 1)
    dn = srow - i_new
    s_new = jnp.where((dn >= 0) & (dn < WINDOW) & (i_new < n_new),
                      _dot_nt(qall, new8[:, 0:KV_WIDTH].astype(BF16)), NEG_INF)
    mx = jnp.maximum(jnp.max(s_past, axis=-1, keepdims=True), jnp.max(s_new, axis=-1, keepdims=True))
    p_past = jnp.exp(s_past - mx)
    p_new = jnp.exp(s_new - mx)
    den = jnp.sum(p_past, axis=-1, keepdims=True) + jnp.sum(p_new, axis=-1, keepdims=True)
    o = (jnp.dot(p_past.astype(BF16), buf[:, KV_WIDTH:KV_ROW].astype(BF16), preferred_element_type=F32)
         + jnp.dot(p_new.astype(BF16), new8[:, KV_WIDTH:KV_ROW].astype(BF16), preferred_element_type=F32)) / den
    o_ref[0] = _sample_out(o)
    nw_ref[0] = pltpu.roll(buf, wb - n_new, 0)
    row8 = lax.broadcasted_iota(jnp.int32, (8, KV_ROW), 0)
    tail = jnp.where(row8 >= 8 - n_new, pltpu.roll(new8, 8 - n_new, 0), pltpu.roll(buf[wb - 8:wb], 8 - n_new, 0))
    nw_ref[0, wb - 8:wb, :] = tail


def _win_sample(qrot8, win_buf, kvw_new8, n_new):
    b, wb, _ = win_buf.shape
    return pl.pallas_call(
        functools.partial(_win_sample_kernel, wb=wb, n_new=n_new), grid=(b,),
        in_specs=[pl.BlockSpec((1, 8, ATTN_WIDTH), lambda i: (i, 0, 0)),
                  pl.BlockSpec((1, wb, KV_ROW), lambda i: (i, 0, 0)),
                  pl.BlockSpec((1, 8, KV_ROW), lambda i: (i, 0, 0))],
        out_specs=[pl.BlockSpec((1, 8, ATTN_WIDTH), lambda i: (i, 0, 0)),
                   pl.BlockSpec((1, wb, KV_ROW), lambda i: (i, 0, 0))],
        out_shape=[_sds((b, 8, ATTN_WIDTH)), _sds((b, wb, KV_ROW))],
        compiler_params=_cp(("arbitrary",)), name="win_sample")(qrot8, win_buf, kvw_new8)


def _post_kernel(x_ref, or_ref, oc_ref, os_ref, ow_ref, g_ref, m2_ref, m3_ref, m4_ref, m5_ref,
                 gnr_ref, gna_ref, ln2_ref, fin_ref, eg_ref, wo_ref, wg_ref, wu_ref, wd_ref, y_ref, *, ff_chunks):
    g = g_ref[0]
    g_hi = g.astype(BF16)
    g_lo = (g - g_hi.astype(F32)).astype(BF16)
    gx = jnp.dot(jnp.concatenate([g_hi, g_lo], axis=-1), eg_ref[...], preferred_element_type=F32)
    attn = (gx[:, 0:ATTN_WIDTH] * oc_ref[0] + gx[:, ATTN_WIDTH:2 * ATTN_WIDTH] * os_ref[0]
            + gx[:, 2 * ATTN_WIDTH:3 * ATTN_WIDTH] * ow_ref[0])
    mix = (_bdot(_rms(or_ref[0], gnr_ref[...]), wo_ref[0:RNN_WIDTH, :])
           + _bdot(_rms(attn, gna_ref[...]), wo_ref[RNN_WIDTH:D_MODEL, :]))
    x1 = x_ref[0] + m2_ref[0] * mix
    h = (_rms(x1, ln2_ref[...]) * (1.0 + m4_ref[0]) + m3_ref[0]).astype(BF16)
    cw = D_FF // ff_chunks
    ff = jnp.zeros(x1.shape, F32)
    for c in range(ff_chunks):
        gate = jnp.dot(h, wg_ref[:, c * cw:(c + 1) * cw], preferred_element_type=F32)
        up = jnp.dot(h, wu_ref[:, c * cw:(c + 1) * cw], preferred_element_type=F32)
        act = gate * jax.nn.sigmoid(gate) * up
        ff = ff + _bdot(act, wd_ref[c * cw:(c + 1) * cw, :])
    y = x1 + m5_ref[0] * ff
    y_ref[0] = _rms(y, fin_ref[...])


def _post(x3, o_r, o_c, o_s, o_w, g, mods, gnr, gna, ln2, fin, egate, wo, wg, wu, wd, tm, ff_chunks=2):
    bx, sx, _ = x3.shape
    sm = mods[0].shape[1]
    if sm == 1:
        mod_spec = pl.BlockSpec((1, 1, D_MODEL), lambda b, t: (b, 0, 0))
    else:
        mod_spec = pl.BlockSpec((1, tm, D_MODEL), lambda b, t: (b, t, 0))
    tile = lambda w: pl.BlockSpec((1, tm, w), lambda b, t: (b, t, 0))

    def const(shape):
        return pl.BlockSpec(shape, lambda b, t: (0,) * len(shape), pipeline_mode=pl.Buffered(1))

    return pl.pallas_call(
        functools.partial(_post_kernel, ff_chunks=ff_chunks), grid=(bx, sx // tm),
        in_specs=[tile(D_MODEL), tile(RNN_WIDTH), tile(ATTN_WIDTH), tile(ATTN_WIDTH), tile(ATTN_WIDTH), tile(LANES),
                  mod_spec, mod_spec, mod_spec, mod_spec,
                  const((1, RNN_WIDTH)), const((1, ATTN_WIDTH)), const((1, D_MODEL)), const((1, D_MODEL)),
                  const((2 * LANES, 3 * ATTN_WIDTH)), const((D_MODEL, D_MODEL)),
                  const((D_MODEL, D_FF)), const((D_MODEL, D_FF)), const((D_FF, D_MODEL))],
        out_specs=tile(D_MODEL), out_shape=_sds((bx, sx, D_MODEL)),
        compiler_params=_cp(("arbitrary", "arbitrary")), name="post")(
            x3, o_r, o_c, o_s, o_w, g, mods[2], mods[3], mods[4], mods[5], gnr, gna, ln2, fin, egate, wo, wg, wu, wd)


def _rope_tables(pos):
    inv = jnp.exp(jnp.arange(ROT_HALF, dtype=F32) * (-math.log(ROPE_THETA) / ROT_HALF))
    ang = pos.astype(F32)[:, None] * inv[None, :]
    cos, sin = jnp.cos(ang), jnp.sin(ang)
    n = pos.shape[0]
    one = jnp.ones((n, HEAD_DIM - ROT_DIM), F32)
    zero = jnp.zeros((n, HEAD_DIM - ROT_DIM), F32)
    z8 = jnp.zeros((n, ROT_HALF), F32)
    c = jnp.concatenate([cos, cos, one], axis=-1)
    s1 = jnp.concatenate([-sin, z8, zero], axis=-1)
    s2 = jnp.concatenate([z8, sin, zero], axis=-1)
    return tuple(jnp.concatenate([t, t], axis=-1) for t in (c, s1, s2))


def _blockdiag(w):
    nb, d, e = w.shape
    return jnp.einsum('nde,nm->ndme', w, jnp.eye(nb, dtype=w.dtype)).reshape(nb * d, nb * e)


def _prep_params(w_in, lru_wa, lru_wx, lru_ba, lru_bx, cmp_pe, cmp_w1, cmp_w2):
    w_in_p = jnp.pad(w_in, ((0, 0), (0, IN_COLS_PAD - w_in.shape[1]))).astype(BF16)
    wax = jnp.concatenate([_blockdiag(lru_wa), _blockdiag(lru_wx)], axis=1).astype(BF16)
    bax = jnp.concatenate([lru_ba, lru_bx])[None, :]
    pe_row = jnp.broadcast_to(jnp.transpose(cmp_pe, (1, 0, 2))[:, :, None, :],
                              (CMP_BLOCK, 2, N_KV_HEADS, HEAD_DIM)).reshape(CMP_BLOCK, KV_ROW)
    w2bd = _blockdiag(jnp.repeat(cmp_w2, N_KV_HEADS, axis=0)).astype(BF16)
    e = np.zeros((LANES, 3 * ATTN_WIDTH), np.float32)
    for br in range(3):
        for h in range(N_HEADS):
            e[br * N_HEADS + h, br * ATTN_WIDTH + h * HEAD_DIM:br * ATTN_WIDTH + (h + 1) * HEAD_DIM] = 1.0
    egate = jnp.asarray(np.concatenate([e, e], axis=0), BF16)
    return w_in_p, wax, bax, pe_row, cmp_w1.astype(BF16), w2bd, egate


def _expand_matrix(nbp, n_tiles, tk):
    blk = (np.arange(n_tiles)[:, None, None] * tk + np.arange(tk)[None, None, :]) // CMP_BLOCK
    return jnp.asarray(blk == np.arange(nbp)[None, :, None], BF16)


def _layer_prompt(x, mods, P, tm=512, tq=256):
    b, s, _ = x.shape
    cos, s1, s2 = _rope_tables(jnp.arange(s))
    xr, yr, q, qrot, kvc, kvs, kvw, g = _pre(x, mods[0], mods[1], P['ln1'], P['w_in'], cos, s1, s2, tm)
    o_r, h_new = _rglru_prompt(xr, yr, P['conv_w'], P['conv_b'], P['wax'], P['bax'], P['lam'])
    nb = s // CMP_BLOCK
    ckv = _compress_prompt(kvc, P['pe_row'], P['w1'], P['w2bd'])
    nbp = LANES * (-(-nb // LANES))
    ckv_p = jnp.pad(ckv, ((0, 0), (0, nbp - nb), (0, 0)))
    o_c, sel = _cmp_select(q, ckv_p, nb, 0, min(128, s))
    tk_sel = min(512, s)
    o_s = _flash_prompt(qrot, kvs, sel, _expand_matrix(nbp, s // tk_sel, tk_sel), "sel", tq, tk_sel)
    o_w = _flash_prompt(qrot, kvw, None, None, "win", tq, tq)
    y = _post(x, o_r, o_c, o_s, o_w, g, mods, P['gn_rnn'], P['gn_attn'], P['ln2'], P['final_g'], P['egate'],
              P['w_out'], P['w_gate'], P['w_up'], P['w_down'], tm)
    wlen = min(WINDOW, s)
    return y, (kvc, kvs, kvw[:, s - wlen:], xr[:, s - (CONV_W - 1):], h_new[:, 0])


def _pad_rows8(a):
    return jnp.pad(a, ((0, 0), (0, 8 - a.shape[1]), (0, 0)))


def _layer_sample(x, mods_tok, cmp_pool, sel_pool, win_buf, conv_buf, h0, page_table, P):
    b, s, _ = x.shape
    n_pages = page_table.shape[1]
    past = n_pages * PAGE_SIZE
    nb = past // CMP_BLOCK + 1
    pos = past + jnp.arange(s)
    cos, s1, s2 = (jnp.tile(t, (b, 1)) for t in _rope_tables(pos))
    n = b * s
    flat = lambda a: a.reshape(1, n, a.shape[-1])
    outs = _pre(flat(x), mods_tok[0], mods_tok[1], P['ln1'], P['w_in'], cos, s1, s2, min(512, n))
    xr, yr, q, qrot, kvc, kvs, kvw, g = (o.reshape(b, s, o.shape[-1]) for o in outs)
    tmaj = lambda a: jnp.transpose(a, (1, 0, 2))
    o_r_t, h_new = _rglru_sample(tmaj(xr), tmaj(yr), tmaj(conv_buf), h0, P['conv_w'], P['conv_b'],
                                 P['wax'], P['bax'], P['lam'])
    o_r = tmaj(o_r_t)
    nbp = LANES * (-(-nb // LANES))
    pool4 = cmp_pool.reshape(cmp_pool.shape[0], BLOCKS_PER_PAGE, CMP_BLOCK, KV_ROW)
    ckv_p = _compress_sample(page_table, pool4, _pad_rows8(kvc), P['pe_row'], P['w1'], P['w2bd'], nbp)
    q8, qrot8 = _pad_rows8(q), _pad_rows8(qrot)
    o_c8, sel = _cmp_select(q8, ckv_p, nb, past, 8)
    pool3 = sel_pool.reshape(sel_pool.shape[0], PAGE_SIZE, KV_ROW)
    o_s8 = _sel_sample(page_table, pool3, qrot8, _pad_rows8(kvs), sel, _expand_matrix(nbp, 1, past)[0], past, s)
    o_w8, new_win = _win_sample(qrot8, win_buf.reshape(b, win_buf.shape[1], KV_ROW), _pad_rows8(kvw), s)
    y = _post(flat(x), flat(o_r), flat(o_c8[:, :s]), flat(o_s8[:, :s]), flat(o_w8[:, :s]), flat(g), mods_tok,
              P['gn_rnn'], P['gn_attn'], P['ln2'], P['final_g'], P['egate'],
              P['w_out'], P['w_gate'], P['w_up'], P['w_down'], min(512, n))
    return y.reshape(b, s, D_MODEL), (kvc, kvs, new_win, xr[:, s - (CONV_W - 1):], h_new)


def kernel(x_prompt, x_sample, cache_cmp_kv, cache_sel_kv, state_win_kv, state_conv, state_lru_h, page_table,
           c_prompt, c_sample, ln1_g, ln2_g, w_ada, b_ada, w_in, conv_w, conv_b, lru_wa, lru_ba, lru_wx, lru_bx,
           lru_lambda, cmp_pe, cmp_w1, cmp_w2, gn_rnn, gn_attn, w_out, w_gate, w_up, w_down, final_g):
    depth = w_in.shape[0]
    assert depth == 1
    bp = x_prompt.shape[0]
    bs, ss, _ = x_sample.shape
    l = 0
    w_in_p, wax, bax, pe_row, w1, w2bd, egate = _prep_params(
        w_in[l], lru_wa[l], lru_wx[l], lru_ba[l], lru_bx[l], cmp_pe[l], cmp_w1[l], cmp_w2[l])
    P = {'ln1': ln1_g[l][None], 'ln2': ln2_g[l][None], 'w_in': w_in_p, 'conv_w': conv_w[l], 'conv_b': conv_b[l][None],
         'wax': wax, 'bax': bax, 'lam': lru_lambda[l][None], 'pe_row': pe_row, 'w1': w1, 'w2bd': w2bd,
         'gn_rnn': gn_rnn[l][None], 'gn_attn': gn_attn[l][None], 'egate': egate, 'final_g': final_g[None],
         'w_out': w_out[l].astype(BF16), 'w_gate': w_gate[l].astype(BF16), 'w_up': w_up[l].astype(BF16),
         'w_down': w_down[l].astype(BF16)}
    mods_all = _ada(jnp.concatenate([c_prompt, c_sample], axis=0), w_ada[l].astype(BF16), b_ada[l][None])
    mods_p = [m[:, None, :] for m in jnp.split(mods_all[:bp], 6, axis=-1)]
    mods_s = [jnp.repeat(m, ss, axis=0)[None] for m in jnp.split(mods_all[bp:], 6, axis=-1)]

    yp, st_p = _layer_prompt(x_prompt, mods_p, P)
    ys, st_s = _layer_sample(x_sample, mods_s, cache_cmp_kv[l], cache_sel_kv[l], state_win_kv[l],
                             state_conv[l], state_lru_h[l], page_table, P)

    def kv5(a):
        return a.reshape(a.shape[0], a.shape[1], 2, N_KV_HEADS, HEAD_DIM)[None]

    return (yp, ys, kv5(st_p[0]), kv5(st_s[0]), kv5(st_p[1]), kv5(st_s[1]), kv5(st_p[2]), kv5(st_s[2]),
            st_p[3][None], st_s[3][None], st_p[4][None], st_s[4][None])
```

```python
import functools
import math

import numpy as np
import jax
import jax.numpy as jnp
from jax import lax
from jax.experimental import pallas as pl
from jax.experimental.pallas import tpu as pltpu

F32 = jnp.float32
BF16 = jnp.bfloat16

D_MODEL = 1024
HEAD_DIM = 64
N_HEADS = 8
N_KV_HEADS = 2
GROUP = N_HEADS // N_KV_HEADS
ATTN_WIDTH = N_HEADS * HEAD_DIM
KV_WIDTH = N_KV_HEADS * HEAD_DIM
KV_ROW = 2 * KV_WIDTH
CMP_BLOCK = 64
CMP_HIDDEN = 128
N_SEL = 16
WINDOW = 512
ROT_DIM = HEAD_DIM // 4
ROT_HALF = ROT_DIM // 2
ROPE_THETA = 500000.0
RNN_WIDTH = D_MODEL - ATTN_WIDTH
CONV_W = 4
LRU_C = 8.0
D_FF = 2816
PAGE_SIZE = 128
BLOCKS_PER_PAGE = PAGE_SIZE // CMP_BLOCK
ATTN_SCALE = HEAD_DIM ** -0.5
EPS = 1e-6
NEG_INF = -1e30
M_INIT = -1e29
FORCE_SCORE = 1e4
LANES = 128
IN_COLS_PAD = 19 * LANES
GATE_COL = 2 * RNN_WIDTH + ATTN_WIDTH + 3 * KV_ROW
VMEM_LIMIT = 56 * 1024 * 1024


def _cp(sem, vmem=VMEM_LIMIT):
    return pltpu.CompilerParams(dimension_semantics=sem, vmem_limit_bytes=vmem)


def _sds(shape, dt=F32):
    return jax.ShapeDtypeStruct(shape, dt)


def _rms(x, g):
    return x * lax.rsqrt(jnp.mean(x * x, axis=-1, keepdims=True) + EPS) * g


def _bdot(a, b):
    return jnp.dot(a.astype(BF16), b, preferred_element_type=F32)


def _dot_nt(a, b, precision=None):
    return lax.dot_general(a, b, (((1,), (1,)), ((), ())), precision=precision, preferred_element_type=F32)


def _ada_kernel(c_ref, w_ref, b_ref, o_ref):
    c = c_ref[...]
    o_ref[...] = _bdot(c * jax.nn.sigmoid(c), w_ref[...]) + b_ref[...]


def _ada(c_all, w_bf, b):
    n = c_all.shape[0]
    return pl.pallas_call(
        _ada_kernel, grid=(6,),
        in_specs=[pl.BlockSpec((n, D_MODEL), lambda j: (0, 0)),
                  pl.BlockSpec((D_MODEL, D_MODEL), lambda j: (0, j)),
                  pl.BlockSpec((1, D_MODEL), lambda j: (0, j))],
        out_specs=pl.BlockSpec((n, D_MODEL), lambda j: (0, j)),
        out_shape=_sds((n, 6 * D_MODEL)), compiler_params=_cp(("arbitrary",)), name="ada")(c_all, w_bf, b)


def _rope(v, cos, s1, s2):
    w = v.shape[-1]
    return v * cos + pltpu.roll(v, w - ROT_HALF, 1) * s1 + pltpu.roll(v, ROT_HALF, 1) * s2


def _pre_kernel(x_ref, shift_ref, scale_ref, ln_ref, w_ref, cos_ref, s1_ref, s2_ref,
                xr_ref, yr_ref, q_ref, qrot_ref, kvc_ref, kvct_ref, kvst_ref, kvwt_ref, g_ref):
    x = x_ref[0]
    h = _rms(x, ln_ref[...]) * (1.0 + scale_ref[0]) + shift_ref[0]
    z = _bdot(h, w_ref[...])
    xr_ref[0] = z[:, 0:RNN_WIDTH]
    yr_ref[0] = z[:, RNN_WIDTH:2 * RNN_WIDTH]
    q = z[:, 2 * RNN_WIDTH:2 * RNN_WIDTH + ATTN_WIDTH]
    q_ref[0] = q
    cos, s1, s2 = cos_ref[...], s1_ref[...], s2_ref[...]
    rep = ATTN_WIDTH // LANES
    qrot_ref[0] = _rope(q, jnp.concatenate([cos] * rep, -1), jnp.concatenate([s1] * rep, -1),
                        jnp.concatenate([s2] * rep, -1))
    c0 = 2 * RNN_WIDTH + ATTN_WIDTH
    kvc = z[:, c0:c0 + KV_ROW]
    kvc_ref[0] = kvc
    kvct_ref[0] = kvc.T
    for i, ref in ((1, kvst_ref), (2, kvwt_ref)):
        kv = z[:, c0 + i * KV_ROW:c0 + (i + 1) * KV_ROW]
        ref[0, 0:KV_WIDTH, :] = _rope(kv[:, :KV_WIDTH], cos, s1, s2).T
        ref[0, KV_WIDTH:KV_ROW, :] = kv[:, KV_WIDTH:].T
    g_ref[0] = jax.nn.sigmoid(z[:, GATE_COL:GATE_COL + LANES])


def _pre(x3, shift, scale, ln, w_in_bf, cos, s1, s2, tm):
    bx, sx, _ = x3.shape
    sm = shift.shape[1]
    if sm == 1:
        mod_spec = pl.BlockSpec((1, 1, D_MODEL), lambda b, t: (b, 0, 0))
    else:
        mod_spec = pl.BlockSpec((1, tm, D_MODEL), lambda b, t: (b, t, 0))
    tab = pl.BlockSpec((tm, LANES), lambda b, t: (t, 0))
    row = lambda w: (pl.BlockSpec((1, tm, w), lambda b, t: (b, t, 0)), _sds((bx, sx, w)))
    col = (pl.BlockSpec((1, KV_ROW, tm), lambda b, t: (b, 0, t)), _sds((bx, KV_ROW, sx)))
    outs = [row(RNN_WIDTH), row(RNN_WIDTH), row(ATTN_WIDTH), row(ATTN_WIDTH), row(KV_ROW), col, col, col, row(LANES)]
    return pl.pallas_call(
        _pre_kernel, grid=(bx, sx // tm),
        in_specs=[pl.BlockSpec((1, tm, D_MODEL), lambda b, t: (b, t, 0)), mod_spec, mod_spec,
                  pl.BlockSpec((1, D_MODEL), lambda b, t: (0, 0)),
                  pl.BlockSpec((D_MODEL, IN_COLS_PAD), lambda b, t: (0, 0)), tab, tab, tab],
        out_specs=[o[0] for o in outs], out_shape=[o[1] for o in outs],
        compiler_params=_cp(("arbitrary", "arbitrary")), name="pre")(x3, shift, scale, ln, w_in_bf, cos, s1, s2)


def _lru_gates(xc, wax_ref, bax_ref, lam_ref):
    ra = _bdot(xc, wax_ref[...]) + bax_ref[...]
    r = jax.nn.sigmoid(ra[:, :RNN_WIDTH])
    i = jax.nn.sigmoid(ra[:, RNN_WIDTH:])
    nl = -lam_ref[...]
    softplus = jnp.maximum(nl, 0.0) + jnp.log(1.0 + jnp.exp(-jnp.abs(nl)))
    a = jnp.exp(-LRU_C * r * softplus)
    return a, jnp.sqrt(1.0 - a * a) * (i * xc)


def _scan_rows(a, b):
    t, c = a.shape
    row = lax.broadcasted_iota(jnp.int32, a.shape, 0)
    k = 1
    while k < t:
        if k < 8:
            a_s = jnp.where(row >= k, pltpu.roll(a, k, 0), 1.0)
            b_s = jnp.where(row >= k, pltpu.roll(b, k, 0), 0.0)
        else:
            a_s = jnp.concatenate([jnp.ones((k, c), F32), a[:t - k]], axis=0)
            b_s = jnp.concatenate([jnp.zeros((k, c), F32), b[:t - k]], axis=0)
        b = a * b_s + b
        a = a * a_s
        k *= 2
    return a, b


def _rglru_kernel(xr_ref, yr_ref, cw_ref, cb_ref, wax_ref, bax_ref, lam_ref, o_ref, hl_ref, xbuf, hc, *, t):
    @pl.when(pl.program_id(1) == 0)
    def _():
        xbuf[0:8, :] = jnp.zeros((8, RNN_WIDTH), F32)
        hc[...] = jnp.zeros_like(hc)

    xbuf[8:8 + t, :] = xr_ref[0]
    xc = cb_ref[...]
    for k in range(CONV_W):
        xc = xc + xbuf[5 + k:5 + k + t, :] * cw_ref[k:k + 1, :]
    xbuf[0:8, :] = xbuf[t:t + 8, :]
    a, b = _lru_gates(xc, wax_ref, bax_ref, lam_ref)
    ac, bc = _scan_rows(a, b)
    h = ac * hc[...] + bc
    hc[...] = h[t - 1:t, :]
    hl_ref[0] = h[t - 1:t, :]
    o_ref[0] = h * jax.nn.gelu(yr_ref[0])


def _rglru_prompt(xr, yr, cw, cb, wax, bax, lam, t=256):
    b, s, _ = xr.shape
    tile = pl.BlockSpec((1, t, RNN_WIDTH), lambda i, j: (i, j, 0))
    full = lambda shp: pl.BlockSpec(shp, lambda i, j: (0,) * len(shp))
    return pl.pallas_call(
        functools.partial(_rglru_kernel, t=t), grid=(b, s // t),
        in_specs=[tile, tile, full((CONV_W, RNN_WIDTH)), full((1, RNN_WIDTH)),
                  full((RNN_WIDTH, 2 * RNN_WIDTH)), full((1, 2 * RNN_WIDTH)), full((1, RNN_WIDTH))],
        out_specs=[tile, pl.BlockSpec((1, 1, RNN_WIDTH), lambda i, j: (i, 0, 0))],
        out_shape=[_sds((b, s, RNN_WIDTH)), _sds((b, 1, RNN_WIDTH))],
        scratch_shapes=[pltpu.VMEM((t + 8, RNN_WIDTH), F32), pltpu.VMEM((1, RNN_WIDTH), F32)],
        compiler_params=_cp(("arbitrary", "arbitrary")), name="rglru_prompt")(xr, yr, cw, cb, wax, bax, lam)


def _rglru_sample_kernel(xr_ref, yr_ref, cp_ref, h0_ref, cw_ref, cb_ref, wax_ref, bax_ref, lam_ref,
                         o_ref, hl_ref, *, s, nb):
    xpad = [cp_ref[i] for i in range(CONV_W - 1)] + [xr_ref[i] for i in range(s)]
    xcs = []
    for j in range(s):
        xc = cb_ref[...]
        for k in range(CONV_W):
            xc = xc + xpad[j + k] * cw_ref[k:k + 1, :]
        xcs.append(xc)
    a, b = _lru_gates(jnp.concatenate(xcs, axis=0), wax_ref, bax_ref, lam_ref)
    h = h0_ref[...]
    for j in range(s):
        h = a[j * nb:(j + 1) * nb] * h + b[j * nb:(j + 1) * nb]
        o_ref[j] = h * jax.nn.gelu(yr_ref[j])
    hl_ref[...] = h


def _rglru_sample(xr_t, yr_t, cp_t, h0, cw, cb, wax, bax, lam):
    s, nb, _ = xr_t.shape
    return pl.pallas_call(
        functools.partial(_rglru_sample_kernel, s=s, nb=nb),
        out_shape=[_sds((s, nb, RNN_WIDTH)), _sds((nb, RNN_WIDTH))],
        compiler_params=pltpu.CompilerParams(vmem_limit_bytes=VMEM_LIMIT), name="rglru_sample")(
            xr_t, yr_t, cp_t, h0, cw, cb, wax, bax, lam)


def _build_w1bd(w1_ref, w1bd):
    w1bd[...] = jnp.zeros_like(w1bd)
    for c in range(2):
        for k in range(N_KV_HEADS):
            r0 = c * KV_WIDTH + k * HEAD_DIM
            f0 = (c * N_KV_HEADS + k) * CMP_HIDDEN
            w1bd[:, r0:r0 + HEAD_DIM, f0:f0 + CMP_HIDDEN] = w1_ref[c]


def _compress_blocks(get_rows, m, pe_ref, w1bd, w2_ref, acc):
    acc[...] = jnp.zeros_like(acc)

    def body(l, carry):
        z = get_rows(l) + pe_ref[pl.ds(l, 1), :]
        acc[...] += _bdot(z, w1bd[l])
        return carry

    lax.fori_loop(0, CMP_BLOCK, body, 0)
    return _bdot(jax.nn.gelu(acc[...]), w2_ref[...])


def _compress_prompt_kernel(kv_ref, pe_ref, w1_ref, w2_ref, o_ref, w1bd, acc, *, nblk):
    @pl.when(pl.program_id(0) == 0)
    def _():
        _build_w1bd(w1_ref, w1bd)

    o_ref[0] = _compress_blocks(lambda l: kv_ref[0, :, l, :], nblk, pe_ref, w1bd, w2_ref, acc)


def _compress_prompt(kvc, pe_row, w1_bf, w2bd_bf):
    b, s, _ = kvc.shape
    nblk = s // CMP_BLOCK
    kv4 = kvc.reshape(b, nblk, CMP_BLOCK, KV_ROW)
    return pl.pallas_call(
        functools.partial(_compress_prompt_kernel, nblk=nblk), grid=(b,),
        in_specs=[pl.BlockSpec((1, nblk, CMP_BLOCK, KV_ROW), lambda i: (i, 0, 0, 0)),
                  pl.BlockSpec((CMP_BLOCK, KV_ROW), lambda i: (0, 0)),
                  pl.BlockSpec((2, CMP_BLOCK, HEAD_DIM, CMP_HIDDEN), lambda i: (0, 0, 0, 0)),
                  pl.BlockSpec((4 * CMP_HIDDEN, KV_ROW), lambda i: (0, 0))],
        out_specs=pl.BlockSpec((1, nblk, KV_ROW), lambda i: (i, 0, 0)),
        out_shape=_sds((b, nblk, KV_ROW)),
        scratch_shapes=[pltpu.VMEM((CMP_BLOCK, KV_ROW, 4 * CMP_HIDDEN), BF16),
                        pltpu.VMEM((nblk, 4 * CMP_HIDDEN), F32)],
        compiler_params=_cp(("arbitrary",)), name="compress_prompt")(kv4, pe_row, w1_bf, w2bd_bf)


def _page_copies(pt_ref, pool_ref, dst, sem, b, slot, n_pages):
    return [pltpu.make_async_copy(pool_ref.at[pt_ref[b, p]], dst(slot, p), sem.at[slot]) for p in range(n_pages)]


def _paged_prefetch(pt_ref, pool_ref, dst, sem, n_pages):
    b = pl.program_id(0)
    nb = pl.num_programs(0)
    slot = lax.rem(b, 2)

    @pl.when(b == 0)
    def _():
        for cp in _page_copies(pt_ref, pool_ref, dst, sem, 0, 0, n_pages):
            cp.start()

    @pl.when(b + 1 < nb)
    def _():
        for cp in _page_copies(pt_ref, pool_ref, dst, sem, b + 1, 1 - slot, n_pages):
            cp.start()

    for cp in _page_copies(pt_ref, pool_ref, dst, sem, b, slot, n_pages):
        cp.wait()
    return slot


def _compress_sample_kernel(pt_ref, pool_ref, new_ref, pe_ref, w1_ref, w2_ref, o_ref,
                            stage, sem, rows, w1s, acc, *, n_pages, m, nbp):
    nkeys = n_pages * PAGE_SIZE

    @pl.when(pl.program_id(0) == 0)
    def _():
        w1s[...] = jnp.zeros_like(w1s)
        for c in range(2):
            for k in range(N_KV_HEADS):
                w1s[:, c, k * HEAD_DIM:(k + 1) * HEAD_DIM, k * CMP_HIDDEN:(k + 1) * CMP_HIDDEN] = w1_ref[c]
            rows[c, nkeys:m * CMP_BLOCK, :] = jnp.zeros((m * CMP_BLOCK - nkeys, KV_WIDTH), F32)

    slot = _paged_prefetch(pt_ref, pool_ref, lambda s, p: stage.at[s, p], sem, n_pages)

    def to_rows(p, carry):
        r0 = pl.multiple_of(p * PAGE_SIZE, PAGE_SIZE)
        for c in range(2):
            rows[c, pl.ds(r0, PAGE_SIZE), :] = stage[slot, p, c * KV_WIDTH:(c + 1) * KV_WIDTH, :].T
        return carry

    lax.fori_loop(0, n_pages, to_rows, 0)
    for c in range(2):
        rows[c, nkeys:nkeys + 8, :] = new_ref[0, :, c * KV_WIDTH:(c + 1) * KV_WIDTH]
    acc[...] = jnp.zeros_like(acc)

    def body(l, carry):
        pe = pe_ref[pl.ds(l, 1), :]
        for c in range(2):
            z = rows[c, pl.ds(l, m, stride=CMP_BLOCK), :] + pe[:, c * KV_WIDTH:(c + 1) * KV_WIDTH]
            acc[c] += _bdot(z, w1s[l, c])
        return carry

    lax.fori_loop(0, CMP_BLOCK, body, 0)
    hw = N_KV_HEADS * CMP_HIDDEN
    for c in range(2):
        o_ref[0, 0:m, c * KV_WIDTH:(c + 1) * KV_WIDTH] = _bdot(
            jax.nn.gelu(acc[c]), w2_ref[c * hw:(c + 1) * hw, c * KV_WIDTH:(c + 1) * KV_WIDTH])
    o_ref[0, m:nbp, :] = jnp.zeros((nbp - m, KV_ROW), F32)


def _compress_sample(page_table, pool_t, kvc_new8, pe_row, w1_bf, w2bd_bf, nbp):
    b, n_pages = page_table.shape
    m = n_pages * BLOCKS_PER_PAGE + 8
    grid_spec = pltpu.PrefetchScalarGridSpec(
        num_scalar_prefetch=1, grid=(b,),
        in_specs=[pl.BlockSpec(memory_space=pl.ANY),
                  pl.BlockSpec((1, 8, KV_ROW), lambda i, pt: (i, 0, 0)),
                  pl.BlockSpec((CMP_BLOCK, KV_ROW), lambda i, pt: (0, 0)),
                  pl.BlockSpec((2, CMP_BLOCK, HEAD_DIM, CMP_HIDDEN), lambda i, pt: (0, 0, 0, 0)),
                  pl.BlockSpec((4 * CMP_HIDDEN, KV_ROW), lambda i, pt: (0, 0))],
        out_specs=pl.BlockSpec((1, nbp, KV_ROW), lambda i, pt: (i, 0, 0)),
        scratch_shapes=[pltpu.VMEM((2, n_pages, KV_ROW, PAGE_SIZE), F32), pltpu.SemaphoreType.DMA((2,)),
                        pltpu.VMEM((2, m * CMP_BLOCK, KV_WIDTH), F32),
                        pltpu.VMEM((CMP_BLOCK, 2, KV_WIDTH, N_KV_HEADS * CMP_HIDDEN), BF16),
                        pltpu.VMEM((2, m, N_KV_HEADS * CMP_HIDDEN), F32)])
    return pl.pallas_call(
        functools.partial(_compress_sample_kernel, n_pages=n_pages, m=m, nbp=nbp),
        grid_spec=grid_spec, out_shape=_sds((b, nbp, KV_ROW)),
        compiler_params=_cp(("arbitrary",)), name="compress_sample")(page_table, pool_t, kvc_new8, pe_row, w1_bf, w2bd_bf)


def _arrange_q(q, h, scale):
    kvh = h // GROUP
    src = q[:, (h // 2) * LANES:(h // 2 + 1) * LANES]
    if (h % 2) != kvh:
        src = pltpu.roll(src, HEAD_DIM, 1)
    lane = lax.broadcasted_iota(jnp.int32, src.shape, 1)
    keep = (lane >= kvh * HEAD_DIM) & (lane < (kvh + 1) * HEAD_DIM)
    return jnp.where(keep, src * scale, 0.0)


def _gather_heads(o_list):
    parts = []
    for h, o in enumerate(o_list):
        kvh = h // GROUP
        parts.append(o[:, kvh * HEAD_DIM:(kvh + 1) * HEAD_DIM])
    return jnp.concatenate(parts, axis=-1)


def _select_mask(imp, qpos, nb):
    tq, nbp = imp.shape
    j = lax.broadcasted_iota(jnp.int32, imp.shape, 1)
    cur = qpos // CMP_BLOCK
    forced = (j == 0) | (j == cur) | (j == cur - 1)
    score = jnp.where(j > cur, -FORCE_SCORE, jnp.where(forced, FORCE_SCORE, imp))
    score = jnp.where(j < nb, score, -3e38)
    rank = jnp.zeros(imp.shape, F32)
    for i in range(nb):
        col = score[:, i:i + 1]
        ge = jnp.where(col >= score, 1.0, 0.0)
        gt = jnp.where(col > score, 1.0, 0.0)
        rank = rank + jnp.where(j > i, ge, gt)
    return jnp.where((rank < float(min(N_SEL, nb))) & (j <= cur) & (j < nb), 1.0, 0.0)


def _cmp_kernel(q_ref, ckv_ref, oc_ref, sel_ref, *, tq, nb, nbp, pos0):
    qpos = pos0 + pl.program_id(1) * tq + lax.broadcasted_iota(jnp.int32, (tq, nbp), 0)
    j = lax.broadcasted_iota(jnp.int32, (tq, nbp), 1)
    m = ((j + 1) * CMP_BLOCK - 1 <= qpos) & (j < nb)
    q = q_ref[0]
    ck = ckv_ref[0, :, 0:KV_WIDTH]
    cv = ckv_ref[0, :, KV_WIDTH:KV_ROW].astype(BF16)
    outs = []
    for kvh in range(N_KV_HEADS):
        imp = jnp.zeros((tq, nbp), F32)
        for g in range(GROUP):
            qa = _arrange_q(q, kvh * GROUP + g, ATTN_SCALE)
            s = jnp.where(m, _dot_nt(qa, ck, precision=lax.Precision.HIGHEST), NEG_INF)
            e = jnp.exp(s - jnp.max(s, axis=-1, keepdims=True))
            p = jnp.where(m, e / jnp.sum(e, axis=-1, keepdims=True), 0.0)
            imp = imp + p
            outs.append(jnp.dot(p.astype(BF16), cv, preferred_element_type=F32))
        sel_ref[0, kvh] = _select_mask(imp, qpos, nb)
    oc_ref[0] = _gather_heads(outs)


def _cmp_select(q, ckv_p, nb, pos0, tq):
    b, sq, _ = q.shape
    nbp = ckv_p.shape[1]
    return pl.pallas_call(
        functools.partial(_cmp_kernel, tq=tq, nb=nb, nbp=nbp, pos0=pos0), grid=(b, sq // tq),
        in_specs=[pl.BlockSpec((1, tq, ATTN_WIDTH), lambda i, t: (i, t, 0)),
                  pl.BlockSpec((1, nbp, KV_ROW), lambda i, t: (i, 0, 0))],
        out_specs=[pl.BlockSpec((1, tq, ATTN_WIDTH), lambda i, t: (i, t, 0)),
                   pl.BlockSpec((1, N_KV_HEADS, tq, nbp), lambda i, t: (i, 0, t, 0))],
        out_shape=[_sds((b, sq, ATTN_WIDTH)), _sds((b, N_KV_HEADS, sq, nbp))],
        compiler_params=_cp(("arbitrary", "arbitrary")), name="cmp_select")(q, ckv_p)


def _select_mask_t(imp, qpos, nb):
    j = lax.broadcasted_iota(jnp.int32, imp.shape, 0)
    cur = qpos // CMP_BLOCK
    forced = (j == 0) | (j == cur) | (j == cur - 1)
    score = jnp.where(j > cur, -FORCE_SCORE, jnp.where(forced, FORCE_SCORE, imp))
    sub = 8
    ngrp = nb // sub
    groups = [score[r * sub:(r + 1) * sub] for r in range(ngrp)]
    jsub = lax.broadcasted_iota(jnp.int32, groups[0].shape, 0)
    ranks = [jnp.zeros(groups[0].shape, F32) for _ in range(ngrp)]
    for i in range(nb):
        row = score[i:i + 1]
        for r in range(ngrp):
            if r * sub > i:
                hit = row >= groups[r]
            elif (r + 1) * sub - 1 <= i:
                hit = row > groups[r]
            else:
                hit = ((jsub + r * sub > i) & (row >= groups[r])) | (row > groups[r])
            ranks[r] = ranks[r] + jnp.where(hit, 1.0, 0.0)
    rank = jnp.concatenate(ranks, axis=0)
    return jnp.where((rank < float(min(N_SEL, nb))) & (j <= cur), 1.0, 0.0)


def _cmp_kernel_t(q_ref, ckv_ref, oc_ref, sel_ref, *, tq, nb, nbp, pos0):
    qpos = pos0 + pl.program_id(1) * tq + lax.broadcasted_iota(jnp.int32, (nb, tq), 1)
    j = lax.broadcasted_iota(jnp.int32, (nb, tq), 0)
    m = (j + 1) * CMP_BLOCK - 1 <= qpos
    q = q_ref[0]
    ck = ckv_ref[0, :, 0:KV_WIDTH]
    cv = ckv_ref[0, :, KV_WIDTH:KV_ROW].astype(BF16)
    outs = []
    for kvh in range(N_KV_HEADS):
        imp = jnp.zeros((nb, tq), F32)
        for g in range(GROUP):
            qa = _arrange_q(q, kvh * GROUP + g, ATTN_SCALE)
            s = jnp.where(m, _dot_nt(ck, qa, precision=lax.Precision.HIGHEST), NEG_INF)
            e = jnp.exp(s - jnp.max(s, axis=0, keepdims=True))
            p = jnp.where(m, e / jnp.sum(e, axis=0, keepdims=True), 0.0)
            imp = imp + p
            outs.append(lax.dot_general(p.astype(BF16), cv, (((0,), (0,)), ((), ())), preferred_element_type=F32))
        sel_t = _select_mask_t(imp, qpos, nb)
        sel_ref[0, kvh] = jnp.concatenate([sel_t, jnp.zeros((nbp - nb, tq), F32)], axis=0).T
    oc_ref[0] = _gather_heads(outs)


def _cmp_select_t(q, ckv, nbp, pos0, tq):
    b, sq, _ = q.shape
    nb = ckv.shape[1]
    return pl.pallas_call(
        functools.partial(_cmp_kernel_t, tq=tq, nb=nb, nbp=nbp, pos0=pos0), grid=(b, sq // tq),
        in_specs=[pl.BlockSpec((1, tq, ATTN_WIDTH), lambda i, t: (i, t, 0)),
                  pl.BlockSpec((1, nb, KV_ROW), lambda i, t: (i, 0, 0))],
        out_specs=[pl.BlockSpec((1, tq, ATTN_WIDTH), lambda i, t: (i, t, 0)),
                   pl.BlockSpec((1, N_KV_HEADS, tq, nbp), lambda i, t: (i, 0, t, 0))],
        out_shape=[_sds((b, sq, ATTN_WIDTH)), _sds((b, N_KV_HEADS, sq, nbp))],
        compiler_params=_cp(("arbitrary", "arbitrary")), name="cmp_select_t")(q, ckv)


def _flash_kernel(q_ref, kv_ref, *rest, tq, tk, nk, mode):
    if mode == "sel":
        sel_ref, e_ref, o_ref, qall, m_s, l_s, acc = rest
    else:
        o_ref, qall, m_s, l_s, acc = rest
    qi = pl.program_id(1)
    kstep = pl.program_id(2)
    ki = kstep if mode == "sel" else qi - (nk - 1) + kstep

    @pl.when(kstep == 0)
    def _():
        q = q_ref[0]
        for h in range(N_HEADS):
            qall[h * tq:(h + 1) * tq, :] = _arrange_q(q, h, ATTN_SCALE).astype(BF16)
        m_s[...] = jnp.full_like(m_s, M_INIT)
        l_s[...] = jnp.zeros_like(l_s)
        acc[...] = jnp.zeros_like(acc)

    if mode == "sel":
        needed = ki * tk <= qi * tq + tq - 1
    else:
        needed = ki >= 0

    @pl.when(needed)
    def _():
        kt = kv_ref[0, 0:KV_WIDTH, :].astype(BF16)
        vt = kv_ref[0, KV_WIDTH:KV_ROW, :].astype(BF16)
        d = (lax.broadcasted_iota(jnp.int32, (tq, tk), 1) - lax.broadcasted_iota(jnp.int32, (tq, tk), 0)
             + (ki * tk - qi * tq))
        causal = d <= 0
        s_all = jnp.dot(qall[...], kt, preferred_element_type=F32)
        for kvh in range(N_KV_HEADS):
            if mode == "sel":
                ex = jnp.dot(sel_ref[0, kvh].astype(BF16), e_ref[0], preferred_element_type=F32)
                valid = causal & (ex > 0.5)
            else:
                valid = causal & (d > -WINDOW)
            for g in range(GROUP):
                h = kvh * GROUP + g
                rows = slice(h * tq, (h + 1) * tq)
                s = jnp.where(valid, s_all[rows], NEG_INF)
                m_old = m_s[rows]
                m_new = jnp.maximum(m_old, jnp.max(s, axis=-1, keepdims=True))
                alpha = jnp.exp(m_old - m_new)
                p = jnp.exp(s - m_new)
                l_s[rows] = alpha * l_s[rows] + jnp.sum(p, axis=-1, keepdims=True)
                acc[rows] = alpha * acc[rows] + _dot_nt(p.astype(BF16), vt)
                m_s[rows] = m_new

    @pl.when(kstep == nk - 1)
    def _():
        outs = [acc[h * tq:(h + 1) * tq] / l_s[h * tq:(h + 1) * tq] for h in range(N_HEADS)]
        o_ref[0] = _gather_heads(outs)


def _flash_prompt(qrot, kv_t, sel, emat, mode, tq, tk):
    b, s, _ = qrot.shape
    nq = s // tq
    if mode == "sel":
        nk = s // tk
        kv_idx = lambda i, qi, ks: (i, 0, jnp.minimum(ks, (qi * tq + tq - 1) // tk))
    else:
        assert tq == tk
        nk = WINDOW // tk + 1
        kv_idx = lambda i, qi, ks: (i, 0, jnp.maximum(qi - (nk - 1) + ks, 0))
    in_specs = [pl.BlockSpec((1, tq, ATTN_WIDTH), lambda i, qi, ks: (i, qi, 0)),
                pl.BlockSpec((1, KV_ROW, tk), kv_idx)]
    args = [qrot, kv_t]
    if mode == "sel":
        nbp = sel.shape[-1]
        in_specs += [pl.BlockSpec((1, N_KV_HEADS, tq, nbp), lambda i, qi, ks: (i, 0, qi, 0)),
                     pl.BlockSpec((1, nbp, tk), lambda i, qi, ks: (jnp.minimum(ks, (qi * tq + tq - 1) // tk), 0, 0))]
        args += [sel, emat]
    return pl.pallas_call(
        functools.partial(_flash_kernel, tq=tq, tk=tk, nk=nk, mode=mode), grid=(b, nq, nk),
        in_specs=in_specs,
        out_specs=pl.BlockSpec((1, tq, ATTN_WIDTH), lambda i, qi, ks: (i, qi, 0)),
        out_shape=_sds((b, s, ATTN_WIDTH)),
        scratch_shapes=[pltpu.VMEM((N_HEADS * tq, LANES), BF16), pltpu.VMEM((N_HEADS * tq, 1), F32),
                        pltpu.VMEM((N_HEADS * tq, 1), F32), pltpu.VMEM((N_HEADS * tq, LANES), F32)],
        compiler_params=_cp(("arbitrary", "arbitrary", "arbitrary")), name="flash_" + mode)(*args)


def _sample_q_rows(q8):
    return jnp.concatenate([_arrange_q(q8, h, ATTN_SCALE) for h in range(N_HEADS)], axis=0).astype(BF16)


def _sample_out(o_all):
    return _gather_heads([o_all[h * 8:(h + 1) * 8] for h in range(N_HEADS)])


def _two_part_attention(s_past, s_new, vt_past, vt_new):
    mx = jnp.maximum(jnp.max(s_past, axis=-1, keepdims=True), jnp.max(s_new, axis=-1, keepdims=True))
    p_past = jnp.exp(s_past - mx)
    p_new = jnp.exp(s_new - mx)
    den = jnp.sum(p_past, axis=-1, keepdims=True) + jnp.sum(p_new, axis=-1, keepdims=True)
    return (_dot_nt(p_past.astype(BF16), vt_past) + _dot_nt(p_new.astype(BF16), vt_new)) / den


def _sel_sample_kernel(pt_ref, pool_ref, q_ref, newt_ref, sel_ref, e_ref, o_ref, buf, sem, *, n_pages, pos0, n_new):
    slot = _paged_prefetch(pt_ref, pool_ref, lambda s, p: buf.at[s, :, pl.ds(p * PAGE_SIZE, PAGE_SIZE)], sem, n_pages)
    nkeys = n_pages * PAGE_SIZE
    nblk = nkeys // CMP_BLOCK
    nrow = 8 * N_HEADS
    qall = _sample_q_rows(q_ref[0])
    selrows = jnp.concatenate([sel_ref[0, h // GROUP] for h in range(N_HEADS)], axis=0)
    qpos = pos0 + lax.rem(lax.broadcasted_iota(jnp.int32, (nrow, 1), 0), 8)
    kpos = lax.broadcasted_iota(jnp.int32, (nrow, nkeys), 1)
    ex = jnp.dot(selrows.astype(BF16), e_ref[...], preferred_element_type=F32)
    s_past = jnp.dot(qall, buf[slot, 0:KV_WIDTH, :].astype(BF16), preferred_element_type=F32)
    s_past = jnp.where((ex > 0.5) & (kpos <= qpos), s_past, NEG_INF)
    newt = newt_ref[0].astype(BF16)
    inew = lax.broadcasted_iota(jnp.int32, (nrow, LANES), 1) - (LANES - n_new)
    sel_new = selrows[:, nblk:nblk + 1] > 0.5
    s_new = jnp.dot(qall, newt[0:KV_WIDTH], preferred_element_type=F32)
    s_new = jnp.where(sel_new & (inew >= 0) & (pos0 + inew <= qpos), s_new, NEG_INF)
    o = _two_part_attention(s_past, s_new, buf[slot, KV_WIDTH:KV_ROW, :].astype(BF16), newt[KV_WIDTH:KV_ROW])
    o_ref[0] = _sample_out(o)


def _sel_sample(page_table, pool_t, qrot8, kvs_newt, sel, emat, pos0, n_new):
    b, n_pages = page_table.shape
    nbp = sel.shape[-1]
    nkeys = n_pages * PAGE_SIZE
    grid_spec = pltpu.PrefetchScalarGridSpec(
        num_scalar_prefetch=1, grid=(b,),
        in_specs=[pl.BlockSpec(memory_space=pl.ANY),
                  pl.BlockSpec((1, 8, ATTN_WIDTH), lambda i, pt: (i, 0, 0)),
                  pl.BlockSpec((1, KV_ROW, LANES), lambda i, pt: (i, 0, 0)),
                  pl.BlockSpec((1, N_KV_HEADS, 8, nbp), lambda i, pt: (i, 0, 0, 0)),
                  pl.BlockSpec((nbp, nkeys), lambda i, pt: (0, 0))],
        out_specs=pl.BlockSpec((1, 8, ATTN_WIDTH), lambda i, pt: (i, 0, 0)),
        scratch_shapes=[pltpu.VMEM((2, KV_ROW, nkeys), F32), pltpu.SemaphoreType.DMA((2,))])
    return pl.pallas_call(
        functools.partial(_sel_sample_kernel, n_pages=n_pages, pos0=pos0, n_new=n_new),
        grid_spec=grid_spec, out_shape=_sds((b, 8, ATTN_WIDTH)),
        compiler_params=_cp(("arbitrary",)), name="sel_sample")(page_table, pool_t, qrot8, kvs_newt, sel, emat)


def _win_sample_kernel(q_ref, buf_ref, newt_ref, o_ref, nw_ref, *, wb, n_new):
    nrow = 8 * N_HEADS
    qall = _sample_q_rows(q_ref[0])
    buf = buf_ref[0]
    newt = newt_ref[0]
    srow = lax.rem(lax.broadcasted_iota(jnp.int32, (nrow, 1), 0), 8)
    i_past = lax.broadcasted_iota(jnp.int32, (nrow, wb), 1)
    dp = wb + srow - i_past
    s_past = jnp.dot(qall, buf[0:KV_WIDTH].astype(BF16), preferred_element_type=F32)
    s_past = jnp.where((dp >= 0) & (dp < WINDOW), s_past, NEG_INF)
    i_new = lax.broadcasted_iota(jnp.int32, (nrow, LANES), 1) - (LANES - n_new)
    dn = srow - i_new
    s_new = jnp.dot(qall, newt[0:KV_WIDTH].astype(BF16), preferred_element_type=F32)
    s_new = jnp.where((i_new >= 0) & (dn >= 0) & (dn < WINDOW), s_new, NEG_INF)
    o = _two_part_attention(s_past, s_new, buf[KV_WIDTH:KV_ROW].astype(BF16), newt[KV_WIDTH:KV_ROW].astype(BF16))
    o_ref[0] = _sample_out(o)
    nw_ref[0] = pltpu.roll(buf, wb - n_new, 1)
    lane = lax.broadcasted_iota(jnp.int32, (KV_ROW, LANES), 1)
    nw_ref[0, :, wb - LANES:wb] = jnp.where(lane >= LANES - n_new, newt,
                                            pltpu.roll(buf[:, wb - LANES:wb], LANES - n_new, 1))


def _win_sample(qrot8, win_t, kvw_newt, n_new):
    b, _, wb = win_t.shape
    return pl.pallas_call(
        functools.partial(_win_sample_kernel, wb=wb, n_new=n_new), grid=(b,),
        in_specs=[pl.BlockSpec((1, 8, ATTN_WIDTH), lambda i: (i, 0, 0)),
                  pl.BlockSpec((1, KV_ROW, wb), lambda i: (i, 0, 0)),
                  pl.BlockSpec((1, KV_ROW, LANES), lambda i: (i, 0, 0))],
        out_specs=[pl.BlockSpec((1, 8, ATTN_WIDTH), lambda i: (i, 0, 0)),
                   pl.BlockSpec((1, KV_ROW, wb), lambda i: (i, 0, 0))],
        out_shape=[_sds((b, 8, ATTN_WIDTH)), _sds((b, KV_ROW, wb))],
        compiler_params=_cp(("arbitrary",)), name="win_sample")(qrot8, win_t, kvw_newt)


def _post_kernel(x_ref, or_ref, oc_ref, os_ref, ow_ref, g_ref, m2_ref, m3_ref, m4_ref, m5_ref,
                 gnr_ref, gna_ref, ln2_ref, fin_ref, eg_ref, wo_ref, wg_ref, wu_ref, wd_ref, y_ref, *, ff_chunks):
    g = g_ref[0]
    g_hi = g.astype(BF16)
    g_lo = (g - g_hi.astype(F32)).astype(BF16)
    gx = jnp.dot(jnp.concatenate([g_hi, g_lo], axis=-1), eg_ref[...], preferred_element_type=F32)
    attn = (gx[:, 0:ATTN_WIDTH] * oc_ref[0] + gx[:, ATTN_WIDTH:2 * ATTN_WIDTH] * os_ref[0]
            + gx[:, 2 * ATTN_WIDTH:3 * ATTN_WIDTH] * ow_ref[0])
    mix = (_bdot(_rms(or_ref[0], gnr_ref[...]), wo_ref[0:RNN_WIDTH, :])
           + _bdot(_rms(attn, gna_ref[...]), wo_ref[RNN_WIDTH:D_MODEL, :]))
    x1 = x_ref[0] + m2_ref[0] * mix
    h = (_rms(x1, ln2_ref[...]) * (1.0 + m4_ref[0]) + m3_ref[0]).astype(BF16)
    cw = D_FF // ff_chunks
    ff = jnp.zeros(x1.shape, F32)
    for c in range(ff_chunks):
        gate = jnp.dot(h, wg_ref[:, c * cw:(c + 1) * cw], preferred_element_type=F32)
        up = jnp.dot(h, wu_ref[:, c * cw:(c + 1) * cw], preferred_element_type=F32)
        act = gate * jax.nn.sigmoid(gate) * up
        ff = ff + _bdot(act, wd_ref[c * cw:(c + 1) * cw, :])
    y = x1 + m5_ref[0] * ff
    y_ref[0] = _rms(y, fin_ref[...])


def _post(x3, o_r, o_c, o_s, o_w, g, mods, gnr, gna, ln2, fin, egate, wo, wg, wu, wd, tm, ff_chunks=2):
    bx, sx, _ = x3.shape
    sm = mods[0].shape[1]
    if sm == 1:
        mod_spec = pl.BlockSpec((1, 1, D_MODEL), lambda b, t: (b, 0, 0))
    else:
        mod_spec = pl.BlockSpec((1, tm, D_MODEL), lambda b, t: (b, t, 0))
    tile = lambda w: pl.BlockSpec((1, tm, w), lambda b, t: (b, t, 0))

    def const(shape):
        return pl.BlockSpec(shape, lambda b, t: (0,) * len(shape), pipeline_mode=pl.Buffered(1))

    return pl.pallas_call(
        functools.partial(_post_kernel, ff_chunks=ff_chunks), grid=(bx, sx // tm),
        in_specs=[tile(D_MODEL), tile(RNN_WIDTH), tile(ATTN_WIDTH), tile(ATTN_WIDTH), tile(ATTN_WIDTH), tile(LANES),
                  mod_spec, mod_spec, mod_spec, mod_spec,
                  const((1, RNN_WIDTH)), const((1, ATTN_WIDTH)), const((1, D_MODEL)), const((1, D_MODEL)),
                  const((2 * LANES, 3 * ATTN_WIDTH)), const((D_MODEL, D_MODEL)),
                  const((D_MODEL, D_FF)), const((D_MODEL, D_FF)), const((D_FF, D_MODEL))],
        out_specs=tile(D_MODEL), out_shape=_sds((bx, sx, D_MODEL)),
        compiler_params=_cp(("arbitrary", "arbitrary")), name="post")(
            x3, o_r, o_c, o_s, o_w, g, mods[2], mods[3], mods[4], mods[5], gnr, gna, ln2, fin, egate, wo, wg, wu, wd)


def _rope_tables(pos):
    inv = jnp.exp(jnp.arange(ROT_HALF, dtype=F32) * (-math.log(ROPE_THETA) / ROT_HALF))
    ang = pos.astype(F32)[:, None] * inv[None, :]
    cos, sin = jnp.cos(ang), jnp.sin(ang)
    n = pos.shape[0]
    one = jnp.ones((n, HEAD_DIM - ROT_DIM), F32)
    zero = jnp.zeros((n, HEAD_DIM - ROT_DIM), F32)
    z8 = jnp.zeros((n, ROT_HALF), F32)
    c = jnp.concatenate([cos, cos, one], axis=-1)
    s1 = jnp.concatenate([-sin, z8, zero], axis=-1)
    s2 = jnp.concatenate([z8, sin, zero], axis=-1)
    return tuple(jnp.concatenate([t, t], axis=-1) for t in (c, s1, s2))


def _blockdiag(w):
    nb, d, e = w.shape
    return jnp.einsum('nde,nm->ndme', w, jnp.eye(nb, dtype=w.dtype)).reshape(nb * d, nb * e)


def _prep_params(w_in, lru_wa, lru_wx, lru_ba, lru_bx, cmp_pe, cmp_w1, cmp_w2):
    w_in_p = jnp.pad(w_in, ((0, 0), (0, IN_COLS_PAD - w_in.shape[1]))).astype(BF16)
    wax = jnp.concatenate([_blockdiag(lru_wa), _blockdiag(lru_wx)], axis=1).astype(BF16)
    bax = jnp.concatenate([lru_ba, lru_bx])[None, :]
    pe_row = jnp.broadcast_to(jnp.transpose(cmp_pe, (1, 0, 2))[:, :, None, :],
                              (CMP_BLOCK, 2, N_KV_HEADS, HEAD_DIM)).reshape(CMP_BLOCK, KV_ROW)
    w2bd = _blockdiag(jnp.repeat(cmp_w2, N_KV_HEADS, axis=0)).astype(BF16)
    e = np.zeros((LANES, 3 * ATTN_WIDTH), np.float32)
    for br in range(3):
        for h in range(N_HEADS):
            e[br * N_HEADS + h, br * ATTN_WIDTH + h * HEAD_DIM:br * ATTN_WIDTH + (h + 1) * HEAD_DIM] = 1.0
    egate = jnp.asarray(np.concatenate([e, e], axis=0), BF16)
    return w_in_p, wax, bax, pe_row, cmp_w1.astype(BF16), w2bd, egate


def _expand_matrix(nbp, n_tiles, tk):
    blk = (np.arange(n_tiles)[:, None, None] * tk + np.arange(tk)[None, None, :]) // CMP_BLOCK
    return jnp.asarray(blk == np.arange(nbp)[None, :, None], BF16)


def _kv_from_t(kv_t):
    n, _, s = kv_t.shape
    return jnp.transpose(kv_t.reshape(n, 2, N_KV_HEADS, HEAD_DIM, s), (0, 4, 1, 2, 3))


def _kv_to_t(kv):
    n, s = kv.shape[:2]
    return jnp.transpose(kv, (0, 2, 3, 4, 1)).reshape(n, KV_ROW, s)


def _layer_prompt(x, mods, P, tm=512, tq=256):
    b, s, _ = x.shape
    cos, s1, s2 = _rope_tables(jnp.arange(s))
    xr, yr, q, qrot, kvc, kvc_t, kvs_t, kvw_t, g = _pre(x, mods[0], mods[1], P['ln1'], P['w_in'], cos, s1, s2, tm)
    o_r, h_new = _rglru_prompt(xr, yr, P['conv_w'], P['conv_b'], P['wax'], P['bax'], P['lam'])
    nb = s // CMP_BLOCK
    ckv = _compress_prompt(kvc, P['pe_row'], P['w1'], P['w2bd'])
    nbp = LANES * (-(-nb // LANES))
    o_c, sel = _cmp_select_t(q, ckv, nbp, 0, min(128, s))
    tk_sel = min(512, s)
    o_s = _flash_prompt(qrot, kvs_t, sel, _expand_matrix(nbp, s // tk_sel, tk_sel), "sel", tq, tk_sel)
    o_w = _flash_prompt(qrot, kvw_t, None, None, "win", tq, tq)
    y = _post(x, o_r, o_c, o_s, o_w, g, mods, P['gn_rnn'], P['gn_attn'], P['ln2'], P['final_g'], P['egate'],
              P['w_out'], P['w_gate'], P['w_up'], P['w_down'], tm)
    wlen = min(WINDOW, s)
    return y, (_kv_from_t(kvc_t), _kv_from_t(kvs_t), _kv_from_t(kvw_t[:, :, s - wlen:]),
               xr[:, s - (CONV_W - 1):], h_new[:, 0])


def _pad_rows8(a):
    return jnp.pad(a, ((0, 0), (0, 8 - a.shape[1]), (0, 0)))


def _layer_sample(x, mods_tok, cmp_pool, sel_pool, win_buf, conv_buf, h0, page_table, P):
    b, s, _ = x.shape
    n_pages = page_table.shape[1]
    past = n_pages * PAGE_SIZE
    nb = past // CMP_BLOCK + 1
    pos = past + jnp.arange(s)
    cos, s1, s2 = (jnp.tile(t, (b, 1)) for t in _rope_tables(pos))
    n = b * s
    flat = lambda a: a.reshape(1, n, a.shape[-1])
    outs = _pre(flat(x), mods_tok[0], mods_tok[1], P['ln1'], P['w_in'], cos, s1, s2, min(512, n))
    xr, yr, q, qrot, kvc = (o.reshape(b, s, o.shape[-1]) for o in outs[:5])
    kvc_t, kvs_t, kvw_t = (jnp.transpose(o[0].reshape(KV_ROW, b, s), (1, 0, 2)) for o in outs[5:8])
    lane_pad = lambda a: jnp.pad(a, ((0, 0), (0, 0), (LANES - s, 0)))
    g = outs[8]
    tmaj = lambda a: jnp.transpose(a, (1, 0, 2))
    o_r_t, h_new = _rglru_sample(tmaj(xr), tmaj(yr), tmaj(conv_buf), h0, P['conv_w'], P['conv_b'],
                                 P['wax'], P['bax'], P['lam'])
    o_r = tmaj(o_r_t)
    nbp = LANES * (-(-nb // LANES))
    ckv_p = _compress_sample(page_table, _kv_to_t(cmp_pool), _pad_rows8(kvc), P['pe_row'], P['w1'], P['w2bd'], nbp)
    q8, qrot8 = _pad_rows8(q), _pad_rows8(qrot)
    o_c8, sel = _cmp_select(q8, ckv_p, nb, past, 8)
    o_s8 = _sel_sample(page_table, _kv_to_t(sel_pool), qrot8, lane_pad(kvs_t), sel,
                       _expand_matrix(nbp, 1, past)[0], past, s)
    o_w8, new_win_t = _win_sample(qrot8, _kv_to_t(win_buf), lane_pad(kvw_t), s)
    y = _post(flat(x), flat(o_r), flat(o_c8[:, :s]), flat(o_s8[:, :s]), flat(o_w8[:, :s]), g, mods_tok,
              P['gn_rnn'], P['gn_attn'], P['ln2'], P['final_g'], P['egate'],
              P['w_out'], P['w_gate'], P['w_up'], P['w_down'], min(512, n))
    return y.reshape(b, s, D_MODEL), (_kv_from_t(kvc_t), _kv_from_t(kvs_t), _kv_from_t(new_win_t),
                                      xr[:, s - (CONV_W - 1):], h_new)


def kernel(x_prompt, x_sample, cache_cmp_kv, cache_sel_kv, state_win_kv, state_conv, state_lru_h, page_table,
           c_prompt, c_sample, ln1_g, ln2_g, w_ada, b_ada, w_in, conv_w, conv_b, lru_wa, lru_ba, lru_wx, lru_bx,
           lru_lambda, cmp_pe, cmp_w1, cmp_w2, gn_rnn, gn_attn, w_out, w_gate, w_up, w_down, final_g):
    depth = w_in.shape[0]
    assert depth == 1
    bp = x_prompt.shape[0]
    bs, ss, _ = x_sample.shape
    l = 0
    w_in_p, wax, bax, pe_row, w1, w2bd, egate = _prep_params(
        w_in[l], lru_wa[l], lru_wx[l], lru_ba[l], lru_bx[l], cmp_pe[l], cmp_w1[l], cmp_w2[l])
    P = {'ln1': ln1_g[l][None], 'ln2': ln2_g[l][None], 'w_in': w_in_p, 'conv_w': conv_w[l], 'conv_b': conv_b[l][None],
         'wax': wax, 'bax': bax, 'lam': lru_lambda[l][None], 'pe_row': pe_row, 'w1': w1, 'w2bd': w2bd,
         'gn_rnn': gn_rnn[l][None], 'gn_attn': gn_attn[l][None], 'egate': egate, 'final_g': final_g[None],
         'w_out': w_out[l].astype(BF16), 'w_gate': w_gate[l].astype(BF16), 'w_up': w_up[l].astype(BF16),
         'w_down': w_down[l].astype(BF16)}
    mods_all = _ada(jnp.concatenate([c_prompt, c_sample], axis=0), w_ada[l].astype(BF16), b_ada[l][None])
    mods_p = [m[:, None, :] for m in jnp.split(mods_all[:bp], 6, axis=-1)]
    mods_s = [jnp.repeat(m, ss, axis=0)[None] for m in jnp.split(mods_all[bp:], 6, axis=-1)]

    yp, st_p = _layer_prompt(x_prompt, mods_p, P)
    ys, st_s = _layer_sample(x_sample, mods_s, cache_cmp_kv[l], cache_sel_kv[l], state_win_kv[l],
                             state_conv[l], state_lru_h[l], page_table, P)

    return (yp, ys, st_p[0][None], st_s[0][None], st_p[1][None], st_s[1][None], st_p[2][None], st_s[2][None],
            st_p[3][None], st_s[3][None], st_p[4][None], st_s[4][None])
```

```python
import functools
import math

import numpy as np
import jax
import jax.numpy as jnp
from jax import lax
from jax.experimental import pallas as pl
from jax.experimental.pallas import tpu as pltpu

F32 = jnp.float32
BF16 = jnp.bfloat16

D_MODEL = 1024
HEAD_DIM = 64
N_HEADS = 8
N_KV_HEADS = 2
GROUP = N_HEADS // N_KV_HEADS
ATTN_WIDTH = N_HEADS * HEAD_DIM
KV_WIDTH = N_KV_HEADS * HEAD_DIM
KV_ROW = 2 * KV_WIDTH
CMP_BLOCK = 64
CMP_HIDDEN = 128
N_SEL = 16
WINDOW = 512
ROT_DIM = HEAD_DIM // 4
ROT_HALF = ROT_DIM // 2
ROPE_THETA = 500000.0
RNN_WIDTH = D_MODEL - ATTN_WIDTH
CONV_W = 4
LRU_C = 8.0
D_FF = 2816
PAGE_SIZE = 128
BLOCKS_PER_PAGE = PAGE_SIZE // CMP_BLOCK
ROW_PITCH = CMP_BLOCK + 8
LSTEP = 8
ATTN_SCALE = HEAD_DIM ** -0.5
LOG2_E = math.log2(math.e)
EPS = 1e-6
NEG_INF = -1e30
M_INIT = -1e29
FORCE_SCORE = 1e4
LANES = 128
IN_COLS_PAD = 19 * LANES
GATE_COL = 2 * RNN_WIDTH + ATTN_WIDTH + 3 * KV_ROW
VMEM_LIMIT = 56 * 1024 * 1024


def _cp(sem, vmem=VMEM_LIMIT):
    return pltpu.CompilerParams(dimension_semantics=sem, vmem_limit_bytes=vmem)


def _sds(shape, dt=F32):
    return jax.ShapeDtypeStruct(shape, dt)


def _rms(x, g):
    return x * lax.rsqrt(jnp.mean(x * x, axis=-1, keepdims=True) + EPS) * g


def _bdot(a, b):
    return jnp.dot(a.astype(BF16), b, preferred_element_type=F32)


def _dot_nt(a, b, precision=None):
    return lax.dot_general(a, b, (((1,), (1,)), ((), ())), precision=precision, preferred_element_type=F32)


def _ada_kernel(c_ref, w_ref, b_ref, o_ref):
    c = c_ref[...]
    o_ref[...] = _bdot(c * jax.nn.sigmoid(c), w_ref[...]) + b_ref[...]


def _ada(c_all, w_bf, b):
    n = c_all.shape[0]
    return pl.pallas_call(
        _ada_kernel, grid=(6,),
        in_specs=[pl.BlockSpec((n, D_MODEL), lambda j: (0, 0)),
                  pl.BlockSpec((D_MODEL, D_MODEL), lambda j: (0, j)),
                  pl.BlockSpec((1, D_MODEL), lambda j: (0, j))],
        out_specs=pl.BlockSpec((n, D_MODEL), lambda j: (0, j)),
        out_shape=_sds((n, 6 * D_MODEL)), compiler_params=_cp(("arbitrary",)), name="ada")(c_all, w_bf, b)


def _rope(v, cos, s1, s2):
    w = v.shape[-1]
    return v * cos + pltpu.roll(v, w - ROT_HALF, 1) * s1 + pltpu.roll(v, ROT_HALF, 1) * s2


def _pre_kernel(x_ref, shift_ref, scale_ref, ln_ref, w_ref, cos_ref, s1_ref, s2_ref,
                xr_ref, yr_ref, q_ref, qrot_ref, kvc_ref, ks_ref, kw_ref, kvct_ref, kvst_ref, kvwt_ref, g_ref):
    x = x_ref[0]
    h = _rms(x, ln_ref[...]) * (1.0 + scale_ref[0]) + shift_ref[0]
    z = _bdot(h, w_ref[...])
    xr_ref[0] = z[:, 0:RNN_WIDTH]
    yr_ref[0] = z[:, RNN_WIDTH:2 * RNN_WIDTH]
    q = z[:, 2 * RNN_WIDTH:2 * RNN_WIDTH + ATTN_WIDTH]
    q_ref[0] = q
    cos, s1, s2 = cos_ref[...], s1_ref[...], s2_ref[...]
    rep = ATTN_WIDTH // LANES
    qrot_ref[0] = _rope(q, jnp.concatenate([cos] * rep, -1), jnp.concatenate([s1] * rep, -1),
                        jnp.concatenate([s2] * rep, -1))
    c0 = 2 * RNN_WIDTH + ATTN_WIDTH
    kvc = z[:, c0:c0 + KV_ROW]
    kvc_ref[0] = kvc
    kvct_ref[0] = kvc.T
    tm = x.shape[0]
    for i, kref, ref in ((1, ks_ref, kvst_ref), (2, kw_ref, kvwt_ref)):
        kv = z[:, c0 + i * KV_ROW:c0 + (i + 1) * KV_ROW]
        keys = _rope(kv[:, :KV_WIDTH], cos, s1, s2)
        if i == 1:
            blk = (pl.program_id(1) * tm + lax.broadcasted_iota(jnp.int32, (tm, HEAD_DIM), 0)) // CMP_BLOCK
            onehot = jnp.where(lax.broadcasted_iota(jnp.int32, (tm, HEAD_DIM), 1) == blk, 1.0, 0.0)
            kref[0] = jnp.concatenate([keys[:, :HEAD_DIM], onehot, keys[:, HEAD_DIM:], onehot], axis=-1)
        else:
            kref[0] = keys
        ref[0, 0:KV_WIDTH, :] = keys.T
        ref[0, KV_WIDTH:KV_ROW, :] = kv[:, KV_WIDTH:].T
    g_ref[0] = jax.nn.sigmoid(z[:, GATE_COL:GATE_COL + LANES])


def _pre(x3, shift, scale, ln, w_in_bf, cos, s1, s2, tm):
    bx, sx, _ = x3.shape
    sm = shift.shape[1]
    if sm == 1:
        mod_spec = pl.BlockSpec((1, 1, D_MODEL), lambda b, t: (b, 0, 0))
    else:
        mod_spec = pl.BlockSpec((1, tm, D_MODEL), lambda b, t: (b, t, 0))
    tab = pl.BlockSpec((tm, LANES), lambda b, t: (t, 0))
    row = lambda w: (pl.BlockSpec((1, tm, w), lambda b, t: (b, t, 0)), _sds((bx, sx, w)))
    col = (pl.BlockSpec((1, KV_ROW, tm), lambda b, t: (b, 0, t)), _sds((bx, KV_ROW, sx)))
    outs = [row(RNN_WIDTH), row(RNN_WIDTH), row(ATTN_WIDTH), row(ATTN_WIDTH), row(KV_ROW), row(2 * KV_WIDTH),
            row(KV_WIDTH), col, col, col, row(LANES)]
    return pl.pallas_call(
        _pre_kernel, grid=(bx, sx // tm),
        in_specs=[pl.BlockSpec((1, tm, D_MODEL), lambda b, t: (b, t, 0)), mod_spec, mod_spec,
                  pl.BlockSpec((1, D_MODEL), lambda b, t: (0, 0)),
                  pl.BlockSpec((D_MODEL, IN_COLS_PAD), lambda b, t: (0, 0)), tab, tab, tab],
        out_specs=[o[0] for o in outs], out_shape=[o[1] for o in outs],
        compiler_params=_cp(("arbitrary", "arbitrary")), name="pre")(x3, shift, scale, ln, w_in_bf, cos, s1, s2)


def _lru_gates(xc, wax_ref, bax_ref, lam_ref):
    ra = _bdot(xc, wax_ref[...]) + bax_ref[...]
    r = jax.nn.sigmoid(ra[:, :RNN_WIDTH])
    i = jax.nn.sigmoid(ra[:, RNN_WIDTH:])
    nl = -lam_ref[...]
    softplus = jnp.maximum(nl, 0.0) + jnp.log(1.0 + jnp.exp(-jnp.abs(nl)))
    a = jnp.exp(-LRU_C * r * softplus)
    return a, jnp.sqrt(1.0 - a * a) * (i * xc)


def _scan_rows(a, b):
    t, c = a.shape
    row = lax.broadcasted_iota(jnp.int32, a.shape, 0)
    k = 1
    while k < t:
        if k < 8:
            a_s = jnp.where(row >= k, pltpu.roll(a, k, 0), 1.0)
            b_s = jnp.where(row >= k, pltpu.roll(b, k, 0), 0.0)
        else:
            a_s = jnp.concatenate([jnp.ones((k, c), F32), a[:t - k]], axis=0)
            b_s = jnp.concatenate([jnp.zeros((k, c), F32), b[:t - k]], axis=0)
        b = a * b_s + b
        a = a * a_s
        k *= 2
    return a, b


def _rglru_kernel(xr_ref, yr_ref, cw_ref, cb_ref, wax_ref, bax_ref, lam_ref, o_ref, hl_ref, xbuf, hc, *, t):
    @pl.when(pl.program_id(1) == 0)
    def _():
        xbuf[0:8, :] = jnp.zeros((8, RNN_WIDTH), F32)
        hc[...] = jnp.zeros_like(hc)

    xbuf[8:8 + t, :] = xr_ref[0]
    xc = cb_ref[...]
    for k in range(CONV_W):
        xc = xc + xbuf[5 + k:5 + k + t, :] * cw_ref[k:k + 1, :]
    xbuf[0:8, :] = xbuf[t:t + 8, :]
    a, b = _lru_gates(xc, wax_ref, bax_ref, lam_ref)
    ac, bc = _scan_rows(a, b)
    h = ac * hc[...] + bc
    hc[...] = h[t - 1:t, :]
    hl_ref[0] = h[t - 1:t, :]
    o_ref[0] = h * jax.nn.gelu(yr_ref[0])


def _rglru_prompt(xr, yr, cw, cb, wax, bax, lam, t=256):
    b, s, _ = xr.shape
    tile = pl.BlockSpec((1, t, RNN_WIDTH), lambda i, j: (i, j, 0))
    full = lambda shp: pl.BlockSpec(shp, lambda i, j: (0,) * len(shp))
    return pl.pallas_call(
        functools.partial(_rglru_kernel, t=t), grid=(b, s // t),
        in_specs=[tile, tile, full((CONV_W, RNN_WIDTH)), full((1, RNN_WIDTH)),
                  full((RNN_WIDTH, 2 * RNN_WIDTH)), full((1, 2 * RNN_WIDTH)), full((1, RNN_WIDTH))],
        out_specs=[tile, pl.BlockSpec((1, 1, RNN_WIDTH), lambda i, j: (i, 0, 0))],
        out_shape=[_sds((b, s, RNN_WIDTH)), _sds((b, 1, RNN_WIDTH))],
        scratch_shapes=[pltpu.VMEM((t + 8, RNN_WIDTH), F32), pltpu.VMEM((1, RNN_WIDTH), F32)],
        compiler_params=_cp(("arbitrary", "arbitrary")), name="rglru_prompt")(xr, yr, cw, cb, wax, bax, lam)


def _rglru_sample_kernel(xr_ref, yr_ref, cp_ref, h0_ref, cw_ref, cb_ref, wax_ref, bax_ref, lam_ref,
                         o_ref, hl_ref, *, s, nb):
    xpad = [cp_ref[i] for i in range(CONV_W - 1)] + [xr_ref[i] for i in range(s)]
    xcs = []
    for j in range(s):
        xc = cb_ref[...]
        for k in range(CONV_W):
            xc = xc + xpad[j + k] * cw_ref[k:k + 1, :]
        xcs.append(xc)
    a, b = _lru_gates(jnp.concatenate(xcs, axis=0), wax_ref, bax_ref, lam_ref)
    h = h0_ref[...]
    for j in range(s):
        h = a[j * nb:(j + 1) * nb] * h + b[j * nb:(j + 1) * nb]
        o_ref[j] = h * jax.nn.gelu(yr_ref[j])
    hl_ref[...] = h


def _rglru_sample(xr_t, yr_t, cp_t, h0, cw, cb, wax, bax, lam):
    s, nb, _ = xr_t.shape
    return pl.pallas_call(
        functools.partial(_rglru_sample_kernel, s=s, nb=nb),
        out_shape=[_sds((s, nb, RNN_WIDTH)), _sds((nb, RNN_WIDTH))],
        compiler_params=pltpu.CompilerParams(vmem_limit_bytes=VMEM_LIMIT), name="rglru_sample")(
            xr_t, yr_t, cp_t, h0, cw, cb, wax, bax, lam)


def _build_w1bd(w1_ref, w1bd):
    w1bd[...] = jnp.zeros_like(w1bd)
    for c in range(2):
        for k in range(N_KV_HEADS):
            r0 = c * KV_WIDTH + k * HEAD_DIM
            f0 = (c * N_KV_HEADS + k) * CMP_HIDDEN
            w1bd[:, r0:r0 + HEAD_DIM, f0:f0 + CMP_HIDDEN] = w1_ref[c]


def _compress_blocks(get_rows, m, pe_ref, w1bd, w2_ref, acc):
    acc[...] = jnp.zeros_like(acc)

    def body(l, carry):
        z = get_rows(l) + pe_ref[pl.ds(l, 1), :]
        acc[...] += _bdot(z, w1bd[l])
        return carry

    lax.fori_loop(0, CMP_BLOCK, body, 0)
    return _bdot(jax.nn.gelu(acc[...]), w2_ref[...])


def _compress_prompt_kernel(kv_ref, pe_ref, w1_ref, w2_ref, o_ref, w1bd, acc, *, nblk):
    @pl.when(pl.program_id(0) == 0)
    def _():
        _build_w1bd(w1_ref, w1bd)

    o_ref[0] = _compress_blocks(lambda l: kv_ref[0, :, l, :], nblk, pe_ref, w1bd, w2_ref, acc)


def _compress_prompt(kvc, pe_row, w1_bf, w2bd_bf):
    b, s, _ = kvc.shape
    nblk = s // CMP_BLOCK
    kv4 = kvc.reshape(b, nblk, CMP_BLOCK, KV_ROW)
    return pl.pallas_call(
        functools.partial(_compress_prompt_kernel, nblk=nblk), grid=(b,),
        in_specs=[pl.BlockSpec((1, nblk, CMP_BLOCK, KV_ROW), lambda i: (i, 0, 0, 0)),
                  pl.BlockSpec((CMP_BLOCK, KV_ROW), lambda i: (0, 0)),
                  pl.BlockSpec((2, CMP_BLOCK, HEAD_DIM, CMP_HIDDEN), lambda i: (0, 0, 0, 0)),
                  pl.BlockSpec((4 * CMP_HIDDEN, KV_ROW), lambda i: (0, 0))],
        out_specs=pl.BlockSpec((1, nblk, KV_ROW), lambda i: (i, 0, 0)),
        out_shape=_sds((b, nblk, KV_ROW)),
        scratch_shapes=[pltpu.VMEM((CMP_BLOCK, KV_ROW, 4 * CMP_HIDDEN), BF16),
                        pltpu.VMEM((nblk, 4 * CMP_HIDDEN), F32)],
        compiler_params=_cp(("arbitrary",)), name="compress_prompt")(kv4, pe_row, w1_bf, w2bd_bf)


def _page_copies(pt_ref, pool_ref, dst, sem, b, slot, n_pages):
    return [pltpu.make_async_copy(pool_ref.at[pt_ref[b, p]], dst(slot, p), sem.at[slot]) for p in range(n_pages)]


def _paged_prefetch(pt_ref, pool_ref, dst, sem, n_pages):
    b = pl.program_id(0)
    nb = pl.num_programs(0)
    slot = lax.rem(b, 2)

    @pl.when(b == 0)
    def _():
        for cp in _page_copies(pt_ref, pool_ref, dst, sem, 0, 0, n_pages):
            cp.start()

    @pl.when(b + 1 < nb)
    def _():
        for cp in _page_copies(pt_ref, pool_ref, dst, sem, b + 1, 1 - slot, n_pages):
            cp.start()

    for cp in _page_copies(pt_ref, pool_ref, dst, sem, b, slot, n_pages):
        cp.wait()
    return slot


def _compress_sample_kernel(pt_ref, pool_ref, new_ref, pe_ref, w1_ref, w2_ref, o_ref,
                            stage, sem, rows, w1s, acc, *, n_pages, m, nbp):
    nblk = n_pages * BLOCKS_PER_PAGE

    @pl.when(pl.program_id(0) == 0)
    def _():
        w1s[...] = jnp.zeros_like(w1s)
        for c in range(2):
            for k in range(N_KV_HEADS):
                w1s[c, :, k * HEAD_DIM:(k + 1) * HEAD_DIM, k * CMP_HIDDEN:(k + 1) * CMP_HIDDEN] = w1_ref[c]
            rows[c, nblk * ROW_PITCH:m * ROW_PITCH, :] = jnp.zeros(((m - nblk) * ROW_PITCH, KV_WIDTH), F32)

    slot = _paged_prefetch(pt_ref, pool_ref, lambda s, p: stage.at[s, p], sem, n_pages)

    def to_rows(p, carry):
        r0 = pl.multiple_of(p * (BLOCKS_PER_PAGE * ROW_PITCH), 8)
        for c in range(2):
            t = stage[slot, p, c * KV_WIDTH:(c + 1) * KV_WIDTH, :].T
            for j in range(BLOCKS_PER_PAGE):
                rows[c, pl.ds(r0 + j * ROW_PITCH, CMP_BLOCK), :] = t[j * CMP_BLOCK:(j + 1) * CMP_BLOCK]
        return carry

    lax.fori_loop(0, n_pages, to_rows, 0, unroll=4)
    for c in range(2):
        rows[c, nblk * ROW_PITCH:nblk * ROW_PITCH + 8, :] = new_ref[0, :, c * KV_WIDTH:(c + 1) * KV_WIDTH]
    acc[...] = jnp.zeros_like(acc)

    def body(i, carry):
        l0 = pl.multiple_of(i * LSTEP, LSTEP)
        pe = pe_ref[pl.ds(l0, LSTEP), :]
        for c in range(2):
            z = jnp.concatenate(
                [rows[c, pl.ds(l0 + j, m, stride=ROW_PITCH), :] + pe[j:j + 1, c * KV_WIDTH:(c + 1) * KV_WIDTH]
                 for j in range(LSTEP)], axis=1)
            w = w1s[c, pl.ds(l0, LSTEP)].reshape(LSTEP * KV_WIDTH, N_KV_HEADS * CMP_HIDDEN)
            acc[c] += _bdot(z, w)
        return carry

    lax.fori_loop(0, CMP_BLOCK // LSTEP, body, 0)
    hw = N_KV_HEADS * CMP_HIDDEN
    for c in range(2):
        o_ref[0, 0:m, c * KV_WIDTH:(c + 1) * KV_WIDTH] = _bdot(
            jax.nn.gelu(acc[c]), w2_ref[c * hw:(c + 1) * hw, c * KV_WIDTH:(c + 1) * KV_WIDTH])
    o_ref[0, m:nbp, :] = jnp.zeros((nbp - m, KV_ROW), F32)


def _compress_sample(page_table, pool_t, kvc_new8, pe_row, w1_bf, w2bd_bf, nbp):
    b, n_pages = page_table.shape
    m = n_pages * BLOCKS_PER_PAGE + 8
    grid_spec = pltpu.PrefetchScalarGridSpec(
        num_scalar_prefetch=1, grid=(b,),
        in_specs=[pl.BlockSpec(memory_space=pl.ANY),
                  pl.BlockSpec((1, 8, KV_ROW), lambda i, pt: (i, 0, 0)),
                  pl.BlockSpec((CMP_BLOCK, KV_ROW), lambda i, pt: (0, 0)),
                  pl.BlockSpec((2, CMP_BLOCK, HEAD_DIM, CMP_HIDDEN), lambda i, pt: (0, 0, 0, 0)),
                  pl.BlockSpec((4 * CMP_HIDDEN, KV_ROW), lambda i, pt: (0, 0))],
        out_specs=pl.BlockSpec((1, nbp, KV_ROW), lambda i, pt: (i, 0, 0)),
        scratch_shapes=[pltpu.VMEM((2, n_pages, KV_ROW, PAGE_SIZE), F32), pltpu.SemaphoreType.DMA((2,)),
                        pltpu.VMEM((2, m * ROW_PITCH, KV_WIDTH), F32),
                        pltpu.VMEM((2, CMP_BLOCK, KV_WIDTH, N_KV_HEADS * CMP_HIDDEN), BF16),
                        pltpu.VMEM((2, m, N_KV_HEADS * CMP_HIDDEN), F32)])
    return pl.pallas_call(
        functools.partial(_compress_sample_kernel, n_pages=n_pages, m=m, nbp=nbp),
        grid_spec=grid_spec, out_shape=_sds((b, nbp, KV_ROW)),
        compiler_params=_cp(("arbitrary",)), name="compress_sample")(page_table, pool_t, kvc_new8, pe_row, w1_bf, w2bd_bf)


def _head_lanes(q, h, dst, scale):
    src = q[:, (h // 2) * LANES:(h // 2 + 1) * LANES]
    if (h % 2) != dst:
        src = pltpu.roll(src, HEAD_DIM, 1)
    lane = lax.broadcasted_iota(jnp.int32, src.shape, 1)
    keep = (lane >= dst * HEAD_DIM) & (lane < (dst + 1) * HEAD_DIM)
    return jnp.where(keep, src * scale, 0.0)


def _arrange_q(q, h, scale):
    return _head_lanes(q, h, h // GROUP, scale)


def _gather_heads(o_list):
    parts = []
    for h, o in enumerate(o_list):
        kvh = h // GROUP
        parts.append(o[:, kvh * HEAD_DIM:(kvh + 1) * HEAD_DIM])
    return jnp.concatenate(parts, axis=-1)


def _select_mask(imp, qpos, nb):
    tq, nbp = imp.shape
    j = lax.broadcasted_iota(jnp.int32, imp.shape, 1)
    cur = qpos // CMP_BLOCK
    forced = (j == 0) | (j == cur) | (j == cur - 1)
    score = jnp.where(j > cur, -FORCE_SCORE, jnp.where(forced, FORCE_SCORE, imp))
    score = jnp.where(j < nb, score, -3e38)
    rank = jnp.zeros(imp.shape, F32)
    for i in range(nb):
        col = score[:, i:i + 1]
        ge = jnp.where(col >= score, 1.0, 0.0)
        gt = jnp.where(col > score, 1.0, 0.0)
        rank = rank + jnp.where(j > i, ge, gt)
    return jnp.where((rank < float(min(N_SEL, nb))) & (j <= cur) & (j < nb), 1.0, 0.0)


def _cmp_kernel(q_ref, ckv_ref, oc_ref, sel_ref, *, tq, nb, nbp, pos0):
    qpos = pos0 + pl.program_id(1) * tq + lax.broadcasted_iota(jnp.int32, (tq, nbp), 0)
    j = lax.broadcasted_iota(jnp.int32, (tq, nbp), 1)
    m = ((j + 1) * CMP_BLOCK - 1 <= qpos) & (j < nb)
    q = q_ref[0]
    ck = ckv_ref[0, :, 0:KV_WIDTH]
    cv = ckv_ref[0, :, KV_WIDTH:KV_ROW].astype(BF16)
    outs = []
    for kvh in range(N_KV_HEADS):
        imp = jnp.zeros((tq, nbp), F32)
        for g in range(GROUP):
            qa = _arrange_q(q, kvh * GROUP + g, ATTN_SCALE)
            s = jnp.where(m, _dot_nt(qa, ck, precision=lax.Precision.HIGHEST), NEG_INF)
            e = jnp.exp(s - jnp.max(s, axis=-1, keepdims=True))
            p = jnp.where(m, e / jnp.sum(e, axis=-1, keepdims=True), 0.0)
            imp = imp + p
            outs.append(jnp.dot(p.astype(BF16), cv, preferred_element_type=F32))
        sel_ref[0, kvh] = _select_mask(imp, qpos, nb)
    oc_ref[0] = _gather_heads(outs)


def _cmp_select(q, ckv_p, nb, pos0, tq):
    b, sq, _ = q.shape
    nbp = ckv_p.shape[1]
    return pl.pallas_call(
        functools.partial(_cmp_kernel, tq=tq, nb=nb, nbp=nbp, pos0=pos0), grid=(b, sq // tq),
        in_specs=[pl.BlockSpec((1, tq, ATTN_WIDTH), lambda i, t: (i, t, 0)),
                  pl.BlockSpec((1, nbp, KV_ROW), lambda i, t: (i, 0, 0))],
        out_specs=[pl.BlockSpec((1, tq, ATTN_WIDTH), lambda i, t: (i, t, 0)),
                   pl.BlockSpec((1, N_KV_HEADS, tq, nbp), lambda i, t: (i, 0, t, 0))],
        out_shape=[_sds((b, sq, ATTN_WIDTH)), _sds((b, N_KV_HEADS, sq, nbp))],
        compiler_params=_cp(("arbitrary", "arbitrary")), name="cmp_select")(q, ckv_p)


def _select_mask_t(imp, qpos, nb):
    j = lax.broadcasted_iota(jnp.int32, imp.shape, 0)
    cur = qpos // CMP_BLOCK
    forced = (j == 0) | (j == cur) | (j == cur - 1)
    score = jnp.where(j > cur, -FORCE_SCORE, jnp.where(forced, FORCE_SCORE, imp))
    sub = 8
    ngrp = nb // sub
    groups = [score[r * sub:(r + 1) * sub] for r in range(ngrp)]
    jsub = lax.broadcasted_iota(jnp.int32, groups[0].shape, 0)
    ranks = [jnp.zeros(groups[0].shape, F32) for _ in range(ngrp)]
    for i in range(nb):
        row = score[i:i + 1]
        for r in range(ngrp):
            if r * sub > i:
                hit = row >= groups[r]
            elif (r + 1) * sub - 1 <= i:
                hit = row > groups[r]
            else:
                hit = ((jsub + r * sub > i) & (row >= groups[r])) | (row > groups[r])
            ranks[r] = ranks[r] + jnp.where(hit, 1.0, 0.0)
    rank = jnp.concatenate(ranks, axis=0)
    return jnp.where((rank < float(min(N_SEL, nb))) & (j <= cur), 1.0, 0.0)


def _cmp_kernel_t(q_ref, ckv_ref, oc_ref, selt_ref, *, tq, nb, nbp, pos0):
    qpos = pos0 + pl.program_id(1) * tq + lax.broadcasted_iota(jnp.int32, (nb, tq), 1)
    j = lax.broadcasted_iota(jnp.int32, (nb, tq), 0)
    m = (j + 1) * CMP_BLOCK - 1 <= qpos
    q = q_ref[0]
    ck = ckv_ref[0, :, 0:KV_WIDTH]
    cv = ckv_ref[0, :, KV_WIDTH:KV_ROW].astype(BF16)
    m4 = jnp.concatenate([m] * GROUP, axis=1)
    outs = []
    for kvh in range(N_KV_HEADS):
        qa = jnp.concatenate([_arrange_q(q, kvh * GROUP + g, ATTN_SCALE) for g in range(GROUP)], axis=0)
        s = jnp.where(m4, _dot_nt(ck, qa, precision=lax.Precision.HIGHEST), NEG_INF)
        e = jnp.exp(s - jnp.max(s, axis=0, keepdims=True))
        p = jnp.where(m4, e / jnp.sum(e, axis=0, keepdims=True), 0.0)
        imp = p[:, 0:tq]
        for g in range(1, GROUP):
            imp = imp + p[:, g * tq:(g + 1) * tq]
        o = lax.dot_general(p.astype(BF16), cv, (((0,), (0,)), ((), ())), preferred_element_type=F32)
        outs += [o[g * tq:(g + 1) * tq] for g in range(GROUP)]
        selt_ref[0, kvh, 0:nb, :] = _select_mask_t(imp, qpos, nb)
        selt_ref[0, kvh, nb:nbp, :] = jnp.zeros((nbp - nb, tq), F32)
    oc_ref[0] = _gather_heads(outs)


def _cmp_select_t(q, ckv, nbp, pos0, tq):
    b, sq, _ = q.shape
    nb = ckv.shape[1]
    return pl.pallas_call(
        functools.partial(_cmp_kernel_t, tq=tq, nb=nb, nbp=nbp, pos0=pos0), grid=(b, sq // tq),
        in_specs=[pl.BlockSpec((1, tq, ATTN_WIDTH), lambda i, t: (i, t, 0)),
                  pl.BlockSpec((1, nb, KV_ROW), lambda i, t: (i, 0, 0))],
        out_specs=[pl.BlockSpec((1, tq, ATTN_WIDTH), lambda i, t: (i, t, 0)),
                   pl.BlockSpec((1, N_KV_HEADS, nbp, tq), lambda i, t: (i, 0, 0, t))],
        out_shape=[_sds((b, sq, ATTN_WIDTH)), _sds((b, N_KV_HEADS, nbp, sq))],
        compiler_params=_cp(("arbitrary", "arbitrary")), name="cmp_select_t")(q, ckv)


def _flash_kernel(q_ref, k_ref, vt_ref, *rest, tq, tk, nk, mode):
    if mode == "sel":
        selt_ref, o_ref, qt, m_s, l_s, acc = rest
    else:
        o_ref, qt, m_s, l_s, acc = rest
    qi = pl.program_id(1)
    kstep = pl.program_id(2)
    ki = kstep if mode == "sel" else qi - (nk - 1) + kstep

    @pl.when(kstep == 0)
    def _():
        q = q_ref[0]
        for h in range(N_HEADS):
            kvh = h // GROUP
            if mode == "sel":
                cols = _head_lanes(q, h, 0, ATTN_SCALE * LOG2_E).T
                bias = (selt_ref[0, kvh, 0:HEAD_DIM, :] - 1.0) * (-NEG_INF)
                cols = jnp.concatenate([cols[0:HEAD_DIM], bias], axis=0)
            else:
                cols = _head_lanes(q, h, kvh, ATTN_SCALE * LOG2_E).T
            qt[:, h * tq:(h + 1) * tq] = cols.astype(BF16)
        m_s[...] = jnp.full_like(m_s, M_INIT)
        l_s[...] = jnp.zeros_like(l_s)
        acc[...] = jnp.zeros_like(acc)

    def step(masked):
        vt = vt_ref[0].astype(BF16)
        if masked:
            d = (lax.broadcasted_iota(jnp.int32, (tk, tq), 0) - lax.broadcasted_iota(jnp.int32, (tk, tq), 1)
                 + (ki * tk - qi * tq))
            valid = (d <= 0) if mode == "sel" else (d <= 0) & (d > -WINDOW)
        if mode == "sel":
            ks = [k_ref[0, :, kvh * LANES:(kvh + 1) * LANES].astype(BF16) for kvh in range(N_KV_HEADS)]
        else:
            ks = [k_ref[0].astype(BF16)] * N_KV_HEADS

        def scores(h):
            return jnp.dot(ks[h // GROUP], qt[:, h * tq:(h + 1) * tq], preferred_element_type=F32)

        def accumulate(h, alpha, pv):
            rows = slice(h * LANES, (h + 1) * LANES)
            acc[rows] = alpha * acc[rows] + pv

        s_next = scores(0)
        pending = None
        for h in range(N_HEADS):
            s = s_next
            if h + 1 < N_HEADS:
                s_next = scores(h + 1)
            if masked:
                s = jnp.where(valid, s, NEG_INF)
            m_old = m_s[h, 0:1, :]
            m_new = jnp.maximum(m_old, jnp.max(s, axis=0, keepdims=True))
            alpha = jnp.exp2(m_old - m_new)
            p = jnp.exp2(s - m_new)
            l_s[h] = jnp.broadcast_to(alpha * l_s[h, 0:1, :] + jnp.sum(p, axis=0, keepdims=True), (8, tq))
            m_s[h] = jnp.broadcast_to(m_new, (8, tq))
            pv = jnp.dot(vt, p.astype(BF16), preferred_element_type=F32)
            if pending is not None:
                accumulate(*pending)
            pending = (h, alpha, pv)
        accumulate(*pending)

    if mode == "sel":
        needed = ki * tk <= qi * tq + tq - 1
        masked = ki * tk + tk - 1 > qi * tq
    else:
        needed = ki >= 0
        masked = kstep != nk - 2

    @pl.when(needed & masked)
    def _():
        step(True)

    @pl.when(needed & jnp.logical_not(masked))
    def _():
        step(False)

    @pl.when(kstep == nk - 1)
    def _():
        outs = [(acc[h * LANES:(h + 1) * LANES] / l_s[h, 0:1, :]).T for h in range(N_HEADS)]
        o_ref[0] = _gather_heads(outs)


def _flash_prompt(qrot, k_rows, kv_t, sel_t, mode, tq, tk):
    b, s, _ = qrot.shape
    nq = s // tq
    if mode == "sel":
        assert s // CMP_BLOCK <= HEAD_DIM
        nk = s // tk
        kidx = lambda qi, ks: jnp.minimum(ks, (qi * tq + tq - 1) // tk)
    else:
        assert tq == tk and WINDOW == 2 * tk
        nk = WINDOW // tk + 1
        kidx = lambda qi, ks: jnp.maximum(qi - (nk - 1) + ks, 0)
    kw = k_rows.shape[-1]
    in_specs = [pl.BlockSpec((1, tq, ATTN_WIDTH), lambda i, qi, ks: (i, qi, 0)),
                pl.BlockSpec((1, tk, kw), lambda i, qi, ks: (i, kidx(qi, ks), 0)),
                pl.BlockSpec((1, KV_WIDTH, tk), lambda i, qi, ks: (i, 1, kidx(qi, ks)))]
    args = [qrot, k_rows, kv_t]
    if mode == "sel":
        nbp = sel_t.shape[2]
        in_specs += [pl.BlockSpec((1, N_KV_HEADS, nbp, tq), lambda i, qi, ks: (i, 0, 0, qi))]
        args += [sel_t]
    return pl.pallas_call(
        functools.partial(_flash_kernel, tq=tq, tk=tk, nk=nk, mode=mode), grid=(b, nq, nk),
        in_specs=in_specs,
        out_specs=pl.BlockSpec((1, tq, ATTN_WIDTH), lambda i, qi, ks: (i, qi, 0)),
        out_shape=_sds((b, s, ATTN_WIDTH)),
        scratch_shapes=[pltpu.VMEM((LANES, N_HEADS * tq), BF16), pltpu.VMEM((N_HEADS, 8, tq), F32),
                        pltpu.VMEM((N_HEADS, 8, tq), F32), pltpu.VMEM((N_HEADS * LANES, tq), F32)],
        compiler_params=_cp(("arbitrary", "arbitrary", "arbitrary")), name="flash_" + mode)(*args)


def _sample_q_rows(q8):
    return jnp.concatenate([_arrange_q(q8, h, ATTN_SCALE) for h in range(N_HEADS)], axis=0).astype(BF16)


def _sample_out(o_all):
    return _gather_heads([o_all[h * 8:(h + 1) * 8] for h in range(N_HEADS)])


def _two_part_attention(s_past, s_new, vt_past, vt_new):
    mx = jnp.maximum(jnp.max(s_past, axis=-1, keepdims=True), jnp.max(s_new, axis=-1, keepdims=True))
    p_past = jnp.exp(s_past - mx)
    p_new = jnp.exp(s_new - mx)
    den = jnp.sum(p_past, axis=-1, keepdims=True) + jnp.sum(p_new, axis=-1, keepdims=True)
    return (_dot_nt(p_past.astype(BF16), vt_past) + _dot_nt(p_new.astype(BF16), vt_new)) / den


def _sel_sample_kernel(pt_ref, pool_ref, q_ref, newt_ref, sel_ref, e_ref, o_ref, buf, sem, *, n_pages, pos0, n_new):
    slot = _paged_prefetch(pt_ref, pool_ref, lambda s, p: buf.at[s, :, pl.ds(p * PAGE_SIZE, PAGE_SIZE)], sem, n_pages)
    nkeys = n_pages * PAGE_SIZE
    nblk = nkeys // CMP_BLOCK
    nrow = 8 * N_HEADS
    qall = _sample_q_rows(q_ref[0])
    selrows = jnp.concatenate([sel_ref[0, h // GROUP] for h in range(N_HEADS)], axis=0)
    qpos = pos0 + lax.rem(lax.broadcasted_iota(jnp.int32, (nrow, 1), 0), 8)
    kpos = lax.broadcasted_iota(jnp.int32, (nrow, nkeys), 1)
    ex = jnp.dot(selrows.astype(BF16), e_ref[...], preferred_element_type=F32)
    s_past = jnp.dot(qall, buf[slot, 0:KV_WIDTH, :].astype(BF16), preferred_element_type=F32)
    s_past = jnp.where((ex > 0.5) & (kpos <= qpos), s_past, NEG_INF)
    newt = newt_ref[0].astype(BF16)
    inew = lax.broadcasted_iota(jnp.int32, (nrow, LANES), 1) - (LANES - n_new)
    sel_new = selrows[:, nblk:nblk + 1] > 0.5
    s_new = jnp.dot(qall, newt[0:KV_WIDTH], preferred_element_type=F32)
    s_new = jnp.where(sel_new & (inew >= 0) & (pos0 + inew <= qpos), s_new, NEG_INF)
    o = _two_part_attention(s_past, s_new, buf[slot, KV_WIDTH:KV_ROW, :].astype(BF16), newt[KV_WIDTH:KV_ROW])
    o_ref[0] = _sample_out(o)


def _sel_sample(page_table, pool_t, qrot8, kvs_newt, sel, emat, pos0, n_new):
    b, n_pages = page_table.shape
    nbp = sel.shape[-1]
    nkeys = n_pages * PAGE_SIZE
    grid_spec = pltpu.PrefetchScalarGridSpec(
        num_scalar_prefetch=1, grid=(b,),
        in_specs=[pl.BlockSpec(memory_space=pl.ANY),
                  pl.BlockSpec((1, 8, ATTN_WIDTH), lambda i, pt: (i, 0, 0)),
                  pl.BlockSpec((1, KV_ROW, LANES), lambda i, pt: (i, 0, 0)),
                  pl.BlockSpec((1, N_KV_HEADS, 8, nbp), lambda i, pt: (i, 0, 0, 0)),
                  pl.BlockSpec((nbp, nkeys), lambda i, pt: (0, 0))],
        out_specs=pl.BlockSpec((1, 8, ATTN_WIDTH), lambda i, pt: (i, 0, 0)),
        scratch_shapes=[pltpu.VMEM((2, KV_ROW, nkeys), F32), pltpu.SemaphoreType.DMA((2,))])
    return pl.pallas_call(
        functools.partial(_sel_sample_kernel, n_pages=n_pages, pos0=pos0, n_new=n_new),
        grid_spec=grid_spec, out_shape=_sds((b, 8, ATTN_WIDTH)),
        compiler_params=_cp(("arbitrary",)), name="sel_sample")(page_table, pool_t, qrot8, kvs_newt, sel, emat)


def _win_sample_kernel(q_ref, buf_ref, newt_ref, o_ref, nw_ref, *, wb, n_new):
    nrow = 8 * N_HEADS
    qall = _sample_q_rows(q_ref[0])
    buf = buf_ref[0]
    newt = newt_ref[0]
    srow = lax.rem(lax.broadcasted_iota(jnp.int32, (nrow, 1), 0), 8)
    i_past = lax.broadcasted_iota(jnp.int32, (nrow, wb), 1)
    dp = wb + srow - i_past
    s_past = jnp.dot(qall, buf[0:KV_WIDTH].astype(BF16), preferred_element_type=F32)
    s_past = jnp.where((dp >= 0) & (dp < WINDOW), s_past, NEG_INF)
    i_new = lax.broadcasted_iota(jnp.int32, (nrow, LANES), 1) - (LANES - n_new)
    dn = srow - i_new
    s_new = jnp.dot(qall, newt[0:KV_WIDTH].astype(BF16), preferred_element_type=F32)
    s_new = jnp.where((i_new >= 0) & (dn >= 0) & (dn < WINDOW), s_new, NEG_INF)
    o = _two_part_attention(s_past, s_new, buf[KV_WIDTH:KV_ROW].astype(BF16), newt[KV_WIDTH:KV_ROW].astype(BF16))
    o_ref[0] = _sample_out(o)
    nw_ref[0] = pltpu.roll(buf, wb - n_new, 1)
    lane = lax.broadcasted_iota(jnp.int32, (KV_ROW, LANES), 1)
    nw_ref[0, :, wb - LANES:wb] = jnp.where(lane >= LANES - n_new, newt,
                                            pltpu.roll(buf[:, wb - LANES:wb], LANES - n_new, 1))


def _win_sample(qrot8, win_t, kvw_newt, n_new):
    b, _, wb = win_t.shape
    return pl.pallas_call(
        functools.partial(_win_sample_kernel, wb=wb, n_new=n_new), grid=(b,),
        in_specs=[pl.BlockSpec((1, 8, ATTN_WIDTH), lambda i: (i, 0, 0)),
                  pl.BlockSpec((1, KV_ROW, wb), lambda i: (i, 0, 0)),
                  pl.BlockSpec((1, KV_ROW, LANES), lambda i: (i, 0, 0))],
        out_specs=[pl.BlockSpec((1, 8, ATTN_WIDTH), lambda i: (i, 0, 0)),
                   pl.BlockSpec((1, KV_ROW, wb), lambda i: (i, 0, 0))],
        out_shape=[_sds((b, 8, ATTN_WIDTH)), _sds((b, KV_ROW, wb))],
        compiler_params=_cp(("arbitrary",)), name="win_sample")(qrot8, win_t, kvw_newt)


def _post_kernel(x_ref, or_ref, oc_ref, os_ref, ow_ref, g_ref, m2_ref, m3_ref, m4_ref, m5_ref,
                 gnr_ref, gna_ref, ln2_ref, fin_ref, eg_ref, wo_ref, wg_ref, wu_ref, wd_ref, y_ref, *, ff_chunks):
    g = g_ref[0]
    g_hi = g.astype(BF16)
    g_lo = (g - g_hi.astype(F32)).astype(BF16)
    gx = jnp.dot(jnp.concatenate([g_hi, g_lo], axis=-1), eg_ref[...], preferred_element_type=F32)
    attn = (gx[:, 0:ATTN_WIDTH] * oc_ref[0] + gx[:, ATTN_WIDTH:2 * ATTN_WIDTH] * os_ref[0]
            + gx[:, 2 * ATTN_WIDTH:3 * ATTN_WIDTH] * ow_ref[0])
    mix = (_bdot(_rms(or_ref[0], gnr_ref[...]), wo_ref[0:RNN_WIDTH, :])
           + _bdot(_rms(attn, gna_ref[...]), wo_ref[RNN_WIDTH:D_MODEL, :]))
    x1 = x_ref[0] + m2_ref[0] * mix
    h = (_rms(x1, ln2_ref[...]) * (1.0 + m4_ref[0]) + m3_ref[0]).astype(BF16)
    cw = D_FF // ff_chunks
    ff = jnp.zeros(x1.shape, F32)
    for c in range(ff_chunks):
        gate = jnp.dot(h, wg_ref[:, c * cw:(c + 1) * cw], preferred_element_type=F32)
        up = jnp.dot(h, wu_ref[:, c * cw:(c + 1) * cw], preferred_element_type=F32)
        act = gate * jax.nn.sigmoid(gate) * up
        ff = ff + _bdot(act, wd_ref[c * cw:(c + 1) * cw, :])
    y = x1 + m5_ref[0] * ff
    y_ref[0] = _rms(y, fin_ref[...])


def _post(x3, o_r, o_c, o_s, o_w, g, mods, gnr, gna, ln2, fin, egate, wo, wg, wu, wd, tm, ff_chunks=2):
    bx, sx, _ = x3.shape
    sm = mods[0].shape[1]
    if sm == 1:
        mod_spec = pl.BlockSpec((1, 1, D_MODEL), lambda b, t: (b, 0, 0))
    else:
        mod_spec = pl.BlockSpec((1, tm, D_MODEL), lambda b, t: (b, t, 0))
    tile = lambda w: pl.BlockSpec((1, tm, w), lambda b, t: (b, t, 0))

    def const(shape):
        return pl.BlockSpec(shape, lambda b, t: (0,) * len(shape), pipeline_mode=pl.Buffered(1))

    return pl.pallas_call(
        functools.partial(_post_kernel, ff_chunks=ff_chunks), grid=(bx, sx // tm),
        in_specs=[tile(D_MODEL), tile(RNN_WIDTH), tile(ATTN_WIDTH), tile(ATTN_WIDTH), tile(ATTN_WIDTH), tile(LANES),
                  mod_spec, mod_spec, mod_spec, mod_spec,
                  const((1, RNN_WIDTH)), const((1, ATTN_WIDTH)), const((1, D_MODEL)), const((1, D_MODEL)),
                  const((2 * LANES, 3 * ATTN_WIDTH)), const((D_MODEL, D_MODEL)),
                  const((D_MODEL, D_FF)), const((D_MODEL, D_FF)), const((D_FF, D_MODEL))],
        out_specs=tile(D_MODEL), out_shape=_sds((bx, sx, D_MODEL)),
        compiler_params=_cp(("arbitrary", "arbitrary")), name="post")(
            x3, o_r, o_c, o_s, o_w, g, mods[2], mods[3], mods[4], mods[5], gnr, gna, ln2, fin, egate, wo, wg, wu, wd)


def _rope_tables(pos):
    inv = jnp.exp(jnp.arange(ROT_HALF, dtype=F32) * (-math.log(ROPE_THETA) / ROT_HALF))
    ang = pos.astype(F32)[:, None] * inv[None, :]
    cos, sin = jnp.cos(ang), jnp.sin(ang)
    n = pos.shape[0]
    one = jnp.ones((n, HEAD_DIM - ROT_DIM), F32)
    zero = jnp.zeros((n, HEAD_DIM - ROT_DIM), F32)
    z8 = jnp.zeros((n, ROT_HALF), F32)
    c = jnp.concatenate([cos, cos, one], axis=-1)
    s1 = jnp.concatenate([-sin, z8, zero], axis=-1)
    s2 = jnp.concatenate([z8, sin, zero], axis=-1)
    return tuple(jnp.concatenate([t, t], axis=-1) for t in (c, s1, s2))


def _blockdiag(w):
    nb, d, e = w.shape
    return jnp.einsum('nde,nm->ndme', w, jnp.eye(nb, dtype=w.dtype)).reshape(nb * d, nb * e)


def _prep_params(w_in, lru_wa, lru_wx, lru_ba, lru_bx, cmp_pe, cmp_w1, cmp_w2):
    w_in_p = jnp.pad(w_in, ((0, 0), (0, IN_COLS_PAD - w_in.shape[1]))).astype(BF16)
    wax = jnp.concatenate([_blockdiag(lru_wa), _blockdiag(lru_wx)], axis=1).astype(BF16)
    bax = jnp.concatenate([lru_ba, lru_bx])[None, :]
    pe_row = jnp.broadcast_to(jnp.transpose(cmp_pe, (1, 0, 2))[:, :, None, :],
                              (CMP_BLOCK, 2, N_KV_HEADS, HEAD_DIM)).reshape(CMP_BLOCK, KV_ROW)
    w2bd = _blockdiag(jnp.repeat(cmp_w2, N_KV_HEADS, axis=0)).astype(BF16)
    e = np.zeros((LANES, 3 * ATTN_WIDTH), np.float32)
    for br in range(3):
        for h in range(N_HEADS):
            e[br * N_HEADS + h, br * ATTN_WIDTH + h * HEAD_DIM:br * ATTN_WIDTH + (h + 1) * HEAD_DIM] = 1.0
    egate = jnp.asarray(np.concatenate([e, e], axis=0), BF16)
    return w_in_p, wax, bax, pe_row, cmp_w1.astype(BF16), w2bd, egate


def _expand_matrix(nbp, n_tiles, tk):
    blk = (np.arange(n_tiles)[:, None, None] * tk + np.arange(tk)[None, None, :]) // CMP_BLOCK
    return jnp.asarray(blk == np.arange(nbp)[None, :, None], BF16)


def _kv_from_t(kv_t):
    n, _, s = kv_t.shape
    return jnp.transpose(kv_t.reshape(n, 2, N_KV_HEADS, HEAD_DIM, s), (0, 4, 1, 2, 3))


def _kv_to_t(kv):
    n, s = kv.shape[:2]
    return jnp.transpose(kv, (0, 2, 3, 4, 1)).reshape(n, KV_ROW, s)


def _layer_prompt(x, mods, P, tm=512, tq=256):
    b, s, _ = x.shape
    cos, s1, s2 = _rope_tables(jnp.arange(s))
    xr, yr, q, qrot, kvc, ks, kw, kvc_t, kvs_t, kvw_t, g = _pre(
        x, mods[0], mods[1], P['ln1'], P['w_in'], cos, s1, s2, tm)
    o_r, h_new = _rglru_prompt(xr, yr, P['conv_w'], P['conv_b'], P['wax'], P['bax'], P['lam'])
    nb = s // CMP_BLOCK
    ckv = _compress_prompt(kvc, P['pe_row'], P['w1'], P['w2bd'])
    nbp = LANES * (-(-nb // LANES))
    o_c, sel_t = _cmp_select_t(q, ckv, nbp, 0, min(256, s))
    o_s = _flash_prompt(qrot, ks, kvs_t, sel_t, "sel", tq, min(512, s))
    o_w = _flash_prompt(qrot, kw, kvw_t, None, "win", tq, tq)
    y = _post(x, o_r, o_c, o_s, o_w, g, mods, P['gn_rnn'], P['gn_attn'], P['ln2'], P['final_g'], P['egate'],
              P['w_out'], P['w_gate'], P['w_up'], P['w_down'], tm)
    wlen = min(WINDOW, s)
    return y, (_kv_from_t(kvc_t), _kv_from_t(kvs_t), _kv_from_t(kvw_t[:, :, s - wlen:]),
               xr[:, s - (CONV_W - 1):], h_new[:, 0])


def _pad_rows8(a):
    return jnp.pad(a, ((0, 0), (0, 8 - a.shape[1]), (0, 0)))


def _layer_sample(x, mods_tok, cmp_pool, sel_pool, win_buf, conv_buf, h0, page_table, P):
    b, s, _ = x.shape
    n_pages = page_table.shape[1]
    past = n_pages * PAGE_SIZE
    nb = past // CMP_BLOCK + 1
    pos = past + jnp.arange(s)
    cos, s1, s2 = (jnp.tile(t, (b, 1)) for t in _rope_tables(pos))
    n = b * s
    flat = lambda a: a.reshape(1, n, a.shape[-1])
    outs = _pre(flat(x), mods_tok[0], mods_tok[1], P['ln1'], P['w_in'], cos, s1, s2, min(512, n))
    xr, yr, q, qrot, kvc = (o.reshape(b, s, o.shape[-1]) for o in outs[:5])
    kvc_t, kvs_t, kvw_t = (jnp.transpose(o[0].reshape(KV_ROW, b, s), (1, 0, 2)) for o in outs[7:10])
    lane_pad = lambda a: jnp.pad(a, ((0, 0), (0, 0), (LANES - s, 0)))
    g = outs[10]
    tmaj = lambda a: jnp.transpose(a, (1, 0, 2))
    o_r_t, h_new = _rglru_sample(tmaj(xr), tmaj(yr), tmaj(conv_buf), h0, P['conv_w'], P['conv_b'],
                                 P['wax'], P['bax'], P['lam'])
    o_r = tmaj(o_r_t)
    nbp = LANES * (-(-nb // LANES))
    ckv_p = _compress_sample(page_table, _kv_to_t(cmp_pool), _pad_rows8(kvc), P['pe_row'], P['w1'], P['w2bd'], nbp)
    q8, qrot8 = _pad_rows8(q), _pad_rows8(qrot)
    o_c8, sel = _cmp_select(q8, ckv_p, nb, past, 8)
    o_s8 = _sel_sample(page_table, _kv_to_t(sel_pool), qrot8, lane_pad(kvs_t), sel,
                       _expand_matrix(nbp, 1, past)[0], past, s)
    o_w8, new_win_t = _win_sample(qrot8, _kv_to_t(win_buf), lane_pad(kvw_t), s)
    y = _post(flat(x), flat(o_r), flat(o_c8[:, :s]), flat(o_s8[:, :s]), flat(o_w8[:, :s]), g, mods_tok,
              P['gn_rnn'], P['gn_attn'], P['ln2'], P['final_g'], P['egate'],
              P['w_out'], P['w_gate'], P['w_up'], P['w_down'], min(512, n))
    return y.reshape(b, s, D_MODEL), (_kv_from_t(kvc_t), _kv_from_t(kvs_t), _kv_from_t(new_win_t),
                                      xr[:, s - (CONV_W - 1):], h_new)


def kernel(x_prompt, x_sample, cache_cmp_kv, cache_sel_kv, state_win_kv, state_conv, state_lru_h, page_table,
           c_prompt, c_sample, ln1_g, ln2_g, w_ada, b_ada, w_in, conv_w, conv_b, lru_wa, lru_ba, lru_wx, lru_bx,
           lru_lambda, cmp_pe, cmp_w1, cmp_w2, gn_rnn, gn_attn, w_out, w_gate, w_up, w_down, final_g):
    depth = w_in.shape[0]
    assert depth == 1
    bp = x_prompt.shape[0]
    bs, ss, _ = x_sample.shape
    l = 0
    w_in_p, wax, bax, pe_row, w1, w2bd, egate = _prep_params(
        w_in[l], lru_wa[l], lru_wx[l], lru_ba[l], lru_bx[l], cmp_pe[l], cmp_w1[l], cmp_w2[l])
    P = {'ln1': ln1_g[l][None], 'ln2': ln2_g[l][None], 'w_in': w_in_p, 'conv_w': conv_w[l], 'conv_b': conv_b[l][None],
         'wax': wax, 'bax': bax, 'lam': lru_lambda[l][None], 'pe_row': pe_row, 'w1': w1, 'w2bd': w2bd,
         'gn_rnn': gn_rnn[l][None], 'gn_attn': gn_attn[l][None], 'egate': egate, 'final_g': final_g[None],
         'w_out': w_out[l].astype(BF16), 'w_gate': w_gate[l].astype(BF16), 'w_up': w_up[l].astype(BF16),
         'w_down': w_down[l].astype(BF16)}
    mods_all = _ada(jnp.concatenate([c_prompt, c_sample], axis=0), w_ada[l].astype(BF16), b_ada[l][None])
    mods_p = [m[:, None, :] for m in jnp.split(mods_all[:bp], 6, axis=-1)]
    mods_s = [jnp.repeat(m, ss, axis=0)[None] for m in jnp.split(mods_all[bp:], 6, axis=-1)]

    yp, st_p = _layer_prompt(x_prompt, mods_p, P)
    ys, st_s = _layer_sample(x_sample, mods_s, cache_cmp_kv[l], cache_sel_kv[l], state_win_kv[l],
                             state_conv[l], state_lru_h[l], page_table, P)

    return (yp, ys, st_p[0][None], st_s[0][None], st_p[1][None], st_s[1][None], st_p[2][None], st_s[2][None],
            st_p[3][None], st_s[3][None], st_p[4][None], st_s[4][None])
```

```python
import functools
import math

import numpy as np
import jax
import jax.numpy as jnp
from jax import lax
from jax.experimental import pallas as pl
from jax.experimental.pallas import tpu as pltpu

F32 = jnp.float32
BF16 = jnp.bfloat16

D_MODEL = 1024
HEAD_DIM = 64
N_HEADS = 8
N_KV_HEADS = 2
GROUP = N_HEADS // N_KV_HEADS
ATTN_WIDTH = N_HEADS * HEAD_DIM
KV_WIDTH = N_KV_HEADS * HEAD_DIM
KV_ROW = 2 * KV_WIDTH
CMP_BLOCK = 64
CMP_HIDDEN = 128
N_SEL = 16
WINDOW = 512
ROT_DIM = HEAD_DIM // 4
ROT_HALF = ROT_DIM // 2
ROPE_THETA = 500000.0
RNN_WIDTH = D_MODEL - ATTN_WIDTH
CONV_W = 4
LRU_C = 8.0
D_FF = 2816
PAGE_SIZE = 128
BLOCKS_PER_PAGE = PAGE_SIZE // CMP_BLOCK
ROW_PITCH = CMP_BLOCK + 8
LSTEP = 8
SCORE_LOOKAHEAD = {"sel": 4, "win": 8}
SAMPLE_ROWS = 16
ATTN_SCALE = HEAD_DIM ** -0.5
LOG2_E = math.log2(math.e)
EPS = 1e-6
NEG_INF = -1e30
M_INIT = -1e29
FORCE_SCORE = 1e4
LANES = 128
IN_COLS_PAD = 19 * LANES
GATE_COL = 2 * RNN_WIDTH + ATTN_WIDTH + 3 * KV_ROW
VMEM_LIMIT = 56 * 1024 * 1024


def _cp(sem, vmem=VMEM_LIMIT):
    return pltpu.CompilerParams(dimension_semantics=sem, vmem_limit_bytes=vmem)


def _sds(shape, dt=F32):
    return jax.ShapeDtypeStruct(shape, dt)


def _rms(x, g):
    return x * lax.rsqrt(jnp.mean(x * x, axis=-1, keepdims=True) + EPS) * g


def _bdot(a, b):
    return jnp.dot(a.astype(BF16), b, preferred_element_type=F32)


def _dot_nt(a, b, precision=None):
    return lax.dot_general(a, b, (((1,), (1,)), ((), ())), precision=precision, preferred_element_type=F32)


def _ada_kernel(c_ref, w_ref, b_ref, o_ref):
    c = c_ref[...]
    o_ref[...] = _bdot(c * jax.nn.sigmoid(c), w_ref[...]) + b_ref[...]


def _ada(c_all, w_bf, b):
    n = c_all.shape[0]
    return pl.pallas_call(
        _ada_kernel, grid=(6,),
        in_specs=[pl.BlockSpec((n, D_MODEL), lambda j: (0, 0)),
                  pl.BlockSpec((D_MODEL, D_MODEL), lambda j: (0, j)),
                  pl.BlockSpec((1, D_MODEL), lambda j: (0, j))],
        out_specs=pl.BlockSpec((n, D_MODEL), lambda j: (0, j)),
        out_shape=_sds((n, 6 * D_MODEL)), compiler_params=_cp(("arbitrary",)), name="ada")(c_all, w_bf, b)


def _rope(v, cos, s1, s2):
    w = v.shape[-1]
    return v * cos + pltpu.roll(v, w - ROT_HALF, 1) * s1 + pltpu.roll(v, ROT_HALF, 1) * s2


def _pre_kernel(x_ref, shift_ref, scale_ref, ln_ref, w_ref, cos_ref, s1_ref, s2_ref,
                xr_ref, yr_ref, q_ref, qrot_ref, kvc_ref, ks_ref, kw_ref, kvct_ref, kvst_ref, kvwt_ref, g_ref):
    x = x_ref[0]
    h = _rms(x, ln_ref[...]) * (1.0 + scale_ref[0]) + shift_ref[0]
    z = _bdot(h, w_ref[...])
    xr_ref[0] = z[:, 0:RNN_WIDTH]
    yr_ref[0] = z[:, RNN_WIDTH:2 * RNN_WIDTH]
    q = z[:, 2 * RNN_WIDTH:2 * RNN_WIDTH + ATTN_WIDTH]
    q_ref[0] = q
    cos, s1, s2 = cos_ref[...], s1_ref[...], s2_ref[...]
    rep = ATTN_WIDTH // LANES
    qrot_ref[0] = _rope(q, jnp.concatenate([cos] * rep, -1), jnp.concatenate([s1] * rep, -1),
                        jnp.concatenate([s2] * rep, -1))
    c0 = 2 * RNN_WIDTH + ATTN_WIDTH
    kvc = z[:, c0:c0 + KV_ROW]
    kvc_ref[0] = kvc
    kvct_ref[0] = kvc.T
    tm = x.shape[0]
    for i, kref, ref in ((1, ks_ref, kvst_ref), (2, kw_ref, kvwt_ref)):
        kv = z[:, c0 + i * KV_ROW:c0 + (i + 1) * KV_ROW]
        keys = _rope(kv[:, :KV_WIDTH], cos, s1, s2)
        if i == 1:
            blk = (pl.program_id(1) * tm + lax.broadcasted_iota(jnp.int32, (tm, HEAD_DIM), 0)) // CMP_BLOCK
            onehot = jnp.where(lax.broadcasted_iota(jnp.int32, (tm, HEAD_DIM), 1) == blk, 1.0, 0.0)
            kref[0] = jnp.concatenate([keys[:, :HEAD_DIM], onehot, keys[:, HEAD_DIM:], onehot], axis=-1)
        else:
            kref[0] = keys
        ref[0, 0:KV_WIDTH, :] = keys.T
        ref[0, KV_WIDTH:KV_ROW, :] = kv[:, KV_WIDTH:].T
    g_ref[0] = jax.nn.sigmoid(z[:, GATE_COL:GATE_COL + LANES])


def _pre(x3, shift, scale, ln, w_in_bf, cos, s1, s2, tm):
    bx, sx, _ = x3.shape
    sm = shift.shape[1]
    if sm == 1:
        mod_spec = pl.BlockSpec((1, 1, D_MODEL), lambda b, t: (b, 0, 0))
    else:
        mod_spec = pl.BlockSpec((1, tm, D_MODEL), lambda b, t: (b, t, 0))
    tab = pl.BlockSpec((tm, LANES), lambda b, t: (t, 0))
    row = lambda w: (pl.BlockSpec((1, tm, w), lambda b, t: (b, t, 0)), _sds((bx, sx, w)))
    col = (pl.BlockSpec((1, KV_ROW, tm), lambda b, t: (b, 0, t)), _sds((bx, KV_ROW, sx)))
    outs = [row(RNN_WIDTH), row(RNN_WIDTH), row(ATTN_WIDTH), row(ATTN_WIDTH), row(KV_ROW), row(2 * KV_WIDTH),
            row(KV_WIDTH), col, col, col, row(LANES)]
    return pl.pallas_call(
        _pre_kernel, grid=(bx, sx // tm),
        in_specs=[pl.BlockSpec((1, tm, D_MODEL), lambda b, t: (b, t, 0)), mod_spec, mod_spec,
                  pl.BlockSpec((1, D_MODEL), lambda b, t: (0, 0)),
                  pl.BlockSpec((D_MODEL, IN_COLS_PAD), lambda b, t: (0, 0)), tab, tab, tab],
        out_specs=[o[0] for o in outs], out_shape=[o[1] for o in outs],
        compiler_params=_cp(("arbitrary", "arbitrary")), name="pre")(x3, shift, scale, ln, w_in_bf, cos, s1, s2)


def _lru_gates(xc, wax_ref, bax_ref, lam_ref):
    ra = _bdot(xc, wax_ref[...]) + bax_ref[...]
    r = jax.nn.sigmoid(ra[:, :RNN_WIDTH])
    i = jax.nn.sigmoid(ra[:, RNN_WIDTH:])
    nl = -lam_ref[...]
    softplus = jnp.maximum(nl, 0.0) + jnp.log(1.0 + jnp.exp(-jnp.abs(nl)))
    a = jnp.exp(-LRU_C * r * softplus)
    return a, jnp.sqrt(1.0 - a * a) * (i * xc)


def _scan_rows(a, b):
    t, c = a.shape
    row = lax.broadcasted_iota(jnp.int32, a.shape, 0)
    k = 1
    while k < t:
        if k < 8:
            a_s = jnp.where(row >= k, pltpu.roll(a, k, 0), 1.0)
            b_s = jnp.where(row >= k, pltpu.roll(b, k, 0), 0.0)
        else:
            a_s = jnp.concatenate([jnp.ones((k, c), F32), a[:t - k]], axis=0)
            b_s = jnp.concatenate([jnp.zeros((k, c), F32), b[:t - k]], axis=0)
        b = a * b_s + b
        a = a * a_s
        k *= 2
    return a, b


def _rglru_kernel(xr_ref, yr_ref, cw_ref, cb_ref, wax_ref, bax_ref, lam_ref, o_ref, hl_ref, xbuf, hc, *, t):
    @pl.when(pl.program_id(1) == 0)
    def _():
        xbuf[0:8, :] = jnp.zeros((8, RNN_WIDTH), F32)
        hc[...] = jnp.zeros_like(hc)

    xbuf[8:8 + t, :] = xr_ref[0]
    xc = cb_ref[...]
    for k in range(CONV_W):
        xc = xc + xbuf[5 + k:5 + k + t, :] * cw_ref[k:k + 1, :]
    xbuf[0:8, :] = xbuf[t:t + 8, :]
    a, b = _lru_gates(xc, wax_ref, bax_ref, lam_ref)
    ac, bc = _scan_rows(a, b)
    h = ac * hc[...] + bc
    hc[...] = h[t - 1:t, :]
    hl_ref[0] = h[t - 1:t, :]
    o_ref[0] = h * jax.nn.gelu(yr_ref[0])


def _rglru_prompt(xr, yr, cw, cb, wax, bax, lam, t=256):
    b, s, _ = xr.shape
    tile = pl.BlockSpec((1, t, RNN_WIDTH), lambda i, j: (i, j, 0))
    full = lambda shp: pl.BlockSpec(shp, lambda i, j: (0,) * len(shp))
    return pl.pallas_call(
        functools.partial(_rglru_kernel, t=t), grid=(b, s // t),
        in_specs=[tile, tile, full((CONV_W, RNN_WIDTH)), full((1, RNN_WIDTH)),
                  full((RNN_WIDTH, 2 * RNN_WIDTH)), full((1, 2 * RNN_WIDTH)), full((1, RNN_WIDTH))],
        out_specs=[tile, pl.BlockSpec((1, 1, RNN_WIDTH), lambda i, j: (i, 0, 0))],
        out_shape=[_sds((b, s, RNN_WIDTH)), _sds((b, 1, RNN_WIDTH))],
        scratch_shapes=[pltpu.VMEM((t + 8, RNN_WIDTH), F32), pltpu.VMEM((1, RNN_WIDTH), F32)],
        compiler_params=_cp(("arbitrary", "arbitrary")), name="rglru_prompt")(xr, yr, cw, cb, wax, bax, lam)


def _rglru_sample_kernel(xr_ref, yr_ref, cp_ref, h0_ref, cw_ref, cb_ref, wax_ref, bax_ref, lam_ref,
                         o_ref, hl_ref, *, s, nb):
    xpad = [cp_ref[i] for i in range(CONV_W - 1)] + [xr_ref[i] for i in range(s)]
    xcs = []
    for j in range(s):
        xc = cb_ref[...]
        for k in range(CONV_W):
            xc = xc + xpad[j + k] * cw_ref[k:k + 1, :]
        xcs.append(xc)
    a, b = _lru_gates(jnp.concatenate(xcs, axis=0), wax_ref, bax_ref, lam_ref)
    h = h0_ref[...]
    for j in range(s):
        h = a[j * nb:(j + 1) * nb] * h + b[j * nb:(j + 1) * nb]
        o_ref[j] = h * jax.nn.gelu(yr_ref[j])
    hl_ref[...] = h


def _rglru_sample(xr_t, yr_t, cp_t, h0, cw, cb, wax, bax, lam):
    s, nb, _ = xr_t.shape
    return pl.pallas_call(
        functools.partial(_rglru_sample_kernel, s=s, nb=nb),
        out_shape=[_sds((s, nb, RNN_WIDTH)), _sds((nb, RNN_WIDTH))],
        compiler_params=pltpu.CompilerParams(vmem_limit_bytes=VMEM_LIMIT), name="rglru_sample")(
            xr_t, yr_t, cp_t, h0, cw, cb, wax, bax, lam)


def _build_w1bd(w1_ref, w1bd):
    w1bd[...] = jnp.zeros_like(w1bd)
    for c in range(2):
        for k in range(N_KV_HEADS):
            r0 = c * KV_WIDTH + k * HEAD_DIM
            f0 = (c * N_KV_HEADS + k) * CMP_HIDDEN
            w1bd[:, r0:r0 + HEAD_DIM, f0:f0 + CMP_HIDDEN] = w1_ref[c]


def _compress_blocks(get_rows, m, pe_ref, w1bd, w2_ref, acc):
    acc[...] = jnp.zeros_like(acc)

    def body(l, carry):
        z = get_rows(l) + pe_ref[pl.ds(l, 1), :]
        acc[...] += _bdot(z, w1bd[l])
        return carry

    lax.fori_loop(0, CMP_BLOCK, body, 0)
    return _bdot(jax.nn.gelu(acc[...]), w2_ref[...])


def _compress_prompt_kernel(kv_ref, pe_ref, w1_ref, w2_ref, o_ref, w1bd, acc, *, nblk):
    @pl.when(pl.program_id(0) == 0)
    def _():
        _build_w1bd(w1_ref, w1bd)

    o_ref[0] = _compress_blocks(lambda l: kv_ref[0, :, l, :], nblk, pe_ref, w1bd, w2_ref, acc)


def _compress_prompt(kvc, pe_row, w1_bf, w2bd_bf):
    b, s, _ = kvc.shape
    nblk = s // CMP_BLOCK
    kv4 = kvc.reshape(b, nblk, CMP_BLOCK, KV_ROW)
    return pl.pallas_call(
        functools.partial(_compress_prompt_kernel, nblk=nblk), grid=(b,),
        in_specs=[pl.BlockSpec((1, nblk, CMP_BLOCK, KV_ROW), lambda i: (i, 0, 0, 0)),
                  pl.BlockSpec((CMP_BLOCK, KV_ROW), lambda i: (0, 0)),
                  pl.BlockSpec((2, CMP_BLOCK, HEAD_DIM, CMP_HIDDEN), lambda i: (0, 0, 0, 0)),
                  pl.BlockSpec((4 * CMP_HIDDEN, KV_ROW), lambda i: (0, 0))],
        out_specs=pl.BlockSpec((1, nblk, KV_ROW), lambda i: (i, 0, 0)),
        out_shape=_sds((b, nblk, KV_ROW)),
        scratch_shapes=[pltpu.VMEM((CMP_BLOCK, KV_ROW, 4 * CMP_HIDDEN), BF16),
                        pltpu.VMEM((nblk, 4 * CMP_HIDDEN), F32)],
        compiler_params=_cp(("arbitrary",)), name="compress_prompt")(kv4, pe_row, w1_bf, w2bd_bf)


def _page_copies(pt_ref, pool_ref, dst, sem, b, slot, n_pages):
    return [pltpu.make_async_copy(pool_ref.at[pt_ref[b, p]], dst(slot, p), sem.at[slot]) for p in range(n_pages)]


def _paged_prefetch(pt_ref, pool_ref, dst, sem, n_pages):
    b = pl.program_id(0)
    nb = pl.num_programs(0)
    slot = lax.rem(b, 2)

    @pl.when(b == 0)
    def _():
        for cp in _page_copies(pt_ref, pool_ref, dst, sem, 0, 0, n_pages):
            cp.start()

    @pl.when(b + 1 < nb)
    def _():
        for cp in _page_copies(pt_ref, pool_ref, dst, sem, b + 1, 1 - slot, n_pages):
            cp.start()

    for cp in _page_copies(pt_ref, pool_ref, dst, sem, b, slot, n_pages):
        cp.wait()
    return slot


def _compress_sample_kernel(pt_ref, pool_ref, new_ref, pe_ref, w1_ref, w2_ref, o_ref,
                            stage, sem, rows, w1s, acc, *, n_pages, m, nbp):
    nblk = n_pages * BLOCKS_PER_PAGE

    @pl.when(pl.program_id(0) == 0)
    def _():
        w1s[...] = jnp.zeros_like(w1s)
        for c in range(2):
            for k in range(N_KV_HEADS):
                w1s[c, :, k * HEAD_DIM:(k + 1) * HEAD_DIM, k * CMP_HIDDEN:(k + 1) * CMP_HIDDEN] = w1_ref[c]
            rows[c, nblk * ROW_PITCH:m * ROW_PITCH, :] = jnp.zeros(((m - nblk) * ROW_PITCH, KV_WIDTH), F32)

    slot = _paged_prefetch(pt_ref, pool_ref, lambda s, p: stage.at[s, p], sem, n_pages)

    def to_rows(p, carry):
        r0 = pl.multiple_of(p * (BLOCKS_PER_PAGE * ROW_PITCH), 8)
        for c in range(2):
            t = stage[slot, p, c * KV_WIDTH:(c + 1) * KV_WIDTH, :].T
            for j in range(BLOCKS_PER_PAGE):
                rows[c, pl.ds(r0 + j * ROW_PITCH, CMP_BLOCK), :] = t[j * CMP_BLOCK:(j + 1) * CMP_BLOCK]
        return carry

    lax.fori_loop(0, n_pages, to_rows, 0, unroll=4)
    for c in range(2):
        rows[c, nblk * ROW_PITCH:nblk * ROW_PITCH + 8, :] = new_ref[0, :, c * KV_WIDTH:(c + 1) * KV_WIDTH]
    acc[...] = jnp.zeros_like(acc)

    def body(i, carry):
        l0 = pl.multiple_of(i * LSTEP, LSTEP)
        pe = pe_ref[pl.ds(l0, LSTEP), :]
        for c in range(2):
            z = jnp.concatenate(
                [rows[c, pl.ds(l0 + j, m, stride=ROW_PITCH), :] + pe[j:j + 1, c * KV_WIDTH:(c + 1) * KV_WIDTH]
                 for j in range(LSTEP)], axis=1)
            w = w1s[c, pl.ds(l0, LSTEP)].reshape(LSTEP * KV_WIDTH, N_KV_HEADS * CMP_HIDDEN)
            acc[c] += _bdot(z, w)
        return carry

    lax.fori_loop(0, CMP_BLOCK // LSTEP, body, 0)
    hw = N_KV_HEADS * CMP_HIDDEN
    for c in range(2):
        o_ref[0, 0:m, c * KV_WIDTH:(c + 1) * KV_WIDTH] = _bdot(
            jax.nn.gelu(acc[c]), w2_ref[c * hw:(c + 1) * hw, c * KV_WIDTH:(c + 1) * KV_WIDTH])
    o_ref[0, m:nbp, :] = jnp.zeros((nbp - m, KV_ROW), F32)


def _compress_sample(page_table, pool_t, kvc_new8, pe_row, w1_bf, w2bd_bf, nbp):
    b, n_pages = page_table.shape
    m = n_pages * BLOCKS_PER_PAGE + 8
    grid_spec = pltpu.PrefetchScalarGridSpec(
        num_scalar_prefetch=1, grid=(b,),
        in_specs=[pl.BlockSpec(memory_space=pl.ANY),
                  pl.BlockSpec((1, 8, KV_ROW), lambda i, pt: (i, 0, 0)),
                  pl.BlockSpec((CMP_BLOCK, KV_ROW), lambda i, pt: (0, 0)),
                  pl.BlockSpec((2, CMP_BLOCK, HEAD_DIM, CMP_HIDDEN), lambda i, pt: (0, 0, 0, 0)),
                  pl.BlockSpec((4 * CMP_HIDDEN, KV_ROW), lambda i, pt: (0, 0))],
        out_specs=pl.BlockSpec((1, nbp, KV_ROW), lambda i, pt: (i, 0, 0)),
        scratch_shapes=[pltpu.VMEM((2, n_pages, KV_ROW, PAGE_SIZE), F32), pltpu.SemaphoreType.DMA((2,)),
                        pltpu.VMEM((2, m * ROW_PITCH, KV_WIDTH), F32),
                        pltpu.VMEM((2, CMP_BLOCK, KV_WIDTH, N_KV_HEADS * CMP_HIDDEN), BF16),
                        pltpu.VMEM((2, m, N_KV_HEADS * CMP_HIDDEN), F32)])
    return pl.pallas_call(
        functools.partial(_compress_sample_kernel, n_pages=n_pages, m=m, nbp=nbp),
        grid_spec=grid_spec, out_shape=_sds((b, nbp, KV_ROW)),
        compiler_params=_cp(("arbitrary",)), name="compress_sample")(page_table, pool_t, kvc_new8, pe_row, w1_bf, w2bd_bf)


def _head_lanes(q, h, dst, scale):
    src = q[:, (h // 2) * LANES:(h // 2 + 1) * LANES]
    if (h % 2) != dst:
        src = pltpu.roll(src, HEAD_DIM, 1)
    lane = lax.broadcasted_iota(jnp.int32, src.shape, 1)
    keep = (lane >= dst * HEAD_DIM) & (lane < (dst + 1) * HEAD_DIM)
    return jnp.where(keep, src * scale, 0.0)


def _arrange_q(q, h, scale):
    return _head_lanes(q, h, h // GROUP, scale)


def _gather_heads(o_list):
    parts = []
    for h, o in enumerate(o_list):
        kvh = h // GROUP
        parts.append(o[:, kvh * HEAD_DIM:(kvh + 1) * HEAD_DIM])
    return jnp.concatenate(parts, axis=-1)


def _select_mask_t(imp, qpos, nb):
    j = lax.broadcasted_iota(jnp.int32, imp.shape, 0)
    cur = qpos // CMP_BLOCK
    forced = (j == 0) | (j == cur) | (j == cur - 1)
    score = jnp.where(j > cur, -FORCE_SCORE, jnp.where(forced, FORCE_SCORE, imp))
    if imp.shape[0] > nb:
        score = jnp.where(j < nb, score, -3e38)
    sub = 8
    ngrp = imp.shape[0] // sub
    groups = [score[r * sub:(r + 1) * sub] for r in range(ngrp)]
    jsub = lax.broadcasted_iota(jnp.int32, groups[0].shape, 0)
    ranks = [jnp.zeros(groups[0].shape, F32) for _ in range(ngrp)]
    for i in range(nb):
        row = score[i:i + 1]
        for r in range(ngrp):
            if r * sub > i:
                hit = row >= groups[r]
            elif (r + 1) * sub - 1 <= i:
                hit = row > groups[r]
            else:
                hit = ((jsub + r * sub > i) & (row >= groups[r])) | (row > groups[r])
            ranks[r] = ranks[r] + jnp.where(hit, 1.0, 0.0)
    rank = jnp.concatenate(ranks, axis=0)
    return jnp.where((rank < float(min(N_SEL, nb))) & (j <= cur) & (j < nb), 1.0, 0.0)


def _cmp_kernel_t(q_ref, ckv_ref, oc_ref, selt_ref, *, tq, nb, nbp, pos0):
    qpos = pos0 + pl.program_id(1) * tq + lax.broadcasted_iota(jnp.int32, (nb, tq), 1)
    j = lax.broadcasted_iota(jnp.int32, (nb, tq), 0)
    m = (j + 1) * CMP_BLOCK - 1 <= qpos
    q = q_ref[0]
    ck = ckv_ref[0, :, 0:KV_WIDTH]
    cv = ckv_ref[0, :, KV_WIDTH:KV_ROW].astype(BF16)
    m4 = jnp.concatenate([m] * GROUP, axis=1)
    outs = []
    for kvh in range(N_KV_HEADS):
        qa = jnp.concatenate([_arrange_q(q, kvh * GROUP + g, ATTN_SCALE) for g in range(GROUP)], axis=0)
        s = jnp.where(m4, _dot_nt(ck, qa, precision=lax.Precision.HIGHEST), NEG_INF)
        e = jnp.exp(s - jnp.max(s, axis=0, keepdims=True))
        p = jnp.where(m4, e / jnp.sum(e, axis=0, keepdims=True), 0.0)
        imp = p[:, 0:tq]
        for g in range(1, GROUP):
            imp = imp + p[:, g * tq:(g + 1) * tq]
        o = lax.dot_general(p.astype(BF16), cv, (((0,), (0,)), ((), ())), preferred_element_type=F32)
        outs += [o[g * tq:(g + 1) * tq] for g in range(GROUP)]
        selt_ref[0, kvh, 0:nb, :] = _select_mask_t(imp, qpos, nb)
        selt_ref[0, kvh, nb:nbp, :] = jnp.zeros((nbp - nb, tq), F32)
    oc_ref[0] = _gather_heads(outs)


def _cmp_select_t(q, ckv, nbp, pos0, tq):
    b, sq, _ = q.shape
    nb = ckv.shape[1]
    return pl.pallas_call(
        functools.partial(_cmp_kernel_t, tq=tq, nb=nb, nbp=nbp, pos0=pos0), grid=(b, sq // tq),
        in_specs=[pl.BlockSpec((1, tq, ATTN_WIDTH), lambda i, t: (i, t, 0)),
                  pl.BlockSpec((1, nb, KV_ROW), lambda i, t: (i, 0, 0))],
        out_specs=[pl.BlockSpec((1, tq, ATTN_WIDTH), lambda i, t: (i, t, 0)),
                   pl.BlockSpec((1, N_KV_HEADS, nbp, tq), lambda i, t: (i, 0, 0, t))],
        out_shape=[_sds((b, sq, ATTN_WIDTH)), _sds((b, N_KV_HEADS, nbp, sq))],
        compiler_params=_cp(("arbitrary", "arbitrary")), name="cmp_select_t")(q, ckv)


def _cmp_sample_kernel(q_ref, ckv_ref, oc_ref, selt_ref, *, nb, nbr, pos0):
    nq = 8
    lanes = GROUP * nq
    j = lax.broadcasted_iota(jnp.int32, (nbr, lanes), 0)
    qpos = pos0 + lax.rem(lax.broadcasted_iota(jnp.int32, (nbr, lanes), 1), nq)
    m = ((j + 1) * CMP_BLOCK - 1 <= qpos) & (j < nb)
    imps = [[] for _ in range(N_KV_HEADS)]
    for i in range(q_ref.shape[0]):
        q = q_ref[i]
        ck = ckv_ref[i, 0:nbr, 0:KV_WIDTH]
        cv = ckv_ref[i, 0:nbr, KV_WIDTH:KV_ROW].astype(BF16)
        outs = []
        for kvh in range(N_KV_HEADS):
            qa = jnp.concatenate([_arrange_q(q, kvh * GROUP + g, ATTN_SCALE) for g in range(GROUP)], axis=0)
            s = jnp.where(m, _dot_nt(ck, qa, precision=lax.Precision.HIGHEST), NEG_INF)
            e = jnp.exp(s - jnp.max(s, axis=0, keepdims=True))
            p = jnp.where(m, e / jnp.sum(e, axis=0, keepdims=True), 0.0)
            imp = p[:, 0:nq]
            for g in range(1, GROUP):
                imp = imp + p[:, g * nq:(g + 1) * nq]
            imps[kvh].append(imp)
            o = lax.dot_general(p.astype(BF16), cv, (((0,), (0,)), ((), ())), preferred_element_type=F32)
            outs += [o[g * nq:(g + 1) * nq] for g in range(GROUP)]
        oc_ref[i] = _gather_heads(outs)
    qpos_all = pos0 + lax.rem(lax.broadcasted_iota(jnp.int32, (nbr, LANES), 1), nq)
    for kvh in range(N_KV_HEADS):
        selt_ref[0, kvh] = _select_mask_t(jnp.concatenate(imps[kvh], axis=1), qpos_all, nb)


def _cmp_select_sample(q8, ckv_p, nb, pos0):
    b = q8.shape[0]
    nbp = ckv_p.shape[1]
    nbr = 8 * (-(-nb // 8))
    rows = SAMPLE_ROWS
    assert b % rows == 0
    o_c, sel_t = pl.pallas_call(
        functools.partial(_cmp_sample_kernel, nb=nb, nbr=nbr, pos0=pos0), grid=(b // rows,),
        in_specs=[pl.BlockSpec((rows, 8, ATTN_WIDTH), lambda i: (i, 0, 0)),
                  pl.BlockSpec((rows, nbp, KV_ROW), lambda i: (i, 0, 0))],
        out_specs=[pl.BlockSpec((rows, 8, ATTN_WIDTH), lambda i: (i, 0, 0)),
                   pl.BlockSpec((1, N_KV_HEADS, nbr, LANES), lambda i: (i, 0, 0, 0))],
        out_shape=[_sds((b, 8, ATTN_WIDTH)), _sds((b // rows, N_KV_HEADS, nbr, LANES))],
        compiler_params=_cp(("arbitrary",)), name="cmp_select_sample")(q8, ckv_p)
    sel = jnp.transpose(sel_t.reshape(b // rows, N_KV_HEADS, nbr, rows, 8), (0, 3, 1, 4, 2))
    return o_c, jnp.pad(sel.reshape(b, N_KV_HEADS, 8, nbr), ((0, 0), (0, 0), (0, 0), (0, nbp - nbr)))


def _flash_kernel(tab_ref, q_ref, k_ref, vt_ref, *rest, tq, tk, mode):
    if mode == "sel":
        selt_ref, o_ref, qt, m_s, l_s, acc = rest
    else:
        o_ref, qt, m_s, l_s, acc = rest
    step_id = pl.program_id(1)
    qi = tab_ref[0, step_id]
    ki = tab_ref[1, step_id]
    flags = tab_ref[2, step_id]

    @pl.when((flags & 1) != 0)
    def _():
        q = q_ref[0]
        for h in range(N_HEADS):
            kvh = h // GROUP
            if mode == "sel":
                cols = _head_lanes(q, h, 0, ATTN_SCALE * LOG2_E).T
                bias = (selt_ref[0, kvh, 0:HEAD_DIM, :] - 1.0) * (-NEG_INF)
                cols = jnp.concatenate([cols[0:HEAD_DIM], bias], axis=0)
            else:
                cols = _head_lanes(q, h, kvh, ATTN_SCALE * LOG2_E).T
            qt[:, h * tq:(h + 1) * tq] = cols.astype(BF16)
        m_s[...] = jnp.full_like(m_s, M_INIT)
        l_s[...] = jnp.zeros_like(l_s)
        acc[...] = jnp.zeros_like(acc)

    def step(masked):
        vt = vt_ref[0].astype(BF16)
        if masked:
            d = (lax.broadcasted_iota(jnp.int32, (tk, tq), 0) - lax.broadcasted_iota(jnp.int32, (tk, tq), 1)
                 + (ki * tk - qi * tq))
            valid = (d <= 0) if mode == "sel" else (d <= 0) & (d > -WINDOW)
        if mode == "sel":
            ks = [k_ref[0, :, kvh * LANES:(kvh + 1) * LANES].astype(BF16) for kvh in range(N_KV_HEADS)]
        else:
            ks = [k_ref[0].astype(BF16)] * N_KV_HEADS

        def scores(h):
            return jnp.dot(ks[h // GROUP], qt[:, h * tq:(h + 1) * tq], preferred_element_type=F32)

        def accumulate(h, alpha, pv):
            rows = slice(h * LANES, (h + 1) * LANES)
            acc[rows] = alpha * acc[rows] + pv

        look = SCORE_LOOKAHEAD[mode]
        ahead = [scores(h) for h in range(look)]
        pending = None
        for h in range(N_HEADS):
            s = ahead.pop(0)
            if h + look < N_HEADS:
                ahead.append(scores(h + look))
            if masked:
                s = jnp.where(valid, s, NEG_INF)
            m_old = m_s[h, 0:1, :]
            m_new = jnp.maximum(m_old, jnp.max(s, axis=0, keepdims=True))
            alpha = jnp.exp2(m_old - m_new)
            p = jnp.exp2(s - m_new)
            l_s[h] = jnp.broadcast_to(alpha * l_s[h, 0:1, :] + jnp.sum(p, axis=0, keepdims=True), (8, tq))
            m_s[h] = jnp.broadcast_to(m_new, (8, tq))
            pv = jnp.dot(vt, p.astype(BF16), preferred_element_type=F32)
            if pending is not None:
                accumulate(*pending)
            pending = (h, alpha, pv)
        accumulate(*pending)

    @pl.when((flags & 4) != 0)
    def _():
        step(True)

    @pl.when((flags & 4) == 0)
    def _():
        step(False)

    @pl.when((flags & 2) != 0)
    def _():
        outs = [(acc[h * LANES:(h + 1) * LANES] / l_s[h, 0:1, :]).T for h in range(N_HEADS)]
        o_ref[0] = _gather_heads(outs)


def _flash_schedule(s, tq, tk, mode):
    steps = []
    for qi in range(s // tq):
        q_lo, q_hi = qi * tq, qi * tq + tq - 1
        k_lo = 0 if mode == "sel" else max(q_lo - (WINDOW - 1), 0)
        tiles = list(range(k_lo // tk, q_hi // tk + 1))
        for ki in tiles:
            causal_edge = ki * tk + tk - 1 > q_lo
            window_edge = mode == "win" and q_hi - ki * tk >= WINDOW
            flags = (1 if ki == tiles[0] else 0) | (2 if ki == tiles[-1] else 0) | (4 if causal_edge or window_edge else 0)
            steps.append((qi, ki, flags))
    return jnp.asarray(np.array(steps, np.int32).T)


def _flash_prompt(qrot, k_rows, kv_t, sel_t, mode, tq, tk):
    b, s, _ = qrot.shape
    if mode == "sel":
        assert s // CMP_BLOCK <= HEAD_DIM
    tab = _flash_schedule(s, tq, tk, mode)
    kw = k_rows.shape[-1]
    in_specs = [pl.BlockSpec((1, tq, ATTN_WIDTH), lambda i, p, t: (i, t[0, p], 0)),
                pl.BlockSpec((1, tk, kw), lambda i, p, t: (i, t[1, p], 0)),
                pl.BlockSpec((1, KV_WIDTH, tk), lambda i, p, t: (i, 1, t[1, p]))]
    args = [qrot, k_rows, kv_t]
    if mode == "sel":
        nbp = sel_t.shape[2]
        in_specs += [pl.BlockSpec((1, N_KV_HEADS, nbp, tq), lambda i, p, t: (i, 0, 0, t[0, p]))]
        args += [sel_t]
    grid_spec = pltpu.PrefetchScalarGridSpec(
        num_scalar_prefetch=1, grid=(b, tab.shape[1]), in_specs=in_specs,
        out_specs=pl.BlockSpec((1, tq, ATTN_WIDTH), lambda i, p, t: (i, t[0, p], 0)),
        scratch_shapes=[pltpu.VMEM((LANES, N_HEADS * tq), BF16), pltpu.VMEM((N_HEADS, 8, tq), F32),
                        pltpu.VMEM((N_HEADS, 8, tq), F32), pltpu.VMEM((N_HEADS * LANES, tq), F32)])
    return pl.pallas_call(
        functools.partial(_flash_kernel, tq=tq, tk=tk, mode=mode), grid_spec=grid_spec,
        out_shape=_sds((b, s, ATTN_WIDTH)),
        compiler_params=_cp(("arbitrary", "arbitrary")), name="flash_" + mode)(tab, *args)


def _sample_q_rows(q8):
    return jnp.concatenate([_arrange_q(q8, h, ATTN_SCALE) for h in range(N_HEADS)], axis=0).astype(BF16)


def _sample_out(o_all):
    return _gather_heads([o_all[h * 8:(h + 1) * 8] for h in range(N_HEADS)])


def _two_part_attention(s_past, s_new, vt_past, vt_new):
    mx = jnp.maximum(jnp.max(s_past, axis=-1, keepdims=True), jnp.max(s_new, axis=-1, keepdims=True))
    p_past = jnp.exp(s_past - mx)
    p_new = jnp.exp(s_new - mx)
    den = jnp.sum(p_past, axis=-1, keepdims=True) + jnp.sum(p_new, axis=-1, keepdims=True)
    return (_dot_nt(p_past.astype(BF16), vt_past) + _dot_nt(p_new.astype(BF16), vt_new)) / den


def _sel_sample_kernel(pt_ref, pool_ref, q_ref, newt_ref, sel_ref, e_ref, o_ref, buf, sem, *, n_pages, pos0, n_new):
    slot = _paged_prefetch(pt_ref, pool_ref, lambda s, p: buf.at[s, :, pl.ds(p * PAGE_SIZE, PAGE_SIZE)], sem, n_pages)
    nkeys = n_pages * PAGE_SIZE
    nblk = nkeys // CMP_BLOCK
    nrow = 8 * N_HEADS
    qall = _sample_q_rows(q_ref[0])
    selrows = jnp.concatenate([sel_ref[0, h // GROUP] for h in range(N_HEADS)], axis=0)
    qpos = pos0 + lax.rem(lax.broadcasted_iota(jnp.int32, (nrow, 1), 0), 8)
    assert nkeys <= pos0
    bias = ((selrows - 1.0) * (-NEG_INF)).astype(BF16)
    s_past = (jnp.dot(qall, buf[slot, 0:KV_WIDTH, :].astype(BF16), preferred_element_type=F32)
              + jnp.dot(bias, e_ref[...], preferred_element_type=F32))
    newt = newt_ref[0].astype(BF16)
    inew = lax.broadcasted_iota(jnp.int32, (nrow, LANES), 1) - (LANES - n_new)
    sel_new = selrows[:, nblk:nblk + 1] > 0.5
    s_new = jnp.dot(qall, newt[0:KV_WIDTH], preferred_element_type=F32)
    s_new = jnp.where(sel_new & (inew >= 0) & (pos0 + inew <= qpos), s_new, NEG_INF)
    o = _two_part_attention(s_past, s_new, buf[slot, KV_WIDTH:KV_ROW, :].astype(BF16), newt[KV_WIDTH:KV_ROW])
    o_ref[0] = _sample_out(o)


def _sel_sample(page_table, pool_t, qrot8, kvs_newt, sel, emat, pos0, n_new):
    b, n_pages = page_table.shape
    nbp = sel.shape[-1]
    nkeys = n_pages * PAGE_SIZE
    grid_spec = pltpu.PrefetchScalarGridSpec(
        num_scalar_prefetch=1, grid=(b,),
        in_specs=[pl.BlockSpec(memory_space=pl.ANY),
                  pl.BlockSpec((1, 8, ATTN_WIDTH), lambda i, pt: (i, 0, 0)),
                  pl.BlockSpec((1, KV_ROW, LANES), lambda i, pt: (i, 0, 0)),
                  pl.BlockSpec((1, N_KV_HEADS, 8, nbp), lambda i, pt: (i, 0, 0, 0)),
                  pl.BlockSpec((nbp, nkeys), lambda i, pt: (0, 0))],
        out_specs=pl.BlockSpec((1, 8, ATTN_WIDTH), lambda i, pt: (i, 0, 0)),
        scratch_shapes=[pltpu.VMEM((2, KV_ROW, nkeys), F32), pltpu.SemaphoreType.DMA((2,))])
    return pl.pallas_call(
        functools.partial(_sel_sample_kernel, n_pages=n_pages, pos0=pos0, n_new=n_new),
        grid_spec=grid_spec, out_shape=_sds((b, 8, ATTN_WIDTH)),
        compiler_params=_cp(("arbitrary",)), name="sel_sample")(page_table, pool_t, qrot8, kvs_newt, sel, emat)


def _win_sample_kernel(q_ref, buf_ref, newt_ref, o_ref, nw_ref, *, wb, n_new):
    nrow = 8 * N_HEADS
    qall = _sample_q_rows(q_ref[0])
    buf = buf_ref[0]
    newt = newt_ref[0]
    srow = lax.rem(lax.broadcasted_iota(jnp.int32, (nrow, 1), 0), 8)
    i_past = lax.broadcasted_iota(jnp.int32, (nrow, wb), 1)
    dp = wb + srow - i_past
    s_past = jnp.dot(qall, buf[0:KV_WIDTH].astype(BF16), preferred_element_type=F32)
    s_past = jnp.where((dp >= 0) & (dp < WINDOW), s_past, NEG_INF)
    i_new = lax.broadcasted_iota(jnp.int32, (nrow, LANES), 1) - (LANES - n_new)
    dn = srow - i_new
    s_new = jnp.dot(qall, newt[0:KV_WIDTH].astype(BF16), preferred_element_type=F32)
    s_new = jnp.where((i_new >= 0) & (dn >= 0) & (dn < WINDOW), s_new, NEG_INF)
    o = _two_part_attention(s_past, s_new, buf[KV_WIDTH:KV_ROW].astype(BF16), newt[KV_WIDTH:KV_ROW].astype(BF16))
    o_ref[0] = _sample_out(o)
    nw_ref[0] = pltpu.roll(buf, wb - n_new, 1)
    lane = lax.broadcasted_iota(jnp.int32, (KV_ROW, LANES), 1)
    nw_ref[0, :, wb - LANES:wb] = jnp.where(lane >= LANES - n_new, newt,
                                            pltpu.roll(buf[:, wb - LANES:wb], LANES - n_new, 1))


def _win_sample(qrot8, win_t, kvw_newt, n_new):
    b, _, wb = win_t.shape
    return pl.pallas_call(
        functools.partial(_win_sample_kernel, wb=wb, n_new=n_new), grid=(b,),
        in_specs=[pl.BlockSpec((1, 8, ATTN_WIDTH), lambda i: (i, 0, 0)),
                  pl.BlockSpec((1, KV_ROW, wb), lambda i: (i, 0, 0)),
                  pl.BlockSpec((1, KV_ROW, LANES), lambda i: (i, 0, 0))],
        out_specs=[pl.BlockSpec((1, 8, ATTN_WIDTH), lambda i: (i, 0, 0)),
                   pl.BlockSpec((1, KV_ROW, wb), lambda i: (i, 0, 0))],
        out_shape=[_sds((b, 8, ATTN_WIDTH)), _sds((b, KV_ROW, wb))],
        compiler_params=_cp(("arbitrary",)), name="win_sample")(qrot8, win_t, kvw_newt)


def _post_kernel(x_ref, or_ref, oc_ref, os_ref, ow_ref, g_ref, m2_ref, m3_ref, m4_ref, m5_ref,
                 gnr_ref, gna_ref, ln2_ref, fin_ref, eg_ref, wo_ref, wg_ref, wu_ref, wd_ref, y_ref, *, ff_chunks):
    g = g_ref[0]
    g_hi = g.astype(BF16)
    g_lo = (g - g_hi.astype(F32)).astype(BF16)
    gx = jnp.dot(jnp.concatenate([g_hi, g_lo], axis=-1), eg_ref[...], preferred_element_type=F32)
    attn = (gx[:, 0:ATTN_WIDTH] * oc_ref[0] + gx[:, ATTN_WIDTH:2 * ATTN_WIDTH] * os_ref[0]
            + gx[:, 2 * ATTN_WIDTH:3 * ATTN_WIDTH] * ow_ref[0])
    mix = (_bdot(_rms(or_ref[0], gnr_ref[...]), wo_ref[0:RNN_WIDTH, :])
           + _bdot(_rms(attn, gna_ref[...]), wo_ref[RNN_WIDTH:D_MODEL, :]))
    x1 = x_ref[0] + m2_ref[0] * mix
    h = (_rms(x1, ln2_ref[...]) * (1.0 + m4_ref[0]) + m3_ref[0]).astype(BF16)
    cw = D_FF // ff_chunks
    ff = jnp.zeros(x1.shape, F32)
    for c in range(ff_chunks):
        gate = jnp.dot(h, wg_ref[:, c * cw:(c + 1) * cw], preferred_element_type=F32)
        up = jnp.dot(h, wu_ref[:, c * cw:(c + 1) * cw], preferred_element_type=F32)
        act = gate * jax.nn.sigmoid(gate) * up
        ff = ff + _bdot(act, wd_ref[c * cw:(c + 1) * cw, :])
    y = x1 + m5_ref[0] * ff
    y_ref[0] = _rms(y, fin_ref[...])


def _post(x3, o_r, o_c, o_s, o_w, g, mods, gnr, gna, ln2, fin, egate, wo, wg, wu, wd, tm, ff_chunks=2):
    bx, sx, _ = x3.shape
    sm = mods[0].shape[1]
    if sm == 1:
        mod_spec = pl.BlockSpec((1, 1, D_MODEL), lambda b, t: (b, 0, 0))
    else:
        mod_spec = pl.BlockSpec((1, tm, D_MODEL), lambda b, t: (b, t, 0))
    tile = lambda w: pl.BlockSpec((1, tm, w), lambda b, t: (b, t, 0))

    def const(shape):
        return pl.BlockSpec(shape, lambda b, t: (0,) * len(shape), pipeline_mode=pl.Buffered(1))

    return pl.pallas_call(
        functools.partial(_post_kernel, ff_chunks=ff_chunks), grid=(bx, sx // tm),
        in_specs=[tile(D_MODEL), tile(RNN_WIDTH), tile(ATTN_WIDTH), tile(ATTN_WIDTH), tile(ATTN_WIDTH), tile(LANES),
                  mod_spec, mod_spec, mod_spec, mod_spec,
                  const((1, RNN_WIDTH)), const((1, ATTN_WIDTH)), const((1, D_MODEL)), const((1, D_MODEL)),
                  const((2 * LANES, 3 * ATTN_WIDTH)), const((D_MODEL, D_MODEL)),
                  const((D_MODEL, D_FF)), const((D_MODEL, D_FF)), const((D_FF, D_MODEL))],
        out_specs=tile(D_MODEL), out_shape=_sds((bx, sx, D_MODEL)),
        compiler_params=_cp(("arbitrary", "arbitrary")), name="post")(
            x3, o_r, o_c, o_s, o_w, g, mods[2], mods[3], mods[4], mods[5], gnr, gna, ln2, fin, egate, wo, wg, wu, wd)


def _rope_tables(pos):
    inv = jnp.exp(jnp.arange(ROT_HALF, dtype=F32) * (-math.log(ROPE_THETA) / ROT_HALF))
    ang = pos.astype(F32)[:, None] * inv[None, :]
    cos, sin = jnp.cos(ang), jnp.sin(ang)
    n = pos.shape[0]
    one = jnp.ones((n, HEAD_DIM - ROT_DIM), F32)
    zero = jnp.zeros((n, HEAD_DIM - ROT_DIM), F32)
    z8 = jnp.zeros((n, ROT_HALF), F32)
    c = jnp.concatenate([cos, cos, one], axis=-1)
    s1 = jnp.concatenate([-sin, z8, zero], axis=-1)
    s2 = jnp.concatenate([z8, sin, zero], axis=-1)
    return tuple(jnp.concatenate([t, t], axis=-1) for t in (c, s1, s2))


def _blockdiag(w):
    nb, d, e = w.shape
    return jnp.einsum('nde,nm->ndme', w, jnp.eye(nb, dtype=w.dtype)).reshape(nb * d, nb * e)


def _prep_params(w_in, lru_wa, lru_wx, lru_ba, lru_bx, cmp_pe, cmp_w1, cmp_w2):
    w_in_p = jnp.pad(w_in, ((0, 0), (0, IN_COLS_PAD - w_in.shape[1]))).astype(BF16)
    wax = jnp.concatenate([_blockdiag(lru_wa), _blockdiag(lru_wx)], axis=1).astype(BF16)
    bax = jnp.concatenate([lru_ba, lru_bx])[None, :]
    pe_row = jnp.broadcast_to(jnp.transpose(cmp_pe, (1, 0, 2))[:, :, None, :],
                              (CMP_BLOCK, 2, N_KV_HEADS, HEAD_DIM)).reshape(CMP_BLOCK, KV_ROW)
    w2bd = _blockdiag(jnp.repeat(cmp_w2, N_KV_HEADS, axis=0)).astype(BF16)
    e = np.zeros((LANES, 3 * ATTN_WIDTH), np.float32)
    for br in range(3):
        for h in range(N_HEADS):
            e[br * N_HEADS + h, br * ATTN_WIDTH + h * HEAD_DIM:br * ATTN_WIDTH + (h + 1) * HEAD_DIM] = 1.0
    egate = jnp.asarray(np.concatenate([e, e], axis=0), BF16)
    return w_in_p, wax, bax, pe_row, cmp_w1.astype(BF16), w2bd, egate


def _expand_matrix(nbp, n_tiles, tk):
    blk = (np.arange(n_tiles)[:, None, None] * tk + np.arange(tk)[None, None, :]) // CMP_BLOCK
    return jnp.asarray(blk == np.arange(nbp)[None, :, None], BF16)


def _kv_from_t(kv_t):
    n, _, s = kv_t.shape
    return jnp.transpose(kv_t.reshape(n, 2, N_KV_HEADS, HEAD_DIM, s), (0, 4, 1, 2, 3))


def _kv_to_t(kv):
    n, s = kv.shape[:2]
    return jnp.transpose(kv, (0, 2, 3, 4, 1)).reshape(n, KV_ROW, s)


def _layer_prompt(x, mods, P, tm=512, tq=256):
    b, s, _ = x.shape
    cos, s1, s2 = _rope_tables(jnp.arange(s))
    xr, yr, q, qrot, kvc, ks, kw, kvc_t, kvs_t, kvw_t, g = _pre(
        x, mods[0], mods[1], P['ln1'], P['w_in'], cos, s1, s2, tm)
    o_r, h_new = _rglru_prompt(xr, yr, P['conv_w'], P['conv_b'], P['wax'], P['bax'], P['lam'])
    nb = s // CMP_BLOCK
    ckv = _compress_prompt(kvc, P['pe_row'], P['w1'], P['w2bd'])
    nbp = LANES * (-(-nb // LANES))
    o_c, sel_t = _cmp_select_t(q, ckv, nbp, 0, min(256, s))
    o_s = _flash_prompt(qrot, ks, kvs_t, sel_t, "sel", tq, min(512, s))
    o_w = _flash_prompt(qrot, kw, kvw_t, None, "win", tq, tq)
    y = _post(x, o_r, o_c, o_s, o_w, g, mods, P['gn_rnn'], P['gn_attn'], P['ln2'], P['final_g'], P['egate'],
              P['w_out'], P['w_gate'], P['w_up'], P['w_down'], tm)
    wlen = min(WINDOW, s)
    return y, (_kv_from_t(kvc_t), _kv_from_t(kvs_t), _kv_from_t(kvw_t[:, :, s - wlen:]),
               xr[:, s - (CONV_W - 1):], h_new[:, 0])


def _pad_rows8(a):
    return jnp.pad(a, ((0, 0), (0, 8 - a.shape[1]), (0, 0)))


def _layer_sample(x, mods_tok, cmp_pool, sel_pool, win_buf, conv_buf, h0, page_table, P):
    b, s, _ = x.shape
    n_pages = page_table.shape[1]
    past = n_pages * PAGE_SIZE
    nb = past // CMP_BLOCK + 1
    pos = past + jnp.arange(s)
    cos, s1, s2 = (jnp.tile(t, (b, 1)) for t in _rope_tables(pos))
    n = b * s
    flat = lambda a: a.reshape(1, n, a.shape[-1])
    outs = _pre(flat(x), mods_tok[0], mods_tok[1], P['ln1'], P['w_in'], cos, s1, s2, min(512, n))
    xr, yr, q, qrot, kvc = (o.reshape(b, s, o.shape[-1]) for o in outs[:5])
    kvc_t, kvs_t, kvw_t = (jnp.transpose(o[0].reshape(KV_ROW, b, s), (1, 0, 2)) for o in outs[7:10])
    lane_pad = lambda a: jnp.pad(a, ((0, 0), (0, 0), (LANES - s, 0)))
    g = outs[10]
    tmaj = lambda a: jnp.transpose(a, (1, 0, 2))
    o_r_t, h_new = _rglru_sample(tmaj(xr), tmaj(yr), tmaj(conv_buf), h0, P['conv_w'], P['conv_b'],
                                 P['wax'], P['bax'], P['lam'])
    o_r = tmaj(o_r_t)
    nbp = LANES * (-(-nb // LANES))
    ckv_p = _compress_sample(page_table, _kv_to_t(cmp_pool), _pad_rows8(kvc), P['pe_row'], P['w1'], P['w2bd'], nbp)
    q8, qrot8 = _pad_rows8(q), _pad_rows8(qrot)
    o_c8, sel = _cmp_select_sample(q8, ckv_p, nb, past)
    o_s8 = _sel_sample(page_table, _kv_to_t(sel_pool), qrot8, lane_pad(kvs_t), sel,
                       _expand_matrix(nbp, 1, past)[0], past, s)
    o_w8, new_win_t = _win_sample(qrot8, _kv_to_t(win_buf), lane_pad(kvw_t), s)
    y = _post(flat(x), flat(o_r), flat(o_c8[:, :s]), flat(o_s8[:, :s]), flat(o_w8[:, :s]), g, mods_tok,
              P['gn_rnn'], P['gn_attn'], P['ln2'], P['final_g'], P['egate'],
              P['w_out'], P['w_gate'], P['w_up'], P['w_down'], min(512, n))
    return y.reshape(b, s, D_MODEL), (_kv_from_t(kvc_t), _kv_from_t(kvs_t), _kv_from_t(new_win_t),
                                      xr[:, s - (CONV_W - 1):], h_new)


def kernel(x_prompt, x_sample, cache_cmp_kv, cache_sel_kv, state_win_kv, state_conv, state_lru_h, page_table,
           c_prompt, c_sample, ln1_g, ln2_g, w_ada, b_ada, w_in, conv_w, conv_b, lru_wa, lru_ba, lru_wx, lru_bx,
           lru_lambda, cmp_pe, cmp_w1, cmp_w2, gn_rnn, gn_attn, w_out, w_gate, w_up, w_down, final_g):
    depth = w_in.shape[0]
    assert depth == 1
    bp = x_prompt.shape[0]
    bs, ss, _ = x_sample.shape
    l = 0
    w_in_p, wax, bax, pe_row, w1, w2bd, egate = _prep_params(
        w_in[l], lru_wa[l], lru_wx[l], lru_ba[l], lru_bx[l], cmp_pe[l], cmp_w1[l], cmp_w2[l])
    P = {'ln1': ln1_g[l][None], 'ln2': ln2_g[l][None], 'w_in': w_in_p, 'conv_w': conv_w[l], 'conv_b': conv_b[l][None],
         'wax': wax, 'bax': bax, 'lam': lru_lambda[l][None], 'pe_row': pe_row, 'w1': w1, 'w2bd': w2bd,
         'gn_rnn': gn_rnn[l][None], 'gn_attn': gn_attn[l][None], 'egate': egate, 'final_g': final_g[None],
         'w_out': w_out[l].astype(BF16), 'w_gate': w_gate[l].astype(BF16), 'w_up': w_up[l].astype(BF16),
         'w_down': w_down[l].astype(BF16)}
    mods_all = _ada(jnp.concatenate([c_prompt, c_sample], axis=0), w_ada[l].astype(BF16), b_ada[l][None])
    mods_p = [m[:, None, :] for m in jnp.split(mods_all[:bp], 6, axis=-1)]
    mods_s = [jnp.repeat(m, ss, axis=0)[None] for m in jnp.split(mods_all[bp:], 6, axis=-1)]

    yp, st_p = _layer_prompt(x_prompt, mods_p, P)
    ys, st_s = _layer_sample(x_sample, mods_s, cache_cmp_kv[l], cache_sel_kv[l], state_win_kv[l],
                             state_conv[l], state_lru_h[l], page_table, P)

    return (yp, ys, st_p[0][None], st_s[0][None], st_p[1][None], st_s[1][None], st_p[2][None], st_s[2][None],
            st_p[3][None], st_s[3][None], st_p[4][None], st_s[4][None])
```

```python
import functools
import math

import numpy as np
import jax
import jax.numpy as jnp
from jax import lax
from jax.experimental import pallas as pl
from jax.experimental.pallas import tpu as pltpu

F32 = jnp.float32
BF16 = jnp.bfloat16

D_MODEL = 1024
HEAD_DIM = 64
N_HEADS = 8
N_KV_HEADS = 2
GROUP = N_HEADS // N_KV_HEADS
ATTN_WIDTH = N_HEADS * HEAD_DIM
KV_WIDTH = N_KV_HEADS * HEAD_DIM
KV_ROW = 2 * KV_WIDTH
CMP_BLOCK = 64
CMP_HIDDEN = 128
N_SEL = 16
WINDOW = 512
ROT_DIM = HEAD_DIM // 4
ROT_HALF = ROT_DIM // 2
ROPE_THETA = 500000.0
RNN_WIDTH = D_MODEL - ATTN_WIDTH
CONV_W = 4
LRU_C = 8.0
D_FF = 2816
PAGE_SIZE = 128
BLOCKS_PER_PAGE = PAGE_SIZE // CMP_BLOCK
ROW_PITCH = CMP_BLOCK + 8
LSTEP = 8
SCORE_LOOKAHEAD = {"sel": 4, "win": 8}
SAMPLE_ROWS = 16
ATTN_SCALE = HEAD_DIM ** -0.5
LOG2_E = math.log2(math.e)
EPS = 1e-6
NEG_INF = -1e30
M_INIT = -1e29
FORCE_SCORE = 1e4
LANES = 128
IN_COLS_PAD = 19 * LANES
GATE_COL = 2 * RNN_WIDTH + ATTN_WIDTH + 3 * KV_ROW
VMEM_LIMIT = 56 * 1024 * 1024


def _cp(sem, vmem=VMEM_LIMIT):
    return pltpu.CompilerParams(dimension_semantics=sem, vmem_limit_bytes=vmem)


def _sds(shape, dt=F32):
    return jax.ShapeDtypeStruct(shape, dt)


def _rms(x, g):
    return x * lax.rsqrt(jnp.mean(x * x, axis=-1, keepdims=True) + EPS) * g


def _bdot(a, b):
    return jnp.dot(a.astype(BF16), b, preferred_element_type=F32)


def _dot_nt(a, b, precision=None):
    return lax.dot_general(a, b, (((1,), (1,)), ((), ())), precision=precision, preferred_element_type=F32)


def _ada_kernel(c_ref, w_ref, b_ref, o_ref):
    c = c_ref[...]
    o_ref[...] = _bdot(c * jax.nn.sigmoid(c), w_ref[...]) + b_ref[...]


def _ada(c_all, w_bf, b):
    n = c_all.shape[0]
    return pl.pallas_call(
        _ada_kernel, grid=(6,),
        in_specs=[pl.BlockSpec((n, D_MODEL), lambda j: (0, 0)),
                  pl.BlockSpec((D_MODEL, D_MODEL), lambda j: (0, j)),
                  pl.BlockSpec((1, D_MODEL), lambda j: (0, j))],
        out_specs=pl.BlockSpec((n, D_MODEL), lambda j: (0, j)),
        out_shape=_sds((n, 6 * D_MODEL)), compiler_params=_cp(("arbitrary",)), name="ada")(c_all, w_bf, b)


def _rope(v, cos, s1, s2):
    w = v.shape[-1]
    return v * cos + pltpu.roll(v, w - ROT_HALF, 1) * s1 + pltpu.roll(v, ROT_HALF, 1) * s2


def _pre_kernel(x_ref, shift_ref, scale_ref, ln_ref, w_ref, cos_ref, s1_ref, s2_ref,
                xr_ref, yr_ref, q_ref, qrot_ref, kvc_ref, ks_ref, kw_ref, kvct_ref, kvst_ref, kvwt_ref, g_ref, qt_ref):
    x = x_ref[0]
    h = _rms(x, ln_ref[...]) * (1.0 + scale_ref[0]) + shift_ref[0]
    z = _bdot(h, w_ref[...])
    xr_ref[0] = z[:, 0:RNN_WIDTH]
    yr_ref[0] = z[:, RNN_WIDTH:2 * RNN_WIDTH]
    q = z[:, 2 * RNN_WIDTH:2 * RNN_WIDTH + ATTN_WIDTH]
    q_ref[0] = q
    cos, s1, s2 = cos_ref[...], s1_ref[...], s2_ref[...]
    rep = ATTN_WIDTH // LANES
    qrot = _rope(q, jnp.concatenate([cos] * rep, -1), jnp.concatenate([s1] * rep, -1),
                 jnp.concatenate([s2] * rep, -1))
    qrot_ref[0] = qrot
    qt_ref[0] = (qrot * (ATTN_SCALE * LOG2_E)).T.astype(BF16)
    c0 = 2 * RNN_WIDTH + ATTN_WIDTH
    kvc = z[:, c0:c0 + KV_ROW]
    kvc_ref[0] = kvc
    kvct_ref[0] = kvc.T
    tm = x.shape[0]
    for i, kref, ref in ((1, ks_ref, kvst_ref), (2, kw_ref, kvwt_ref)):
        kv = z[:, c0 + i * KV_ROW:c0 + (i + 1) * KV_ROW]
        keys = _rope(kv[:, :KV_WIDTH], cos, s1, s2)
        if i == 1:
            blk = (pl.program_id(1) * tm + lax.broadcasted_iota(jnp.int32, (tm, HEAD_DIM), 0)) // CMP_BLOCK
            onehot = jnp.where(lax.broadcasted_iota(jnp.int32, (tm, HEAD_DIM), 1) == blk, 1.0, 0.0)
            kref[0] = jnp.concatenate([keys[:, :HEAD_DIM], onehot, keys[:, HEAD_DIM:], onehot], axis=-1)
        else:
            kref[0] = keys
        ref[0, 0:KV_WIDTH, :] = keys.T
        ref[0, KV_WIDTH:KV_ROW, :] = kv[:, KV_WIDTH:].T
    g_ref[0] = jax.nn.sigmoid(z[:, GATE_COL:GATE_COL + LANES])


def _pre(x3, shift, scale, ln, w_in_bf, cos, s1, s2, tm):
    bx, sx, _ = x3.shape
    sm = shift.shape[1]
    if sm == 1:
        mod_spec = pl.BlockSpec((1, 1, D_MODEL), lambda b, t: (b, 0, 0))
    else:
        mod_spec = pl.BlockSpec((1, tm, D_MODEL), lambda b, t: (b, t, 0))
    tab = pl.BlockSpec((tm, LANES), lambda b, t: (t, 0))
    row = lambda w: (pl.BlockSpec((1, tm, w), lambda b, t: (b, t, 0)), _sds((bx, sx, w)))
    col = (pl.BlockSpec((1, KV_ROW, tm), lambda b, t: (b, 0, t)), _sds((bx, KV_ROW, sx)))
    qt = (pl.BlockSpec((1, ATTN_WIDTH, tm), lambda b, t: (b, 0, t)), _sds((bx, ATTN_WIDTH, sx), BF16))
    outs = [row(RNN_WIDTH), row(RNN_WIDTH), row(ATTN_WIDTH), row(ATTN_WIDTH), row(KV_ROW), row(2 * KV_WIDTH),
            row(KV_WIDTH), col, col, col, row(LANES), qt]
    return pl.pallas_call(
        _pre_kernel, grid=(bx, sx // tm),
        in_specs=[pl.BlockSpec((1, tm, D_MODEL), lambda b, t: (b, t, 0)), mod_spec, mod_spec,
                  pl.BlockSpec((1, D_MODEL), lambda b, t: (0, 0)),
                  pl.BlockSpec((D_MODEL, IN_COLS_PAD), lambda b, t: (0, 0)), tab, tab, tab],
        out_specs=[o[0] for o in outs], out_shape=[o[1] for o in outs],
        compiler_params=_cp(("arbitrary", "arbitrary")), name="pre")(x3, shift, scale, ln, w_in_bf, cos, s1, s2)


def _lru_gates(xc, wax_ref, bax_ref, lam_ref):
    ra = _bdot(xc, wax_ref[...]) + bax_ref[...]
    r = jax.nn.sigmoid(ra[:, :RNN_WIDTH])
    i = jax.nn.sigmoid(ra[:, RNN_WIDTH:])
    nl = -lam_ref[...]
    softplus = jnp.maximum(nl, 0.0) + jnp.log(1.0 + jnp.exp(-jnp.abs(nl)))
    a = jnp.exp(-LRU_C * r * softplus)
    return a, jnp.sqrt(1.0 - a * a) * (i * xc)


def _scan_rows(a, b):
    t, c = a.shape
    row = lax.broadcasted_iota(jnp.int32, a.shape, 0)
    k = 1
    while k < t:
        if k < 8:
            a_s = jnp.where(row >= k, pltpu.roll(a, k, 0), 1.0)
            b_s = jnp.where(row >= k, pltpu.roll(b, k, 0), 0.0)
        else:
            a_s = jnp.concatenate([jnp.ones((k, c), F32), a[:t - k]], axis=0)
            b_s = jnp.concatenate([jnp.zeros((k, c), F32), b[:t - k]], axis=0)
        b = a * b_s + b
        a = a * a_s
        k *= 2
    return a, b


def _rglru_kernel(xr_ref, yr_ref, cw_ref, cb_ref, wax_ref, bax_ref, lam_ref, o_ref, hl_ref, xbuf, hc, *, t):
    @pl.when(pl.program_id(1) == 0)
    def _():
        xbuf[0:8, :] = jnp.zeros((8, RNN_WIDTH), F32)
        hc[...] = jnp.zeros_like(hc)

    xbuf[8:8 + t, :] = xr_ref[0]
    xc = cb_ref[...]
    for k in range(CONV_W):
        xc = xc + xbuf[5 + k:5 + k + t, :] * cw_ref[k:k + 1, :]
    xbuf[0:8, :] = xbuf[t:t + 8, :]
    a, b = _lru_gates(xc, wax_ref, bax_ref, lam_ref)
    ac, bc = _scan_rows(a, b)
    h = ac * hc[...] + bc
    hc[...] = h[t - 1:t, :]
    hl_ref[0] = h[t - 1:t, :]
    o_ref[0] = h * jax.nn.gelu(yr_ref[0])


def _rglru_prompt(xr, yr, cw, cb, wax, bax, lam, t=256):
    b, s, _ = xr.shape
    tile = pl.BlockSpec((1, t, RNN_WIDTH), lambda i, j: (i, j, 0))
    full = lambda shp: pl.BlockSpec(shp, lambda i, j: (0,) * len(shp))
    return pl.pallas_call(
        functools.partial(_rglru_kernel, t=t), grid=(b, s // t),
        in_specs=[tile, tile, full((CONV_W, RNN_WIDTH)), full((1, RNN_WIDTH)),
                  full((RNN_WIDTH, 2 * RNN_WIDTH)), full((1, 2 * RNN_WIDTH)), full((1, RNN_WIDTH))],
        out_specs=[tile, pl.BlockSpec((1, 1, RNN_WIDTH), lambda i, j: (i, 0, 0))],
        out_shape=[_sds((b, s, RNN_WIDTH)), _sds((b, 1, RNN_WIDTH))],
        scratch_shapes=[pltpu.VMEM((t + 8, RNN_WIDTH), F32), pltpu.VMEM((1, RNN_WIDTH), F32)],
        compiler_params=_cp(("arbitrary", "arbitrary")), name="rglru_prompt")(xr, yr, cw, cb, wax, bax, lam)


def _rglru_sample_kernel(xr_ref, yr_ref, cp_ref, h0_ref, cw_ref, cb_ref, wax_ref, bax_ref, lam_ref,
                         o_ref, hl_ref, *, s, nb):
    xpad = [cp_ref[i] for i in range(CONV_W - 1)] + [xr_ref[i] for i in range(s)]
    xcs = []
    for j in range(s):
        xc = cb_ref[...]
        for k in range(CONV_W):
            xc = xc + xpad[j + k] * cw_ref[k:k + 1, :]
        xcs.append(xc)
    a, b = _lru_gates(jnp.concatenate(xcs, axis=0), wax_ref, bax_ref, lam_ref)
    h = h0_ref[...]
    for j in range(s):
        h = a[j * nb:(j + 1) * nb] * h + b[j * nb:(j + 1) * nb]
        o_ref[j] = h * jax.nn.gelu(yr_ref[j])
    hl_ref[...] = h


def _rglru_sample(xr_t, yr_t, cp_t, h0, cw, cb, wax, bax, lam):
    s, nb, _ = xr_t.shape
    return pl.pallas_call(
        functools.partial(_rglru_sample_kernel, s=s, nb=nb),
        out_shape=[_sds((s, nb, RNN_WIDTH)), _sds((nb, RNN_WIDTH))],
        compiler_params=pltpu.CompilerParams(vmem_limit_bytes=VMEM_LIMIT), name="rglru_sample")(
            xr_t, yr_t, cp_t, h0, cw, cb, wax, bax, lam)


def _build_w1bd(w1_ref, w1bd):
    w1bd[...] = jnp.zeros_like(w1bd)
    for c in range(2):
        for k in range(N_KV_HEADS):
            r0 = c * KV_WIDTH + k * HEAD_DIM
            f0 = (c * N_KV_HEADS + k) * CMP_HIDDEN
            w1bd[:, r0:r0 + HEAD_DIM, f0:f0 + CMP_HIDDEN] = w1_ref[c]


def _compress_blocks(get_rows, m, pe_ref, w1bd, w2_ref, acc):
    acc[...] = jnp.zeros_like(acc)

    def body(l, carry):
        z = get_rows(l) + pe_ref[pl.ds(l, 1), :]
        acc[...] += _bdot(z, w1bd[l])
        return carry

    lax.fori_loop(0, CMP_BLOCK, body, 0)
    return _bdot(jax.nn.gelu(acc[...]), w2_ref[...])


def _compress_prompt_kernel(kv_ref, pe_ref, w1_ref, w2_ref, o_ref, w1bd, acc, *, nblk):
    @pl.when(pl.program_id(0) == 0)
    def _():
        _build_w1bd(w1_ref, w1bd)

    o_ref[0] = _compress_blocks(lambda l: kv_ref[0, :, l, :], nblk, pe_ref, w1bd, w2_ref, acc)


def _compress_prompt(kvc, pe_row, w1_bf, w2bd_bf):
    b, s, _ = kvc.shape
    nblk = s // CMP_BLOCK
    kv4 = kvc.reshape(b, nblk, CMP_BLOCK, KV_ROW)
    return pl.pallas_call(
        functools.partial(_compress_prompt_kernel, nblk=nblk), grid=(b,),
        in_specs=[pl.BlockSpec((1, nblk, CMP_BLOCK, KV_ROW), lambda i: (i, 0, 0, 0)),
                  pl.BlockSpec((CMP_BLOCK, KV_ROW), lambda i: (0, 0)),
                  pl.BlockSpec((2, CMP_BLOCK, HEAD_DIM, CMP_HIDDEN), lambda i: (0, 0, 0, 0)),
                  pl.BlockSpec((4 * CMP_HIDDEN, KV_ROW), lambda i: (0, 0))],
        out_specs=pl.BlockSpec((1, nblk, KV_ROW), lambda i: (i, 0, 0)),
        out_shape=_sds((b, nblk, KV_ROW)),
        scratch_shapes=[pltpu.VMEM((CMP_BLOCK, KV_ROW, 4 * CMP_HIDDEN), BF16),
                        pltpu.VMEM((nblk, 4 * CMP_HIDDEN), F32)],
        compiler_params=_cp(("arbitrary",)), name="compress_prompt")(kv4, pe_row, w1_bf, w2bd_bf)


def _page_copies(pt_ref, pool_ref, dst, sem, b, slot, n_pages):
    return [pltpu.make_async_copy(pool_ref.at[pt_ref[b, p]], dst(slot, p), sem.at[slot]) for p in range(n_pages)]


def _paged_prefetch(pt_ref, pool_ref, dst, sem, n_pages):
    b = pl.program_id(0)
    nb = pl.num_programs(0)
    slot = lax.rem(b, 2)

    @pl.when(b == 0)
    def _():
        for cp in _page_copies(pt_ref, pool_ref, dst, sem, 0, 0, n_pages):
            cp.start()

    @pl.when(b + 1 < nb)
    def _():
        for cp in _page_copies(pt_ref, pool_ref, dst, sem, b + 1, 1 - slot, n_pages):
            cp.start()

    for cp in _page_copies(pt_ref, pool_ref, dst, sem, b, slot, n_pages):
        cp.wait()
    return slot


def _compress_sample_kernel(pt_ref, pool_ref, new_ref, pe_ref, w1_ref, w2_ref, o_ref,
                            stage, sem, rows, w1s, acc, *, n_pages, m, nbp, n_rows):
    nblk = n_pages * BLOCKS_PER_PAGE
    n_iter = CMP_BLOCK // LSTEP
    assert n_pages % n_iter == 0
    pages_per_iter = n_pages // n_iter
    b = pl.program_id(0)
    cur = lax.rem(b, 2)
    nxt = 1 - cur

    def copies(row, slot):
        return _page_copies(pt_ref, pool_ref, lambda s, p: stage.at[s, p], sem, row, slot, n_pages)

    def half_to_rows(slot, p, c):
        r0 = pl.multiple_of(p * (BLOCKS_PER_PAGE * ROW_PITCH), 8)
        t = stage[slot, p, c * KV_WIDTH:(c + 1) * KV_WIDTH, :].T
        for j in range(BLOCKS_PER_PAGE):
            rows[slot, c, pl.ds(r0 + j * ROW_PITCH, CMP_BLOCK), :] = t[j * CMP_BLOCK:(j + 1) * CMP_BLOCK]

    def to_rows(slot, p):
        for c in range(2):
            half_to_rows(slot, p, c)

    @pl.when(b == 0)
    def _():
        w1s[...] = jnp.zeros_like(w1s)
        for c in range(2):
            for k in range(N_KV_HEADS):
                w1s[c, :, k * HEAD_DIM:(k + 1) * HEAD_DIM, k * CMP_HIDDEN:(k + 1) * CMP_HIDDEN] = w1_ref[c]
            for s in range(2):
                rows[s, c, nblk * ROW_PITCH:m * ROW_PITCH, :] = jnp.zeros(((m - nblk) * ROW_PITCH, KV_WIDTH), F32)
        for s in range(min(2, n_rows)):
            for cp in copies(s, s):
                cp.start()
        for cp in copies(0, 0):
            cp.wait()

        def first(p, carry):
            to_rows(0, p)
            return carry

        lax.fori_loop(0, n_pages, first, 0, unroll=4)

    @pl.when(b + 1 < n_rows)
    def _():
        for cp in copies(b + 1, nxt):
            cp.wait()

    @pl.when(b + 2 < n_rows)
    def _():
        for cp in copies(b + 2, cur):
            cp.start()

    for c in range(2):
        rows[cur, c, nblk * ROW_PITCH:nblk * ROW_PITCH + 8, :] = new_ref[0, :, c * KV_WIDTH:(c + 1) * KV_WIDTH]
    acc[...] = jnp.zeros_like(acc)

    def body(i, carry):
        l0 = pl.multiple_of(i * LSTEP, LSTEP)
        pe = pe_ref[pl.ds(l0, LSTEP), :]
        halves = [(pp, c) for pp in range(pages_per_iter) for c in range(2)] if n_rows > 1 else []
        per_gather = -(-len(halves) // LSTEP)
        prods = []
        for c in range(2):
            parts = []
            for j in range(LSTEP):
                parts.append(rows[cur, c, pl.ds(l0 + j, m, stride=ROW_PITCH), :]
                             + pe[j:j + 1, c * KV_WIDTH:(c + 1) * KV_WIDTH])
                if c == 1:
                    for pp, hc in halves[:per_gather]:
                        half_to_rows(nxt, i * pages_per_iter + pp, hc)
                    halves = halves[per_gather:]
            w = w1s[c, pl.ds(l0, LSTEP)].reshape(LSTEP * KV_WIDTH, N_KV_HEADS * CMP_HIDDEN)
            prods.append(_bdot(jnp.concatenate(parts, axis=1), w))
        for c in range(2):
            acc[c] += prods[c]
        return carry

    lax.fori_loop(0, n_iter, body, 0)
    hw = N_KV_HEADS * CMP_HIDDEN
    for c in range(2):
        o_ref[0, 0:m, c * KV_WIDTH:(c + 1) * KV_WIDTH] = _bdot(
            jax.nn.gelu(acc[c]), w2_ref[c * hw:(c + 1) * hw, c * KV_WIDTH:(c + 1) * KV_WIDTH])
    o_ref[0, m:nbp, :] = jnp.zeros((nbp - m, KV_ROW), F32)


def _compress_sample(page_table, pool_t, kvc_new8, pe_row, w1_bf, w2bd_bf, nbp):
    b, n_pages = page_table.shape
    m = n_pages * BLOCKS_PER_PAGE + 8
    grid_spec = pltpu.PrefetchScalarGridSpec(
        num_scalar_prefetch=1, grid=(b,),
        in_specs=[pl.BlockSpec(memory_space=pl.ANY),
                  pl.BlockSpec((1, 8, KV_ROW), lambda i, pt: (i, 0, 0)),
                  pl.BlockSpec((CMP_BLOCK, KV_ROW), lambda i, pt: (0, 0)),
                  pl.BlockSpec((2, CMP_BLOCK, HEAD_DIM, CMP_HIDDEN), lambda i, pt: (0, 0, 0, 0)),
                  pl.BlockSpec((4 * CMP_HIDDEN, KV_ROW), lambda i, pt: (0, 0))],
        out_specs=pl.BlockSpec((1, nbp, KV_ROW), lambda i, pt: (i, 0, 0)),
        scratch_shapes=[pltpu.VMEM((2, n_pages, KV_ROW, PAGE_SIZE), F32), pltpu.SemaphoreType.DMA((2,)),
                        pltpu.VMEM((2, 2, m * ROW_PITCH, KV_WIDTH), F32),
                        pltpu.VMEM((2, CMP_BLOCK, KV_WIDTH, N_KV_HEADS * CMP_HIDDEN), BF16),
                        pltpu.VMEM((2, m, N_KV_HEADS * CMP_HIDDEN), F32)])
    return pl.pallas_call(
        functools.partial(_compress_sample_kernel, n_pages=n_pages, m=m, nbp=nbp, n_rows=b),
        grid_spec=grid_spec, out_shape=_sds((b, nbp, KV_ROW)),
        compiler_params=_cp(("arbitrary",)), name="compress_sample")(page_table, pool_t, kvc_new8, pe_row, w1_bf, w2bd_bf)


def _head_lanes(q, h, dst, scale):
    src = q[:, (h // 2) * LANES:(h // 2 + 1) * LANES]
    if (h % 2) != dst:
        src = pltpu.roll(src, HEAD_DIM, 1)
    lane = lax.broadcasted_iota(jnp.int32, src.shape, 1)
    keep = (lane >= dst * HEAD_DIM) & (lane < (dst + 1) * HEAD_DIM)
    return jnp.where(keep, src * scale, 0.0)


def _arrange_q(q, h, scale):
    return _head_lanes(q, h, h // GROUP, scale)


def _gather_heads(o_list):
    parts = []
    for h, o in enumerate(o_list):
        kvh = h // GROUP
        parts.append(o[:, kvh * HEAD_DIM:(kvh + 1) * HEAD_DIM])
    return jnp.concatenate(parts, axis=-1)


def _select_mask_t(imp, qpos, nb):
    j = lax.broadcasted_iota(jnp.int32, imp.shape, 0)
    cur = qpos // CMP_BLOCK
    forced = (j == 0) | (j == cur) | (j == cur - 1)
    score = jnp.where(j > cur, -FORCE_SCORE, jnp.where(forced, FORCE_SCORE, imp))
    if imp.shape[0] > nb:
        score = jnp.where(j < nb, score, -3e38)
    sub = 8
    ngrp = imp.shape[0] // sub
    groups = [score[r * sub:(r + 1) * sub] for r in range(ngrp)]
    jsub = lax.broadcasted_iota(jnp.int32, groups[0].shape, 0)
    ranks = [jnp.zeros(groups[0].shape, F32) for _ in range(ngrp)]
    for i in range(nb):
        row = score[i:i + 1]
        for r in range(ngrp):
            if r * sub > i:
                hit = row >= groups[r]
            elif (r + 1) * sub - 1 <= i:
                hit = row > groups[r]
            else:
                hit = ((jsub + r * sub > i) & (row >= groups[r])) | (row > groups[r])
            ranks[r] = ranks[r] + jnp.where(hit, 1.0, 0.0)
    rank = jnp.concatenate(ranks, axis=0)
    return jnp.where((rank < float(min(N_SEL, nb))) & (j <= cur) & (j < nb), 1.0, 0.0)


def _cmp_kernel_t(q_ref, ckv_ref, oc_ref, selt_ref, *, tq, nb, nbp, pos0):
    qpos = pos0 + pl.program_id(1) * tq + lax.broadcasted_iota(jnp.int32, (nb, tq), 1)
    j = lax.broadcasted_iota(jnp.int32, (nb, tq), 0)
    m = (j + 1) * CMP_BLOCK - 1 <= qpos
    q = q_ref[0]
    ck = ckv_ref[0, :, 0:KV_WIDTH]
    cv = ckv_ref[0, :, KV_WIDTH:KV_ROW].astype(BF16)
    m4 = jnp.concatenate([m] * GROUP, axis=1)
    outs = []
    for kvh in range(N_KV_HEADS):
        qa = jnp.concatenate([_arrange_q(q, kvh * GROUP + g, ATTN_SCALE) for g in range(GROUP)], axis=0)
        s = jnp.where(m4, _dot_nt(ck, qa, precision=lax.Precision.HIGHEST), NEG_INF)
        e = jnp.exp(s - jnp.max(s, axis=0, keepdims=True))
        p = jnp.where(m4, e / jnp.sum(e, axis=0, keepdims=True), 0.0)
        imp = p[:, 0:tq]
        for g in range(1, GROUP):
            imp = imp + p[:, g * tq:(g + 1) * tq]
        o = lax.dot_general(p.astype(BF16), cv, (((0,), (0,)), ((), ())), preferred_element_type=F32)
        outs += [o[g * tq:(g + 1) * tq] for g in range(GROUP)]
        selt_ref[0, kvh, 0:nb, :] = _select_mask_t(imp, qpos, nb)
        selt_ref[0, kvh, nb:nbp, :] = jnp.zeros((nbp - nb, tq), F32)
    oc_ref[0] = _gather_heads(outs)


def _cmp_select_t(q, ckv, nbp, pos0, tq):
    b, sq, _ = q.shape
    nb = ckv.shape[1]
    return pl.pallas_call(
        functools.partial(_cmp_kernel_t, tq=tq, nb=nb, nbp=nbp, pos0=pos0), grid=(b, sq // tq),
        in_specs=[pl.BlockSpec((1, tq, ATTN_WIDTH), lambda i, t: (i, t, 0)),
                  pl.BlockSpec((1, nb, KV_ROW), lambda i, t: (i, 0, 0))],
        out_specs=[pl.BlockSpec((1, tq, ATTN_WIDTH), lambda i, t: (i, t, 0)),
                   pl.BlockSpec((1, N_KV_HEADS, nbp, tq), lambda i, t: (i, 0, 0, t))],
        out_shape=[_sds((b, sq, ATTN_WIDTH)), _sds((b, N_KV_HEADS, nbp, sq))],
        compiler_params=_cp(("arbitrary", "arbitrary")), name="cmp_select_t")(q, ckv)


def _cmp_sample_kernel(q_ref, ckv_ref, oc_ref, selt_ref, *, nb, nbr, pos0):
    nq = 8
    lanes = GROUP * nq
    j = lax.broadcasted_iota(jnp.int32, (nbr, lanes), 0)
    qpos = pos0 + lax.rem(lax.broadcasted_iota(jnp.int32, (nbr, lanes), 1), nq)
    m = ((j + 1) * CMP_BLOCK - 1 <= qpos) & (j < nb)
    imps = [[] for _ in range(N_KV_HEADS)]
    for i in range(q_ref.shape[0]):
        q = q_ref[i]
        ck = ckv_ref[i, 0:nbr, 0:KV_WIDTH]
        cv = ckv_ref[i, 0:nbr, KV_WIDTH:KV_ROW].astype(BF16)
        outs = []
        for kvh in range(N_KV_HEADS):
            qa = jnp.concatenate([_arrange_q(q, kvh * GROUP + g, ATTN_SCALE) for g in range(GROUP)], axis=0)
            s = jnp.where(m, _dot_nt(ck, qa, precision=lax.Precision.HIGHEST), NEG_INF)
            e = jnp.exp(s - jnp.max(s, axis=0, keepdims=True))
            p = jnp.where(m, e / jnp.sum(e, axis=0, keepdims=True), 0.0)
            imp = p[:, 0:nq]
            for g in range(1, GROUP):
                imp = imp + p[:, g * nq:(g + 1) * nq]
            imps[kvh].append(imp)
            o = lax.dot_general(p.astype(BF16), cv, (((0,), (0,)), ((), ())), preferred_element_type=F32)
            outs += [o[g * nq:(g + 1) * nq] for g in range(GROUP)]
        oc_ref[i] = _gather_heads(outs)
    qpos_all = pos0 + lax.rem(lax.broadcasted_iota(jnp.int32, (nbr, LANES), 1), nq)
    for kvh in range(N_KV_HEADS):
        selt_ref[0, kvh] = _select_mask_t(jnp.concatenate(imps[kvh], axis=1), qpos_all, nb)


def _cmp_select_sample(q8, ckv_p, nb, pos0):
    b = q8.shape[0]
    nbp = ckv_p.shape[1]
    nbr = 8 * (-(-nb // 8))
    rows = SAMPLE_ROWS
    assert b % rows == 0
    o_c, sel_t = pl.pallas_call(
        functools.partial(_cmp_sample_kernel, nb=nb, nbr=nbr, pos0=pos0), grid=(b // rows,),
        in_specs=[pl.BlockSpec((rows, 8, ATTN_WIDTH), lambda i: (i, 0, 0)),
                  pl.BlockSpec((rows, nbp, KV_ROW), lambda i: (i, 0, 0))],
        out_specs=[pl.BlockSpec((rows, 8, ATTN_WIDTH), lambda i: (i, 0, 0)),
                   pl.BlockSpec((1, N_KV_HEADS, nbr, LANES), lambda i: (i, 0, 0, 0))],
        out_shape=[_sds((b, 8, ATTN_WIDTH)), _sds((b // rows, N_KV_HEADS, nbr, LANES))],
        compiler_params=_cp(("arbitrary",)), name="cmp_select_sample")(q8, ckv_p)
    sel = jnp.transpose(sel_t.reshape(b // rows, N_KV_HEADS, nbr, rows, 8), (0, 3, 1, 4, 2))
    return o_c, jnp.pad(sel.reshape(b, N_KV_HEADS, 8, nbr), ((0, 0), (0, 0), (0, 0), (0, nbp - nbr)))


def _flash_kernel(tab_ref, q_ref, k_ref, vt_ref, *rest, tq, tk, mode):
    if mode == "sel":
        selt_ref, o_ref, qt, m_s, l_s, acc = rest
    else:
        o_ref, qt, m_s, l_s, acc = rest
    step_id = pl.program_id(1)
    qi = tab_ref[0, step_id]
    ki = tab_ref[1, step_id]
    flags = tab_ref[2, step_id]

    @pl.when((flags & 1) != 0)
    def _():
        zeros = jnp.zeros((HEAD_DIM, tq), BF16)
        for h in range(N_HEADS):
            kvh = h // GROUP
            qh = q_ref[0, h * HEAD_DIM:(h + 1) * HEAD_DIM, :]
            if mode == "sel":
                bias = ((selt_ref[0, kvh, 0:HEAD_DIM, :] - 1.0) * (-NEG_INF)).astype(BF16)
                cols = jnp.concatenate([qh, bias], axis=0)
            else:
                cols = jnp.concatenate([qh, zeros] if kvh == 0 else [zeros, qh], axis=0)
            qt[:, h * tq:(h + 1) * tq] = cols
        m_s[...] = jnp.full_like(m_s, M_INIT)
        l_s[...] = jnp.zeros_like(l_s)
        acc[...] = jnp.zeros_like(acc)

    def step(masked):
        vt = vt_ref[0].astype(BF16)
        if masked:
            d = (lax.broadcasted_iota(jnp.int32, (tk, tq), 0) - lax.broadcasted_iota(jnp.int32, (tk, tq), 1)
                 + (ki * tk - qi * tq))
            valid = (d <= 0) if mode == "sel" else (d <= 0) & (d > -WINDOW)
        if mode == "sel":
            ks = [k_ref[0, :, kvh * LANES:(kvh + 1) * LANES].astype(BF16) for kvh in range(N_KV_HEADS)]
        else:
            ks = [k_ref[0].astype(BF16)] * N_KV_HEADS

        def scores(h):
            return jnp.dot(ks[h // GROUP], qt[:, h * tq:(h + 1) * tq], preferred_element_type=F32)

        def accumulate(h, alpha, pv):
            rows = slice(h * LANES, (h + 1) * LANES)
            acc[rows] = alpha * acc[rows] + pv

        look = SCORE_LOOKAHEAD[mode]
        ahead = [scores(h) for h in range(look)]
        pending = None
        for h in range(N_HEADS):
            s = ahead.pop(0)
            if h + look < N_HEADS:
                ahead.append(scores(h + look))
            if masked:
                s = jnp.where(valid, s, NEG_INF)
            m_old = m_s[h, 0:1, :]
            m_new = jnp.maximum(m_old, jnp.max(s, axis=0, keepdims=True))
            alpha = jnp.exp2(m_old - m_new)
            p = jnp.exp2(s - m_new)
            l_s[h] = jnp.broadcast_to(alpha * l_s[h, 0:1, :] + jnp.sum(p, axis=0, keepdims=True), (8, tq))
            m_s[h] = jnp.broadcast_to(m_new, (8, tq))
            pv = jnp.dot(vt, p.astype(BF16), preferred_element_type=F32)
            if pending is not None:
                accumulate(*pending)
            pending = (h, alpha, pv)
        accumulate(*pending)

    @pl.when((flags & 4) != 0)
    def _():
        step(True)

    @pl.when((flags & 4) == 0)
    def _():
        step(False)

    @pl.when((flags & 2) != 0)
    def _():
        for h in range(N_HEADS):
            r0 = h * LANES + (h // GROUP) * HEAD_DIM
            o_ref[0, h * HEAD_DIM:(h + 1) * HEAD_DIM, :] = acc[r0:r0 + HEAD_DIM] / l_s[h, 0:1, :]


def _flash_schedule(s, tq, tk, mode):
    steps = []
    for qi in range(s // tq):
        q_lo, q_hi = qi * tq, qi * tq + tq - 1
        k_lo = 0 if mode == "sel" else max(q_lo - (WINDOW - 1), 0)
        tiles = list(range(k_lo // tk, q_hi // tk + 1))
        for ki in tiles:
            causal_edge = ki * tk + tk - 1 > q_lo
            window_edge = mode == "win" and q_hi - ki * tk >= WINDOW
            flags = (1 if ki == tiles[0] else 0) | (2 if ki == tiles[-1] else 0) | (4 if causal_edge or window_edge else 0)
            steps.append((qi, ki, flags))
    return jnp.asarray(np.array(steps, np.int32).T)


def _flash_prompt(q_t, k_rows, kv_t, sel_t, mode, tq, tk):
    b, _, s = q_t.shape
    if mode == "sel":
        assert s // CMP_BLOCK <= HEAD_DIM
    tab = _flash_schedule(s, tq, tk, mode)
    kw = k_rows.shape[-1]
    in_specs = [pl.BlockSpec((1, ATTN_WIDTH, tq), lambda i, p, t: (i, 0, t[0, p])),
                pl.BlockSpec((1, tk, kw), lambda i, p, t: (i, t[1, p], 0)),
                pl.BlockSpec((1, KV_WIDTH, tk), lambda i, p, t: (i, 1, t[1, p]))]
    args = [q_t, k_rows, kv_t]
    if mode == "sel":
        nbp = sel_t.shape[2]
        in_specs += [pl.BlockSpec((1, N_KV_HEADS, nbp, tq), lambda i, p, t: (i, 0, 0, t[0, p]))]
        args += [sel_t]
    grid_spec = pltpu.PrefetchScalarGridSpec(
        num_scalar_prefetch=1, grid=(b, tab.shape[1]), in_specs=in_specs,
        out_specs=pl.BlockSpec((1, ATTN_WIDTH, tq), lambda i, p, t: (i, 0, t[0, p])),
        scratch_shapes=[pltpu.VMEM((LANES, N_HEADS * tq), BF16), pltpu.VMEM((N_HEADS, 8, tq), F32),
                        pltpu.VMEM((N_HEADS, 8, tq), F32), pltpu.VMEM((N_HEADS * LANES, tq), F32)])
    return pl.pallas_call(
        functools.partial(_flash_kernel, tq=tq, tk=tk, mode=mode), grid_spec=grid_spec,
        out_shape=_sds((b, ATTN_WIDTH, s)),
        compiler_params=_cp(("arbitrary", "arbitrary")), name="flash_" + mode)(tab, *args)


def _sample_q_rows(q8):
    return jnp.concatenate([_arrange_q(q8, h, ATTN_SCALE) for h in range(N_HEADS)], axis=0).astype(BF16)


def _sample_out(o_all):
    return _gather_heads([o_all[h * 8:(h + 1) * 8] for h in range(N_HEADS)])


def _two_part_attention(s_past, s_new, vt_past, vt_new):
    mx = jnp.maximum(jnp.max(s_past, axis=-1, keepdims=True), jnp.max(s_new, axis=-1, keepdims=True))
    p_past = jnp.exp(s_past - mx)
    p_new = jnp.exp(s_new - mx)
    den = jnp.sum(p_past, axis=-1, keepdims=True) + jnp.sum(p_new, axis=-1, keepdims=True)
    return (_dot_nt(p_past.astype(BF16), vt_past) + _dot_nt(p_new.astype(BF16), vt_new)) / den


def _sel_sample_kernel(pt_ref, pool_ref, q_ref, newt_ref, sel_ref, e_ref, o_ref, buf, sem, *, n_pages, pos0, n_new):
    slot = _paged_prefetch(pt_ref, pool_ref, lambda s, p: buf.at[s, :, pl.ds(p * PAGE_SIZE, PAGE_SIZE)], sem, n_pages)
    nkeys = n_pages * PAGE_SIZE
    nblk = nkeys // CMP_BLOCK
    nrow = 8 * N_HEADS
    qall = _sample_q_rows(q_ref[0])
    selrows = jnp.concatenate([sel_ref[0, h // GROUP] for h in range(N_HEADS)], axis=0)
    qpos = pos0 + lax.rem(lax.broadcasted_iota(jnp.int32, (nrow, 1), 0), 8)
    assert nkeys <= pos0
    bias = ((selrows - 1.0) * (-NEG_INF)).astype(BF16)
    s_past = (jnp.dot(qall, buf[slot, 0:KV_WIDTH, :].astype(BF16), preferred_element_type=F32)
              + jnp.dot(bias, e_ref[...], preferred_element_type=F32))
    newt = newt_ref[0].astype(BF16)
    inew = lax.broadcasted_iota(jnp.int32, (nrow, LANES), 1) - (LANES - n_new)
    sel_new = selrows[:, nblk:nblk + 1] > 0.5
    s_new = jnp.dot(qall, newt[0:KV_WIDTH], preferred_element_type=F32)
    s_new = jnp.where(sel_new & (inew >= 0) & (pos0 + inew <= qpos), s_new, NEG_INF)
    o = _two_part_attention(s_past, s_new, buf[slot, KV_WIDTH:KV_ROW, :].astype(BF16), newt[KV_WIDTH:KV_ROW])
    o_ref[0] = _sample_out(o)


def _sel_sample(page_table, pool_t, qrot8, kvs_newt, sel, emat, pos0, n_new):
    b, n_pages = page_table.shape
    nbp = sel.shape[-1]
    nkeys = n_pages * PAGE_SIZE
    grid_spec = pltpu.PrefetchScalarGridSpec(
        num_scalar_prefetch=1, grid=(b,),
        in_specs=[pl.BlockSpec(memory_space=pl.ANY),
                  pl.BlockSpec((1, 8, ATTN_WIDTH), lambda i, pt: (i, 0, 0)),
                  pl.BlockSpec((1, KV_ROW, LANES), lambda i, pt: (i, 0, 0)),
                  pl.BlockSpec((1, N_KV_HEADS, 8, nbp), lambda i, pt: (i, 0, 0, 0)),
                  pl.BlockSpec((nbp, nkeys), lambda i, pt: (0, 0))],
        out_specs=pl.BlockSpec((1, 8, ATTN_WIDTH), lambda i, pt: (i, 0, 0)),
        scratch_shapes=[pltpu.VMEM((2, KV_ROW, nkeys), F32), pltpu.SemaphoreType.DMA((2,))])
    return pl.pallas_call(
        functools.partial(_sel_sample_kernel, n_pages=n_pages, pos0=pos0, n_new=n_new),
        grid_spec=grid_spec, out_shape=_sds((b, 8, ATTN_WIDTH)),
        compiler_params=_cp(("arbitrary",)), name="sel_sample")(page_table, pool_t, qrot8, kvs_newt, sel, emat)


def _win_sample_kernel(q_ref, buf_ref, newt_ref, o_ref, nw_ref, *, wb, n_new):
    nrow = 8 * N_HEADS
    qall = _sample_q_rows(q_ref[0])
    buf = buf_ref[0]
    newt = newt_ref[0]
    srow = lax.rem(lax.broadcasted_iota(jnp.int32, (nrow, 1), 0), 8)
    i_past = lax.broadcasted_iota(jnp.int32, (nrow, wb), 1)
    dp = wb + srow - i_past
    s_past = jnp.dot(qall, buf[0:KV_WIDTH].astype(BF16), preferred_element_type=F32)
    s_past = jnp.where((dp >= 0) & (dp < WINDOW), s_past, NEG_INF)
    i_new = lax.broadcasted_iota(jnp.int32, (nrow, LANES), 1) - (LANES - n_new)
    dn = srow - i_new
    s_new = jnp.dot(qall, newt[0:KV_WIDTH].astype(BF16), preferred_element_type=F32)
    s_new = jnp.where((i_new >= 0) & (dn >= 0) & (dn < WINDOW), s_new, NEG_INF)
    o = _two_part_attention(s_past, s_new, buf[KV_WIDTH:KV_ROW].astype(BF16), newt[KV_WIDTH:KV_ROW].astype(BF16))
    o_ref[0] = _sample_out(o)
    nw_ref[0] = pltpu.roll(buf, wb - n_new, 1)
    lane = lax.broadcasted_iota(jnp.int32, (KV_ROW, LANES), 1)
    nw_ref[0, :, wb - LANES:wb] = jnp.where(lane >= LANES - n_new, newt,
                                            pltpu.roll(buf[:, wb - LANES:wb], LANES - n_new, 1))


def _win_sample(qrot8, win_t, kvw_newt, n_new):
    b, _, wb = win_t.shape
    return pl.pallas_call(
        functools.partial(_win_sample_kernel, wb=wb, n_new=n_new), grid=(b,),
        in_specs=[pl.BlockSpec((1, 8, ATTN_WIDTH), lambda i: (i, 0, 0)),
                  pl.BlockSpec((1, KV_ROW, wb), lambda i: (i, 0, 0)),
                  pl.BlockSpec((1, KV_ROW, LANES), lambda i: (i, 0, 0))],
        out_specs=[pl.BlockSpec((1, 8, ATTN_WIDTH), lambda i: (i, 0, 0)),
                   pl.BlockSpec((1, KV_ROW, wb), lambda i: (i, 0, 0))],
        out_shape=[_sds((b, 8, ATTN_WIDTH)), _sds((b, KV_ROW, wb))],
        compiler_params=_cp(("arbitrary",)), name="win_sample")(qrot8, win_t, kvw_newt)


def _post_kernel(x_ref, or_ref, oc_ref, os_ref, ow_ref, g_ref, m2_ref, m3_ref, m4_ref, m5_ref,
                 gnr_ref, gna_ref, ln2_ref, fin_ref, eg_ref, wo_ref, wg_ref, wu_ref, wd_ref, y_ref,
                 *, ff_chunks, attn_t):
    g = g_ref[0]
    g_hi = g.astype(BF16)
    g_lo = (g - g_hi.astype(F32)).astype(BF16)
    gx = jnp.dot(jnp.concatenate([g_hi, g_lo], axis=-1), eg_ref[...], preferred_element_type=F32)
    o_s, o_w = (os_ref[0].T, ow_ref[0].T) if attn_t else (os_ref[0], ow_ref[0])
    attn = (gx[:, 0:ATTN_WIDTH] * oc_ref[0] + gx[:, ATTN_WIDTH:2 * ATTN_WIDTH] * o_s
            + gx[:, 2 * ATTN_WIDTH:3 * ATTN_WIDTH] * o_w)
    mix = (_bdot(_rms(or_ref[0], gnr_ref[...]), wo_ref[0:RNN_WIDTH, :])
           + _bdot(_rms(attn, gna_ref[...]), wo_ref[RNN_WIDTH:D_MODEL, :]))
    x1 = x_ref[0] + m2_ref[0] * mix
    h = (_rms(x1, ln2_ref[...]) * (1.0 + m4_ref[0]) + m3_ref[0]).astype(BF16)
    cw = D_FF // ff_chunks
    ff = jnp.zeros(x1.shape, F32)
    for c in range(ff_chunks):
        gate = jnp.dot(h, wg_ref[:, c * cw:(c + 1) * cw], preferred_element_type=F32)
        up = jnp.dot(h, wu_ref[:, c * cw:(c + 1) * cw], preferred_element_type=F32)
        act = gate * jax.nn.sigmoid(gate) * up
        ff = ff + _bdot(act, wd_ref[c * cw:(c + 1) * cw, :])
    y = x1 + m5_ref[0] * ff
    y_ref[0] = _rms(y, fin_ref[...])


def _post(x3, o_r, o_c, o_s, o_w, g, mods, gnr, gna, ln2, fin, egate, wo, wg, wu, wd, tm, attn_t, ff_chunks=2):
    bx, sx, _ = x3.shape
    sm = mods[0].shape[1]
    if sm == 1:
        mod_spec = pl.BlockSpec((1, 1, D_MODEL), lambda b, t: (b, 0, 0))
    else:
        mod_spec = pl.BlockSpec((1, tm, D_MODEL), lambda b, t: (b, t, 0))
    tile = lambda w: pl.BlockSpec((1, tm, w), lambda b, t: (b, t, 0))
    attn_tile = pl.BlockSpec((1, ATTN_WIDTH, tm), lambda b, t: (b, 0, t)) if attn_t else tile(ATTN_WIDTH)

    def const(shape):
        return pl.BlockSpec(shape, lambda b, t: (0,) * len(shape), pipeline_mode=pl.Buffered(1))

    return pl.pallas_call(
        functools.partial(_post_kernel, ff_chunks=ff_chunks, attn_t=attn_t), grid=(bx, sx // tm),
        in_specs=[tile(D_MODEL), tile(RNN_WIDTH), tile(ATTN_WIDTH), attn_tile, attn_tile, tile(LANES),
                  mod_spec, mod_spec, mod_spec, mod_spec,
                  const((1, RNN_WIDTH)), const((1, ATTN_WIDTH)), const((1, D_MODEL)), const((1, D_MODEL)),
                  const((2 * LANES, 3 * ATTN_WIDTH)), const((D_MODEL, D_MODEL)),
                  const((D_MODEL, D_FF)), const((D_MODEL, D_FF)), const((D_FF, D_MODEL))],
        out_specs=tile(D_MODEL), out_shape=_sds((bx, sx, D_MODEL)),
        compiler_params=_cp(("arbitrary", "arbitrary")), name="post")(
            x3, o_r, o_c, o_s, o_w, g, mods[2], mods[3], mods[4], mods[5], gnr, gna, ln2, fin, egate, wo, wg, wu, wd)


def _rope_tables(pos):
    inv = jnp.exp(jnp.arange(ROT_HALF, dtype=F32) * (-math.log(ROPE_THETA) / ROT_HALF))
    ang = pos.astype(F32)[:, None] * inv[None, :]
    cos, sin = jnp.cos(ang), jnp.sin(ang)
    n = pos.shape[0]
    one = jnp.ones((n, HEAD_DIM - ROT_DIM), F32)
    zero = jnp.zeros((n, HEAD_DIM - ROT_DIM), F32)
    z8 = jnp.zeros((n, ROT_HALF), F32)
    c = jnp.concatenate([cos, cos, one], axis=-1)
    s1 = jnp.concatenate([-sin, z8, zero], axis=-1)
    s2 = jnp.concatenate([z8, sin, zero], axis=-1)
    return tuple(jnp.concatenate([t, t], axis=-1) for t in (c, s1, s2))


def _blockdiag(w):
    nb, d, e = w.shape
    return jnp.einsum('nde,nm->ndme', w, jnp.eye(nb, dtype=w.dtype)).reshape(nb * d, nb * e)


def _prep_params(w_in, lru_wa, lru_wx, lru_ba, lru_bx, cmp_pe, cmp_w1, cmp_w2):
    w_in_p = jnp.pad(w_in, ((0, 0), (0, IN_COLS_PAD - w_in.shape[1]))).astype(BF16)
    wax = jnp.concatenate([_blockdiag(lru_wa), _blockdiag(lru_wx)], axis=1).astype(BF16)
    bax = jnp.concatenate([lru_ba, lru_bx])[None, :]
    pe_row = jnp.broadcast_to(jnp.transpose(cmp_pe, (1, 0, 2))[:, :, None, :],
                              (CMP_BLOCK, 2, N_KV_HEADS, HEAD_DIM)).reshape(CMP_BLOCK, KV_ROW)
    w2bd = _blockdiag(jnp.repeat(cmp_w2, N_KV_HEADS, axis=0)).astype(BF16)
    e = np.zeros((LANES, 3 * ATTN_WIDTH), np.float32)
    for br in range(3):
        for h in range(N_HEADS):
            e[br * N_HEADS + h, br * ATTN_WIDTH + h * HEAD_DIM:br * ATTN_WIDTH + (h + 1) * HEAD_DIM] = 1.0
    egate = jnp.asarray(np.concatenate([e, e], axis=0), BF16)
    return w_in_p, wax, bax, pe_row, cmp_w1.astype(BF16), w2bd, egate


def _expand_matrix(nbp, n_tiles, tk):
    blk = (np.arange(n_tiles)[:, None, None] * tk + np.arange(tk)[None, None, :]) // CMP_BLOCK
    return jnp.asarray(blk == np.arange(nbp)[None, :, None], BF16)


def _kv_from_t(kv_t):
    n, _, s = kv_t.shape
    return jnp.transpose(kv_t.reshape(n, 2, N_KV_HEADS, HEAD_DIM, s), (0, 4, 1, 2, 3))


def _kv_to_t(kv):
    n, s = kv.shape[:2]
    return jnp.transpose(kv, (0, 2, 3, 4, 1)).reshape(n, KV_ROW, s)


def _layer_prompt(x, mods, P, tm=512, tq=256):
    b, s, _ = x.shape
    cos, s1, s2 = _rope_tables(jnp.arange(s))
    xr, yr, q, _, kvc, ks, kw, kvc_t, kvs_t, kvw_t, g, q_t = _pre(
        x, mods[0], mods[1], P['ln1'], P['w_in'], cos, s1, s2, tm)
    o_r, h_new = _rglru_prompt(xr, yr, P['conv_w'], P['conv_b'], P['wax'], P['bax'], P['lam'])
    nb = s // CMP_BLOCK
    ckv = _compress_prompt(kvc, P['pe_row'], P['w1'], P['w2bd'])
    nbp = LANES * (-(-nb // LANES))
    o_c, sel_t = _cmp_select_t(q, ckv, nbp, 0, min(256, s))
    o_s_t = _flash_prompt(q_t, ks, kvs_t, sel_t, "sel", tq, min(512, s))
    o_w_t = _flash_prompt(q_t, kw, kvw_t, None, "win", tq, tq)
    y = _post(x, o_r, o_c, o_s_t, o_w_t, g, mods, P['gn_rnn'], P['gn_attn'], P['ln2'], P['final_g'], P['egate'],
              P['w_out'], P['w_gate'], P['w_up'], P['w_down'], tm, True)
    wlen = min(WINDOW, s)
    return y, (_kv_from_t(kvc_t), _kv_from_t(kvs_t), _kv_from_t(kvw_t[:, :, s - wlen:]),
               xr[:, s - (CONV_W - 1):], h_new[:, 0])


def _pad_rows8(a):
    return jnp.pad(a, ((0, 0), (0, 8 - a.shape[1]), (0, 0)))


def _layer_sample(x, mods_tok, cmp_pool, sel_pool, win_buf, conv_buf, h0, page_table, P):
    b, s, _ = x.shape
    n_pages = page_table.shape[1]
    past = n_pages * PAGE_SIZE
    nb = past // CMP_BLOCK + 1
    pos = past + jnp.arange(s)
    cos, s1, s2 = (jnp.tile(t, (b, 1)) for t in _rope_tables(pos))
    n = b * s
    flat = lambda a: a.reshape(1, n, a.shape[-1])
    outs = _pre(flat(x), mods_tok[0], mods_tok[1], P['ln1'], P['w_in'], cos, s1, s2, min(512, n))
    xr, yr, q, qrot, kvc = (o.reshape(b, s, o.shape[-1]) for o in outs[:5])
    kvc_t, kvs_t, kvw_t = (jnp.transpose(o[0].reshape(KV_ROW, b, s), (1, 0, 2)) for o in outs[7:10])
    lane_pad = lambda a: jnp.pad(a, ((0, 0), (0, 0), (LANES - s, 0)))
    g = outs[10]
    tmaj = lambda a: jnp.transpose(a, (1, 0, 2))
    o_r_t, h_new = _rglru_sample(tmaj(xr), tmaj(yr), tmaj(conv_buf), h0, P['conv_w'], P['conv_b'],
                                 P['wax'], P['bax'], P['lam'])
    o_r = tmaj(o_r_t)
    nbp = LANES * (-(-nb // LANES))
    ckv_p = _compress_sample(page_table, _kv_to_t(cmp_pool), _pad_rows8(kvc), P['pe_row'], P['w1'], P['w2bd'], nbp)
    q8, qrot8 = _pad_rows8(q), _pad_rows8(qrot)
    o_c8, sel = _cmp_select_sample(q8, ckv_p, nb, past)
    o_s8 = _sel_sample(page_table, _kv_to_t(sel_pool), qrot8, lane_pad(kvs_t), sel,
                       _expand_matrix(nbp, 1, past)[0], past, s)
    o_w8, new_win_t = _win_sample(qrot8, _kv_to_t(win_buf), lane_pad(kvw_t), s)
    y = _post(flat(x), flat(o_r), flat(o_c8[:, :s]), flat(o_s8[:, :s]), flat(o_w8[:, :s]), g, mods_tok,
              P['gn_rnn'], P['gn_attn'], P['ln2'], P['final_g'], P['egate'],
              P['w_out'], P['w_gate'], P['w_up'], P['w_down'], min(512, n), False)
    return y.reshape(b, s, D_MODEL), (_kv_from_t(kvc_t), _kv_from_t(kvs_t), _kv_from_t(new_win_t),
                                      xr[:, s - (CONV_W - 1):], h_new)


def kernel(x_prompt, x_sample, cache_cmp_kv, cache_sel_kv, state_win_kv, state_conv, state_lru_h, page_table,
           c_prompt, c_sample, ln1_g, ln2_g, w_ada, b_ada, w_in, conv_w, conv_b, lru_wa, lru_ba, lru_wx, lru_bx,
           lru_lambda, cmp_pe, cmp_w1, cmp_w2, gn_rnn, gn_attn, w_out, w_gate, w_up, w_down, final_g):
    depth = w_in.shape[0]
    assert depth == 1
    bp = x_prompt.shape[0]
    bs, ss, _ = x_sample.shape
    l = 0
    w_in_p, wax, bax, pe_row, w1, w2bd, egate = _prep_params(
        w_in[l], lru_wa[l], lru_wx[l], lru_ba[l], lru_bx[l], cmp_pe[l], cmp_w1[l], cmp_w2[l])
    P = {'ln1': ln1_g[l][None], 'ln2': ln2_g[l][None], 'w_in': w_in_p, 'conv_w': conv_w[l], 'conv_b': conv_b[l][None],
         'wax': wax, 'bax': bax, 'lam': lru_lambda[l][None], 'pe_row': pe_row, 'w1': w1, 'w2bd': w2bd,
         'gn_rnn': gn_rnn[l][None], 'gn_attn': gn_attn[l][None], 'egate': egate, 'final_g': final_g[None],
         'w_out': w_out[l].astype(BF16), 'w_gate': w_gate[l].astype(BF16), 'w_up': w_up[l].astype(BF16),
         'w_down': w_down[l].astype(BF16)}
    mods_all = _ada(jnp.concatenate([c_prompt, c_sample], axis=0), w_ada[l].astype(BF16), b_ada[l][None])
    mods_p = [m[:, None, :] for m in jnp.split(mods_all[:bp], 6, axis=-1)]
    mods_s = [jnp.repeat(m, ss, axis=0)[None] for m in jnp.split(mods_all[bp:], 6, axis=-1)]

    yp, st_p = _layer_prompt(x_prompt, mods_p, P)
    ys, st_s = _layer_sample(x_sample, mods_s, cache_cmp_kv[l], cache_sel_kv[l], state_win_kv[l],
                             state_conv[l], state_lru_h[l], page_table, P)

    return (yp, ys, st_p[0][None], st_s[0][None], st_p[1][None], st_s[1][None], st_p[2][None], st_s[2][None],
            st_p[3][None], st_s[3][None], st_p[4][None], st_s[4][None])
```

```python
import functools
import math

import numpy as np
import jax
import jax.numpy as jnp
from jax import lax
from jax.experimental import pallas as pl
from jax.experimental.pallas import tpu as pltpu

F32 = jnp.float32
BF16 = jnp.bfloat16

D_MODEL = 1024
HEAD_DIM = 64
N_HEADS = 8
N_KV_HEADS = 2
GROUP = N_HEADS // N_KV_HEADS
ATTN_WIDTH = N_HEADS * HEAD_DIM
KV_WIDTH = N_KV_HEADS * HEAD_DIM
KV_ROW = 2 * KV_WIDTH
CMP_BLOCK = 64
CMP_HIDDEN = 128
N_SEL = 16
WINDOW = 512
ROT_DIM = HEAD_DIM // 4
ROT_HALF = ROT_DIM // 2
ROPE_THETA = 500000.0
RNN_WIDTH = D_MODEL - ATTN_WIDTH
CONV_W = 4
LRU_C = 8.0
D_FF = 2816
PAGE_SIZE = 128
BLOCKS_PER_PAGE = PAGE_SIZE // CMP_BLOCK
ROW_PITCH = CMP_BLOCK + 8
LSTEP = 8
SCORE_LOOKAHEAD = {"sel": 4, "win": 8}
WIN_SAMPLE_ROWS = 4
SAMPLE_ROWS = 16
ATTN_SCALE = HEAD_DIM ** -0.5
LOG2_E = math.log2(math.e)
EPS = 1e-6
NEG_INF = -1e30
M_INIT = -1e29
FORCE_SCORE = 1e4
LANES = 128
IN_COLS_PAD = 19 * LANES
GATE_COL = 2 * RNN_WIDTH + ATTN_WIDTH + 3 * KV_ROW
VMEM_LIMIT = 56 * 1024 * 1024


def _cp(sem, vmem=VMEM_LIMIT):
    return pltpu.CompilerParams(dimension_semantics=sem, vmem_limit_bytes=vmem)


def _sds(shape, dt=F32):
    return jax.ShapeDtypeStruct(shape, dt)


def _rms(x, g):
    return x * lax.rsqrt(jnp.mean(x * x, axis=-1, keepdims=True) + EPS) * g


def _bdot(a, b):
    return jnp.dot(a.astype(BF16), b, preferred_element_type=F32)


def _dot_nt(a, b, precision=None):
    return lax.dot_general(a, b, (((1,), (1,)), ((), ())), precision=precision, preferred_element_type=F32)


def _ada_kernel(c_ref, w_ref, b_ref, o_ref):
    c = c_ref[...]
    o_ref[...] = _bdot(c * jax.nn.sigmoid(c), w_ref[...]) + b_ref[...]


def _ada(c_all, w_bf, b):
    n = c_all.shape[0]
    return pl.pallas_call(
        _ada_kernel, grid=(6,),
        in_specs=[pl.BlockSpec((n, D_MODEL), lambda j: (0, 0)),
                  pl.BlockSpec((D_MODEL, D_MODEL), lambda j: (0, j)),
                  pl.BlockSpec((1, D_MODEL), lambda j: (0, j))],
        out_specs=pl.BlockSpec((n, D_MODEL), lambda j: (0, j)),
        out_shape=_sds((n, 6 * D_MODEL)), compiler_params=_cp(("arbitrary",)), name="ada")(c_all, w_bf, b)


def _rope(v, cos, s1, s2):
    w = v.shape[-1]
    return v * cos + pltpu.roll(v, w - ROT_HALF, 1) * s1 + pltpu.roll(v, ROT_HALF, 1) * s2


def _pre_kernel(x_ref, shift_ref, scale_ref, ln_ref, w_ref, cos_ref, s1_ref, s2_ref,
                xr_ref, yr_ref, q_ref, qrot_ref, kvc_ref, ks_ref, kw_ref, kvct_ref, kvst_ref, kvwt_ref, g_ref, qt_ref):
    x = x_ref[0]
    h = _rms(x, ln_ref[...]) * (1.0 + scale_ref[0]) + shift_ref[0]
    z = _bdot(h, w_ref[...])
    xr_ref[0] = z[:, 0:RNN_WIDTH]
    yr_ref[0] = z[:, RNN_WIDTH:2 * RNN_WIDTH]
    q = z[:, 2 * RNN_WIDTH:2 * RNN_WIDTH + ATTN_WIDTH]
    q_ref[0] = q
    cos, s1, s2 = cos_ref[...], s1_ref[...], s2_ref[...]
    rep = ATTN_WIDTH // LANES
    qrot = _rope(q, jnp.concatenate([cos] * rep, -1), jnp.concatenate([s1] * rep, -1),
                 jnp.concatenate([s2] * rep, -1))
    qrot_ref[0] = qrot
    qt_ref[0] = (qrot * (ATTN_SCALE * LOG2_E)).T.astype(BF16)
    c0 = 2 * RNN_WIDTH + ATTN_WIDTH
    kvc = z[:, c0:c0 + KV_ROW]
    kvc_ref[0] = kvc
    kvct_ref[0] = kvc.T
    tm = x.shape[0]
    for i, kref, ref in ((1, ks_ref, kvst_ref), (2, kw_ref, kvwt_ref)):
        kv = z[:, c0 + i * KV_ROW:c0 + (i + 1) * KV_ROW]
        keys = _rope(kv[:, :KV_WIDTH], cos, s1, s2)
        if i == 1:
            blk = (pl.program_id(1) * tm + lax.broadcasted_iota(jnp.int32, (tm, HEAD_DIM), 0)) // CMP_BLOCK
            onehot = jnp.where(lax.broadcasted_iota(jnp.int32, (tm, HEAD_DIM), 1) == blk, 1.0, 0.0)
            kref[0] = jnp.concatenate([keys[:, :HEAD_DIM], onehot, keys[:, HEAD_DIM:], onehot], axis=-1)
        else:
            kref[0] = keys
        ref[0, 0:KV_WIDTH, :] = keys.T
        ref[0, KV_WIDTH:KV_ROW, :] = kv[:, KV_WIDTH:].T
    g_ref[0] = jax.nn.sigmoid(z[:, GATE_COL:GATE_COL + LANES])


def _pre(x3, shift, scale, ln, w_in_bf, cos, s1, s2, tm):
    bx, sx, _ = x3.shape
    sm = shift.shape[1]
    if sm == 1:
        mod_spec = pl.BlockSpec((1, 1, D_MODEL), lambda b, t: (b, 0, 0))
    else:
        mod_spec = pl.BlockSpec((1, tm, D_MODEL), lambda b, t: (b, t, 0))
    tab = pl.BlockSpec((tm, LANES), lambda b, t: (t, 0))
    row = lambda w: (pl.BlockSpec((1, tm, w), lambda b, t: (b, t, 0)), _sds((bx, sx, w)))
    col = (pl.BlockSpec((1, KV_ROW, tm), lambda b, t: (b, 0, t)), _sds((bx, KV_ROW, sx)))
    qt = (pl.BlockSpec((1, ATTN_WIDTH, tm), lambda b, t: (b, 0, t)), _sds((bx, ATTN_WIDTH, sx), BF16))
    outs = [row(RNN_WIDTH), row(RNN_WIDTH), row(ATTN_WIDTH), row(ATTN_WIDTH), row(KV_ROW), row(2 * KV_WIDTH),
            row(KV_WIDTH), col, col, col, row(LANES), qt]
    return pl.pallas_call(
        _pre_kernel, grid=(bx, sx // tm),
        in_specs=[pl.BlockSpec((1, tm, D_MODEL), lambda b, t: (b, t, 0)), mod_spec, mod_spec,
                  pl.BlockSpec((1, D_MODEL), lambda b, t: (0, 0)),
                  pl.BlockSpec((D_MODEL, IN_COLS_PAD), lambda b, t: (0, 0)), tab, tab, tab],
        out_specs=[o[0] for o in outs], out_shape=[o[1] for o in outs],
        compiler_params=_cp(("arbitrary", "arbitrary")), name="pre")(x3, shift, scale, ln, w_in_bf, cos, s1, s2)


def _lru_gates(xc, wax_ref, bax_ref, lam_ref):
    ra = _bdot(xc, wax_ref[...]) + bax_ref[...]
    r = jax.nn.sigmoid(ra[:, :RNN_WIDTH])
    i = jax.nn.sigmoid(ra[:, RNN_WIDTH:])
    nl = -lam_ref[...]
    softplus = jnp.maximum(nl, 0.0) + jnp.log(1.0 + jnp.exp(-jnp.abs(nl)))
    a = jnp.exp(-LRU_C * r * softplus)
    v = 1.0 - a * a
    return a, jnp.where(v > 0.0, v * lax.rsqrt(v), 0.0) * (i * xc)


def _scan_rows(a, b):
    t, c = a.shape
    row = lax.broadcasted_iota(jnp.int32, a.shape, 0)
    k = 1
    while k < t:
        if k < 8:
            a_s = jnp.where(row >= k, pltpu.roll(a, k, 0), 1.0)
            b_s = jnp.where(row >= k, pltpu.roll(b, k, 0), 0.0)
        else:
            a_s = jnp.concatenate([jnp.ones((k, c), F32), a[:t - k]], axis=0)
            b_s = jnp.concatenate([jnp.zeros((k, c), F32), b[:t - k]], axis=0)
        b = a * b_s + b
        a = a * a_s
        k *= 2
    return a, b


def _rglru_kernel(xr_ref, yr_ref, cw_ref, cb_ref, wax_ref, bax_ref, lam_ref, o_ref, hl_ref, tail, hc, *, t):
    @pl.when(pl.program_id(1) == 0)
    def _():
        tail[...] = jnp.zeros_like(tail)
        hc[...] = jnp.zeros_like(hc)

    x = xr_ref[0]
    prev = tail[...]
    row8 = lax.broadcasted_iota(jnp.int32, (8, RNN_WIDTH), 0)
    xc = cb_ref[...] + x * cw_ref[CONV_W - 1:CONV_W, :]
    for j in range(1, CONV_W):
        rolled = pltpu.roll(x, j, 0)
        head = jnp.where(row8 < j, pltpu.roll(prev, j, 0), rolled[0:8])
        xc = xc + jnp.concatenate([head, rolled[8:]], axis=0) * cw_ref[CONV_W - 1 - j:CONV_W - j, :]
    tail[...] = x[t - 8:t]
    a, b = _lru_gates(xc, wax_ref, bax_ref, lam_ref)
    ac, bc = _scan_rows(a, b)
    h = ac * hc[...] + bc
    hc[...] = h[t - 1:t, :]
    hl_ref[0] = h[t - 1:t, :]
    o_ref[0] = h * jax.nn.gelu(yr_ref[0])


def _rglru_prompt(xr, yr, cw, cb, wax, bax, lam, t=256):
    b, s, _ = xr.shape
    tile = pl.BlockSpec((1, t, RNN_WIDTH), lambda i, j: (i, j, 0))
    full = lambda shp: pl.BlockSpec(shp, lambda i, j: (0,) * len(shp))
    return pl.pallas_call(
        functools.partial(_rglru_kernel, t=t), grid=(b, s // t),
        in_specs=[tile, tile, full((CONV_W, RNN_WIDTH)), full((1, RNN_WIDTH)),
                  full((RNN_WIDTH, 2 * RNN_WIDTH)), full((1, 2 * RNN_WIDTH)), full((1, RNN_WIDTH))],
        out_specs=[tile, pl.BlockSpec((1, 1, RNN_WIDTH), lambda i, j: (i, 0, 0))],
        out_shape=[_sds((b, s, RNN_WIDTH)), _sds((b, 1, RNN_WIDTH))],
        scratch_shapes=[pltpu.VMEM((8, RNN_WIDTH), F32), pltpu.VMEM((1, RNN_WIDTH), F32)],
        compiler_params=_cp(("arbitrary", "arbitrary")), name="rglru_prompt")(xr, yr, cw, cb, wax, bax, lam)


def _rglru_sample_kernel(xr_ref, yr_ref, cp_ref, h0_ref, cw_ref, cb_ref, wax_ref, bax_ref, lam_ref,
                         o_ref, hl_ref, *, s, nb):
    xpad = [cp_ref[i] for i in range(CONV_W - 1)] + [xr_ref[i] for i in range(s)]
    xcs = []
    for j in range(s):
        xc = cb_ref[...]
        for k in range(CONV_W):
            xc = xc + xpad[j + k] * cw_ref[k:k + 1, :]
        xcs.append(xc)
    a, b = _lru_gates(jnp.concatenate(xcs, axis=0), wax_ref, bax_ref, lam_ref)
    h = h0_ref[...]
    for j in range(s):
        h = a[j * nb:(j + 1) * nb] * h + b[j * nb:(j + 1) * nb]
        o_ref[j] = h * jax.nn.gelu(yr_ref[j])
    hl_ref[...] = h


def _rglru_sample(xr_t, yr_t, cp_t, h0, cw, cb, wax, bax, lam):
    s, nb, _ = xr_t.shape
    return pl.pallas_call(
        functools.partial(_rglru_sample_kernel, s=s, nb=nb),
        out_shape=[_sds((s, nb, RNN_WIDTH)), _sds((nb, RNN_WIDTH))],
        compiler_params=pltpu.CompilerParams(vmem_limit_bytes=VMEM_LIMIT), name="rglru_sample")(
            xr_t, yr_t, cp_t, h0, cw, cb, wax, bax, lam)


def _build_w1bd(w1_ref, w1bd):
    w1bd[...] = jnp.zeros_like(w1bd)
    for c in range(2):
        for k in range(N_KV_HEADS):
            r0 = c * KV_WIDTH + k * HEAD_DIM
            f0 = (c * N_KV_HEADS + k) * CMP_HIDDEN
            w1bd[:, r0:r0 + HEAD_DIM, f0:f0 + CMP_HIDDEN] = w1_ref[c]


def _compress_blocks(get_rows, m, pe_ref, w1bd, w2_ref, acc):
    acc[...] = jnp.zeros_like(acc)
    step = 4

    def body(i, carry):
        l0 = pl.multiple_of(i * step, step)
        z = jnp.concatenate([get_rows(l0 + j) + pe_ref[pl.ds(l0 + j, 1), :] for j in range(step)], axis=1)
        acc[...] += _bdot(z, w1bd[pl.ds(l0, step)].reshape(step * KV_ROW, 4 * CMP_HIDDEN))
        return carry

    lax.fori_loop(0, CMP_BLOCK // step, body, 0)
    return _bdot(jax.nn.gelu(acc[...]), w2_ref[...])


def _compress_prompt_kernel(kv_ref, pe_ref, w1_ref, w2_ref, o_ref, w1bd, acc, *, nblk):
    @pl.when(pl.program_id(0) == 0)
    def _():
        _build_w1bd(w1_ref, w1bd)

    o_ref[0] = _compress_blocks(lambda l: kv_ref[0, :, l, :], nblk, pe_ref, w1bd, w2_ref, acc)


def _compress_prompt(kvc, pe_row, w1_bf, w2bd_bf):
    b, s, _ = kvc.shape
    nblk = s // CMP_BLOCK
    kv4 = kvc.reshape(b, nblk, CMP_BLOCK, KV_ROW)
    return pl.pallas_call(
        functools.partial(_compress_prompt_kernel, nblk=nblk), grid=(b,),
        in_specs=[pl.BlockSpec((1, nblk, CMP_BLOCK, KV_ROW), lambda i: (i, 0, 0, 0)),
                  pl.BlockSpec((CMP_BLOCK, KV_ROW), lambda i: (0, 0)),
                  pl.BlockSpec((2, CMP_BLOCK, HEAD_DIM, CMP_HIDDEN), lambda i: (0, 0, 0, 0)),
                  pl.BlockSpec((4 * CMP_HIDDEN, KV_ROW), lambda i: (0, 0))],
        out_specs=pl.BlockSpec((1, nblk, KV_ROW), lambda i: (i, 0, 0)),
        out_shape=_sds((b, nblk, KV_ROW)),
        scratch_shapes=[pltpu.VMEM((CMP_BLOCK, KV_ROW, 4 * CMP_HIDDEN), BF16),
                        pltpu.VMEM((nblk, 4 * CMP_HIDDEN), F32)],
        compiler_params=_cp(("arbitrary",)), name="compress_prompt")(kv4, pe_row, w1_bf, w2bd_bf)


def _page_copies(pt_ref, pool_ref, dst, sem, b, slot, n_pages):
    return [pltpu.make_async_copy(pool_ref.at[pt_ref[b, p]], dst(slot, p), sem.at[slot]) for p in range(n_pages)]


def _paged_prefetch(pt_ref, pool_ref, dst, sem, n_pages):
    b = pl.program_id(0)
    nb = pl.num_programs(0)
    slot = lax.rem(b, 2)

    @pl.when(b == 0)
    def _():
        for cp in _page_copies(pt_ref, pool_ref, dst, sem, 0, 0, n_pages):
            cp.start()

    @pl.when(b + 1 < nb)
    def _():
        for cp in _page_copies(pt_ref, pool_ref, dst, sem, b + 1, 1 - slot, n_pages):
            cp.start()

    for cp in _page_copies(pt_ref, pool_ref, dst, sem, b, slot, n_pages):
        cp.wait()
    return slot


def _compress_sample_kernel(pt_ref, pool_ref, new_ref, pe_ref, w1_ref, w2_ref, o_ref,
                            stage, sem, rows, w1s, acc, *, n_pages, m, nbp, n_rows):
    nblk = n_pages * BLOCKS_PER_PAGE
    n_iter = CMP_BLOCK // LSTEP
    assert n_pages % n_iter == 0
    pages_per_iter = n_pages // n_iter
    b = pl.program_id(0)
    cur = lax.rem(b, 2)
    nxt = 1 - cur

    def copies(row, slot):
        return _page_copies(pt_ref, pool_ref, lambda s, p: stage.at[s, p], sem, row, slot, n_pages)

    def half_to_rows(slot, p, c):
        r0 = pl.multiple_of(p * (BLOCKS_PER_PAGE * ROW_PITCH), 8)
        t = stage[slot, p, c * KV_WIDTH:(c + 1) * KV_WIDTH, :].T
        for j in range(BLOCKS_PER_PAGE):
            rows[slot, c, pl.ds(r0 + j * ROW_PITCH, CMP_BLOCK), :] = t[j * CMP_BLOCK:(j + 1) * CMP_BLOCK]

    def to_rows(slot, p):
        for c in range(2):
            half_to_rows(slot, p, c)

    @pl.when(b == 0)
    def _():
        w1s[...] = jnp.zeros_like(w1s)
        for c in range(2):
            for k in range(N_KV_HEADS):
                w1s[c, :, k * HEAD_DIM:(k + 1) * HEAD_DIM, k * CMP_HIDDEN:(k + 1) * CMP_HIDDEN] = w1_ref[c]
            for s in range(2):
                rows[s, c, nblk * ROW_PITCH:m * ROW_PITCH, :] = jnp.zeros(((m - nblk) * ROW_PITCH, KV_WIDTH), F32)
        for s in range(min(2, n_rows)):
            for cp in copies(s, s):
                cp.start()
        for cp in copies(0, 0):
            cp.wait()

        def first(p, carry):
            to_rows(0, p)
            return carry

        lax.fori_loop(0, n_pages, first, 0, unroll=4)

    @pl.when(b + 1 < n_rows)
    def _():
        for cp in copies(b + 1, nxt):
            cp.wait()

    @pl.when(b + 2 < n_rows)
    def _():
        for cp in copies(b + 2, cur):
            cp.start()

    for c in range(2):
        rows[cur, c, nblk * ROW_PITCH:nblk * ROW_PITCH + 8, :] = new_ref[0, :, c * KV_WIDTH:(c + 1) * KV_WIDTH]
    acc[...] = jnp.zeros_like(acc)

    def body(i, carry):
        l0 = pl.multiple_of(i * LSTEP, LSTEP)
        pe = pe_ref[pl.ds(l0, LSTEP), :]
        halves = [(pp, c) for pp in range(pages_per_iter) for c in range(2)] if n_rows > 1 else []
        per_gather = -(-len(halves) // LSTEP)
        prods = []
        for c in range(2):
            parts = []
            for j in range(LSTEP):
                parts.append(rows[cur, c, pl.ds(l0 + j, m, stride=ROW_PITCH), :]
                             + pe[j:j + 1, c * KV_WIDTH:(c + 1) * KV_WIDTH])
                if c == 1:
                    for pp, hc in halves[:per_gather]:
                        half_to_rows(nxt, i * pages_per_iter + pp, hc)
                    halves = halves[per_gather:]
            w = w1s[c, pl.ds(l0, LSTEP)].reshape(LSTEP * KV_WIDTH, N_KV_HEADS * CMP_HIDDEN)
            prods.append(_bdot(jnp.concatenate(parts, axis=1), w))
        for c in range(2):
            acc[c] += prods[c]
        return carry

    lax.fori_loop(0, n_iter, body, 0)
    hw = N_KV_HEADS * CMP_HIDDEN
    for c in range(2):
        o_ref[0, 0:m, c * KV_WIDTH:(c + 1) * KV_WIDTH] = _bdot(
            jax.nn.gelu(acc[c]), w2_ref[c * hw:(c + 1) * hw, c * KV_WIDTH:(c + 1) * KV_WIDTH])
    o_ref[0, m:nbp, :] = jnp.zeros((nbp - m, KV_ROW), F32)


def _compress_sample(page_table, pool_t, kvc_new8, pe_row, w1_bf, w2bd_bf, nbp):
    b, n_pages = page_table.shape
    m = n_pages * BLOCKS_PER_PAGE + 8
    grid_spec = pltpu.PrefetchScalarGridSpec(
        num_scalar_prefetch=1, grid=(b,),
        in_specs=[pl.BlockSpec(memory_space=pl.ANY),
                  pl.BlockSpec((1, 8, KV_ROW), lambda i, pt: (i, 0, 0)),
                  pl.BlockSpec((CMP_BLOCK, KV_ROW), lambda i, pt: (0, 0)),
                  pl.BlockSpec((2, CMP_BLOCK, HEAD_DIM, CMP_HIDDEN), lambda i, pt: (0, 0, 0, 0)),
                  pl.BlockSpec((4 * CMP_HIDDEN, KV_ROW), lambda i, pt: (0, 0))],
        out_specs=pl.BlockSpec((1, nbp, KV_ROW), lambda i, pt: (i, 0, 0)),
        scratch_shapes=[pltpu.VMEM((2, n_pages, KV_ROW, PAGE_SIZE), F32), pltpu.SemaphoreType.DMA((2,)),
                        pltpu.VMEM((2, 2, m * ROW_PITCH, KV_WIDTH), F32),
                        pltpu.VMEM((2, CMP_BLOCK, KV_WIDTH, N_KV_HEADS * CMP_HIDDEN), BF16),
                        pltpu.VMEM((2, m, N_KV_HEADS * CMP_HIDDEN), F32)])
    return pl.pallas_call(
        functools.partial(_compress_sample_kernel, n_pages=n_pages, m=m, nbp=nbp, n_rows=b),
        grid_spec=grid_spec, out_shape=_sds((b, nbp, KV_ROW)),
        compiler_params=_cp(("arbitrary",)), name="compress_sample")(page_table, pool_t, kvc_new8, pe_row, w1_bf, w2bd_bf)


def _head_lanes(q, h, dst, scale):
    src = q[:, (h // 2) * LANES:(h // 2 + 1) * LANES]
    if (h % 2) != dst:
        src = pltpu.roll(src, HEAD_DIM, 1)
    lane = lax.broadcasted_iota(jnp.int32, src.shape, 1)
    keep = (lane >= dst * HEAD_DIM) & (lane < (dst + 1) * HEAD_DIM)
    return jnp.where(keep, src * scale, 0.0)


def _arrange_q(q, h, scale):
    return _head_lanes(q, h, h // GROUP, scale)


def _gather_heads(o_list):
    parts = []
    for h, o in enumerate(o_list):
        kvh = h // GROUP
        parts.append(o[:, kvh * HEAD_DIM:(kvh + 1) * HEAD_DIM])
    return jnp.concatenate(parts, axis=-1)


def _select_mask_t(imp, qpos, nb):
    j = lax.broadcasted_iota(jnp.int32, imp.shape, 0)
    cur = qpos // CMP_BLOCK
    forced = (j == 0) | (j == cur) | (j == cur - 1)
    score = jnp.where(j > cur, -FORCE_SCORE, jnp.where(forced, FORCE_SCORE, imp))
    if imp.shape[0] > nb:
        score = jnp.where(j < nb, score, -3e38)
    sub = 8
    ngrp = imp.shape[0] // sub
    groups = [score[r * sub:(r + 1) * sub] for r in range(ngrp)]
    jsub = lax.broadcasted_iota(jnp.int32, groups[0].shape, 0)
    ranks = [jnp.zeros(groups[0].shape, F32) for _ in range(ngrp)]
    for i in range(nb):
        row = score[i:i + 1]
        for r in range(ngrp):
            if r * sub > i:
                hit = row >= groups[r]
            elif (r + 1) * sub - 1 <= i:
                hit = row > groups[r]
            else:
                hit = ((jsub + r * sub > i) & (row >= groups[r])) | (row > groups[r])
            ranks[r] = ranks[r] + jnp.where(hit, 1.0, 0.0)
    rank = jnp.concatenate(ranks, axis=0)
    return jnp.where((rank < float(min(N_SEL, nb))) & (j <= cur) & (j < nb), 1.0, 0.0)


def _select_mask_tile(imp, qpos, nb, cur_max, score_s, rank_s):
    sub = 8
    ngrp = nb // sub
    j = lax.broadcasted_iota(jnp.int32, imp.shape, 0)
    cur = qpos // CMP_BLOCK
    forced = (j == 0) | (j == cur) | (j == cur - 1)
    score_s[...] = jnp.where(j > cur, -FORCE_SCORE, jnp.where(forced, FORCE_SCORE, imp))
    rank_s[...] = jnp.zeros_like(rank_s)
    jsub = lax.broadcasted_iota(jnp.int32, (sub, imp.shape[1]), 0)
    for ci in range(ngrp):
        @pl.when(ci * sub <= cur_max)
        def _(ci=ci):
            rows = score_s[ci * sub:(ci + 1) * sub]
            for r in range(ngrp):
                @pl.when(r * sub <= cur_max)
                def _(r=r):
                    grp = score_s[r * sub:(r + 1) * sub]
                    acc = rank_s[r * sub:(r + 1) * sub]
                    for ii in range(sub):
                        i = ci * sub + ii
                        row = rows[ii:ii + 1]
                        if r * sub > i:
                            hit = row >= grp
                        elif (r + 1) * sub - 1 <= i:
                            hit = row > grp
                        else:
                            hit = ((jsub + r * sub > i) & (row >= grp)) | (row > grp)
                        acc = acc + jnp.where(hit, 1.0, 0.0)
                    rank_s[r * sub:(r + 1) * sub] = acc
    return jnp.where((rank_s[...] < float(min(N_SEL, nb))) & (j <= cur), 1.0, 0.0)


def _cmp_kernel_t(q_ref, ckv_ref, oc_ref, selt_ref, score_s, rank_s, *, tq, nb, nbp, pos0):
    qpos = pos0 + pl.program_id(1) * tq + lax.broadcasted_iota(jnp.int32, (nb, tq), 1)
    j = lax.broadcasted_iota(jnp.int32, (nb, tq), 0)
    m = (j + 1) * CMP_BLOCK - 1 <= qpos
    q = q_ref[0]
    ck = ckv_ref[0, :, 0:KV_WIDTH]
    cv = ckv_ref[0, :, KV_WIDTH:KV_ROW].astype(BF16)
    m4 = jnp.concatenate([m] * GROUP, axis=1)
    outs = []
    for kvh in range(N_KV_HEADS):
        qa = jnp.concatenate([_arrange_q(q, kvh * GROUP + g, ATTN_SCALE) for g in range(GROUP)], axis=0)
        s = jnp.where(m4, _dot_nt(ck, qa, precision=lax.Precision.HIGHEST), NEG_INF)
        e = jnp.exp(s - jnp.max(s, axis=0, keepdims=True))
        p = jnp.where(m4, e / jnp.sum(e, axis=0, keepdims=True), 0.0)
        imp = p[:, 0:tq]
        for g in range(1, GROUP):
            imp = imp + p[:, g * tq:(g + 1) * tq]
        o = lax.dot_general(p.astype(BF16), cv, (((0,), (0,)), ((), ())), preferred_element_type=F32)
        outs += [o[g * tq:(g + 1) * tq] for g in range(GROUP)]
        cur_max = (pos0 + (pl.program_id(1) + 1) * tq - 1) // CMP_BLOCK
        selt_ref[0, kvh, 0:nb, :] = _select_mask_tile(imp, qpos, nb, cur_max, score_s, rank_s)
        selt_ref[0, kvh, nb:nbp, :] = jnp.zeros((nbp - nb, tq), F32)
    oc_ref[0] = _gather_heads(outs)


def _cmp_select_t(q, ckv, nbp, pos0, tq):
    b, sq, _ = q.shape
    nb = ckv.shape[1]
    return pl.pallas_call(
        functools.partial(_cmp_kernel_t, tq=tq, nb=nb, nbp=nbp, pos0=pos0), grid=(b, sq // tq),
        in_specs=[pl.BlockSpec((1, tq, ATTN_WIDTH), lambda i, t: (i, t, 0)),
                  pl.BlockSpec((1, nb, KV_ROW), lambda i, t: (i, 0, 0))],
        out_specs=[pl.BlockSpec((1, tq, ATTN_WIDTH), lambda i, t: (i, t, 0)),
                   pl.BlockSpec((1, N_KV_HEADS, nbp, tq), lambda i, t: (i, 0, 0, t))],
        out_shape=[_sds((b, sq, ATTN_WIDTH)), _sds((b, N_KV_HEADS, nbp, sq))],
        scratch_shapes=[pltpu.VMEM((nb, tq), F32), pltpu.VMEM((nb, tq), F32)],
        compiler_params=_cp(("arbitrary", "arbitrary")), name="cmp_select_t")(q, ckv)


def _cmp_sample_kernel(q_ref, ckv_ref, oc_ref, selt_ref, *, nb, nbr, pos0):
    nq = 8
    lanes = GROUP * nq
    j = lax.broadcasted_iota(jnp.int32, (nbr, lanes), 0)
    qpos = pos0 + lax.rem(lax.broadcasted_iota(jnp.int32, (nbr, lanes), 1), nq)
    m = ((j + 1) * CMP_BLOCK - 1 <= qpos) & (j < nb)
    imps = [[] for _ in range(N_KV_HEADS)]
    for i in range(q_ref.shape[0]):
        q = q_ref[i]
        ck = ckv_ref[i, 0:nbr, 0:KV_WIDTH]
        cv = ckv_ref[i, 0:nbr, KV_WIDTH:KV_ROW].astype(BF16)
        outs = []
        for kvh in range(N_KV_HEADS):
            qa = jnp.concatenate([_arrange_q(q, kvh * GROUP + g, ATTN_SCALE) for g in range(GROUP)], axis=0)
            s = jnp.where(m, _dot_nt(ck, qa, precision=lax.Precision.HIGHEST), NEG_INF)
            e = jnp.exp(s - jnp.max(s, axis=0, keepdims=True))
            p = jnp.where(m, e / jnp.sum(e, axis=0, keepdims=True), 0.0)
            imp = p[:, 0:nq]
            for g in range(1, GROUP):
                imp = imp + p[:, g * nq:(g + 1) * nq]
            imps[kvh].append(imp)
            o = lax.dot_general(p.astype(BF16), cv, (((0,), (0,)), ((), ())), preferred_element_type=F32)
            outs += [o[g * nq:(g + 1) * nq] for g in range(GROUP)]
        oc_ref[i] = _gather_heads(outs)
    qpos_all = pos0 + lax.rem(lax.broadcasted_iota(jnp.int32, (nbr, LANES), 1), nq)
    for kvh in range(N_KV_HEADS):
        selt_ref[0, kvh] = _select_mask_t(jnp.concatenate(imps[kvh], axis=1), qpos_all, nb)


def _cmp_select_sample(q8, ckv_p, nb, pos0):
    b = q8.shape[0]
    nbp = ckv_p.shape[1]
    nbr = 8 * (-(-nb // 8))
    rows = SAMPLE_ROWS
    assert b % rows == 0
    o_c, sel_t = pl.pallas_call(
        functools.partial(_cmp_sample_kernel, nb=nb, nbr=nbr, pos0=pos0), grid=(b // rows,),
        in_specs=[pl.BlockSpec((rows, 8, ATTN_WIDTH), lambda i: (i, 0, 0)),
                  pl.BlockSpec((rows, nbp, KV_ROW), lambda i: (i, 0, 0))],
        out_specs=[pl.BlockSpec((rows, 8, ATTN_WIDTH), lambda i: (i, 0, 0)),
                   pl.BlockSpec((1, N_KV_HEADS, nbr, LANES), lambda i: (i, 0, 0, 0))],
        out_shape=[_sds((b, 8, ATTN_WIDTH)), _sds((b // rows, N_KV_HEADS, nbr, LANES))],
        compiler_params=_cp(("arbitrary",)), name="cmp_select_sample")(q8, ckv_p)
    sel = jnp.transpose(sel_t.reshape(b // rows, N_KV_HEADS, nbr, rows, 8), (0, 3, 1, 4, 2))
    return o_c, jnp.pad(sel.reshape(b, N_KV_HEADS, 8, nbr), ((0, 0), (0, 0), (0, 0), (0, nbp - nbr)))


def _flash_kernel(tab_ref, q_ref, k_ref, vt_ref, *rest, tq, tk, mode):
    if mode == "sel":
        selt_ref, o_ref, qt, m_s, l_s, acc = rest
    else:
        o_ref, qt, m_s, l_s, acc = rest
    step_id = pl.program_id(1)
    qi = tab_ref[0, step_id]
    ki = tab_ref[1, step_id]
    flags = tab_ref[2, step_id]

    @pl.when((flags & 1) != 0)
    def _():
        zeros = jnp.zeros((HEAD_DIM, tq), BF16)
        for h in range(N_HEADS):
            kvh = h // GROUP
            qh = q_ref[0, h * HEAD_DIM:(h + 1) * HEAD_DIM, :]
            if mode == "sel":
                bias = ((selt_ref[0, kvh, 0:HEAD_DIM, :] - 1.0) * (-NEG_INF)).astype(BF16)
                cols = jnp.concatenate([qh, bias], axis=0)
            else:
                cols = jnp.concatenate([qh, zeros] if kvh == 0 else [zeros, qh], axis=0)
            qt[:, h * tq:(h + 1) * tq] = cols
        m_s[...] = jnp.full_like(m_s, M_INIT)
        l_s[...] = jnp.zeros_like(l_s)
        acc[...] = jnp.zeros_like(acc)

    def step(masked):
        vt = vt_ref[0].astype(BF16)
        if masked:
            d = (lax.broadcasted_iota(jnp.int32, (tk, tq), 0) - lax.broadcasted_iota(jnp.int32, (tk, tq), 1)
                 + (ki * tk - qi * tq))
            valid = (d <= 0) if mode == "sel" else (d <= 0) & (d > -WINDOW)
        if mode == "sel":
            ks = [k_ref[0, :, kvh * LANES:(kvh + 1) * LANES].astype(BF16) for kvh in range(N_KV_HEADS)]
        else:
            ks = [k_ref[0].astype(BF16)] * N_KV_HEADS

        def scores(h):
            return jnp.dot(ks[h // GROUP], qt[:, h * tq:(h + 1) * tq], preferred_element_type=F32)

        def accumulate(h, alpha, pv):
            rows = slice(h * LANES, (h + 1) * LANES)
            acc[rows] = alpha * acc[rows] + pv

        look = SCORE_LOOKAHEAD[mode]
        ahead = [scores(h) for h in range(look)]
        pending = None
        for h in range(N_HEADS):
            s = ahead.pop(0)
            if h + look < N_HEADS:
                ahead.append(scores(h + look))
            if masked:
                s = jnp.where(valid, s, NEG_INF)
            m_old = m_s[h, 0:1, :]
            m_new = jnp.maximum(m_old, jnp.max(s, axis=0, keepdims=True))
            alpha = jnp.exp2(m_old - m_new)
            p = jnp.exp2(s - m_new)
            l_s[h] = jnp.broadcast_to(alpha * l_s[h, 0:1, :] + jnp.sum(p, axis=0, keepdims=True), (8, tq))
            m_s[h] = jnp.broadcast_to(m_new, (8, tq))
            pv = jnp.dot(vt, p.astype(BF16), preferred_element_type=F32)
            if pending is not None:
                accumulate(*pending)
            pending = (h, alpha, pv)
        accumulate(*pending)

    @pl.when((flags & 4) != 0)
    def _():
        step(True)

    @pl.when((flags & 4) == 0)
    def _():
        step(False)

    @pl.when((flags & 2) != 0)
    def _():
        for h in range(N_HEADS):
            r0 = h * LANES + (h // GROUP) * HEAD_DIM
            o_ref[0, h * HEAD_DIM:(h + 1) * HEAD_DIM, :] = acc[r0:r0 + HEAD_DIM] / l_s[h, 0:1, :]


def _flash_schedule(s, tq, tk, mode):
    steps = []
    for qi in range(s // tq):
        q_lo, q_hi = qi * tq, qi * tq + tq - 1
        k_lo = 0 if mode == "sel" else max(q_lo - (WINDOW - 1), 0)
        tiles = list(range(k_lo // tk, q_hi // tk + 1))
        for ki in tiles:
            causal_edge = ki * tk + tk - 1 > q_lo
            window_edge = mode == "win" and q_hi - ki * tk >= WINDOW
            flags = (1 if ki == tiles[0] else 0) | (2 if ki == tiles[-1] else 0) | (4 if causal_edge or window_edge else 0)
            steps.append((qi, ki, flags))
    return jnp.asarray(np.array(steps, np.int32).T)


def _flash_prompt(q_t, k_rows, kv_t, sel_t, mode, tq, tk):
    b, _, s = q_t.shape
    if mode == "sel":
        assert s // CMP_BLOCK <= HEAD_DIM
    tab = _flash_schedule(s, tq, tk, mode)
    kw = k_rows.shape[-1]
    in_specs = [pl.BlockSpec((1, ATTN_WIDTH, tq), lambda i, p, t: (i, 0, t[0, p])),
                pl.BlockSpec((1, tk, kw), lambda i, p, t: (i, t[1, p], 0)),
                pl.BlockSpec((1, KV_WIDTH, tk), lambda i, p, t: (i, 1, t[1, p]))]
    args = [q_t, k_rows, kv_t]
    if mode == "sel":
        nbp = sel_t.shape[2]
        in_specs += [pl.BlockSpec((1, N_KV_HEADS, nbp, tq), lambda i, p, t: (i, 0, 0, t[0, p]))]
        args += [sel_t]
    grid_spec = pltpu.PrefetchScalarGridSpec(
        num_scalar_prefetch=1, grid=(b, tab.shape[1]), in_specs=in_specs,
        out_specs=pl.BlockSpec((1, ATTN_WIDTH, tq), lambda i, p, t: (i, 0, t[0, p])),
        scratch_shapes=[pltpu.VMEM((LANES, N_HEADS * tq), BF16), pltpu.VMEM((N_HEADS, 8, tq), F32),
                        pltpu.VMEM((N_HEADS, 8, tq), F32), pltpu.VMEM((N_HEADS * LANES, tq), F32)])
    return pl.pallas_call(
        functools.partial(_flash_kernel, tq=tq, tk=tk, mode=mode), grid_spec=grid_spec,
        out_shape=_sds((b, ATTN_WIDTH, s)),
        compiler_params=_cp(("arbitrary", "arbitrary")), name="flash_" + mode)(tab, *args)


def _sample_q_rows(q8):
    return jnp.concatenate([_arrange_q(q8, h, ATTN_SCALE) for h in range(N_HEADS)], axis=0).astype(BF16)


def _sample_out(o_all):
    return _gather_heads([o_all[h * 8:(h + 1) * 8] for h in range(N_HEADS)])


def _two_part_attention(s_past, s_new, vt_past, vt_new):
    mx = jnp.maximum(jnp.max(s_past, axis=-1, keepdims=True), jnp.max(s_new, axis=-1, keepdims=True))
    p_past = jnp.exp(s_past - mx)
    p_new = jnp.exp(s_new - mx)
    den = jnp.sum(p_past, axis=-1, keepdims=True) + jnp.sum(p_new, axis=-1, keepdims=True)
    return (_dot_nt(p_past.astype(BF16), vt_past) + _dot_nt(p_new.astype(BF16), vt_new)) / den


def _sel_sample_kernel(pt_ref, pool_ref, q_ref, newt_ref, sel_ref, e_ref, o_ref, buf, sem, kcomb,
                       *, n_pages, pos0, n_new):
    nkeys = n_pages * PAGE_SIZE
    nblk = nkeys // CMP_BLOCK
    nrow = 8 * N_HEADS

    @pl.when(pl.program_id(0) == 0)
    def _():
        kcomb[KV_WIDTH:KV_WIDTH + nblk, :] = e_ref[0:nblk, :]

    slot = _paged_prefetch(pt_ref, pool_ref, lambda s, p: buf.at[s, :, pl.ds(p * PAGE_SIZE, PAGE_SIZE)], sem, n_pages)
    qall = _sample_q_rows(q_ref[0])
    selrows = jnp.concatenate([sel_ref[0, h // GROUP] for h in range(N_HEADS)], axis=0)
    qpos = pos0 + lax.rem(lax.broadcasted_iota(jnp.int32, (nrow, 1), 0), 8)
    assert nkeys <= pos0
    bias = ((selrows[:, 0:nblk] - 1.0) * (-NEG_INF)).astype(BF16)
    kcomb[0:KV_WIDTH, :] = buf[slot, 0:KV_WIDTH, :].astype(BF16)
    s_past = jnp.dot(jnp.concatenate([qall, bias], axis=1), kcomb[...], preferred_element_type=F32)
    newt = newt_ref[0].astype(BF16)
    inew = lax.broadcasted_iota(jnp.int32, (nrow, LANES), 1) - (LANES - n_new)
    sel_new = selrows[:, nblk:nblk + 1] > 0.5
    s_new = jnp.dot(qall, newt[0:KV_WIDTH], preferred_element_type=F32)
    s_new = jnp.where(sel_new & (inew >= 0) & (pos0 + inew <= qpos), s_new, NEG_INF)
    o = _two_part_attention(s_past, s_new, buf[slot, KV_WIDTH:KV_ROW, :].astype(BF16), newt[KV_WIDTH:KV_ROW])
    o_ref[0] = _sample_out(o)


def _sel_sample(page_table, pool_t, qrot8, kvs_newt, sel, emat, pos0, n_new):
    b, n_pages = page_table.shape
    nbp = sel.shape[-1]
    nkeys = n_pages * PAGE_SIZE
    grid_spec = pltpu.PrefetchScalarGridSpec(
        num_scalar_prefetch=1, grid=(b,),
        in_specs=[pl.BlockSpec(memory_space=pl.ANY),
                  pl.BlockSpec((1, 8, ATTN_WIDTH), lambda i, pt: (i, 0, 0)),
                  pl.BlockSpec((1, KV_ROW, LANES), lambda i, pt: (i, 0, 0)),
                  pl.BlockSpec((1, N_KV_HEADS, 8, nbp), lambda i, pt: (i, 0, 0, 0)),
                  pl.BlockSpec((nbp, nkeys), lambda i, pt: (0, 0))],
        out_specs=pl.BlockSpec((1, 8, ATTN_WIDTH), lambda i, pt: (i, 0, 0)),
        scratch_shapes=[pltpu.VMEM((2, KV_ROW, nkeys), F32), pltpu.SemaphoreType.DMA((2,)),
                        pltpu.VMEM((KV_WIDTH + nkeys // CMP_BLOCK, nkeys), BF16)])
    return pl.pallas_call(
        functools.partial(_sel_sample_kernel, n_pages=n_pages, pos0=pos0, n_new=n_new),
        grid_spec=grid_spec, out_shape=_sds((b, 8, ATTN_WIDTH)),
        compiler_params=_cp(("arbitrary",)), name="sel_sample")(page_table, pool_t, qrot8, kvs_newt, sel, emat)


def _win_sample_kernel(q_ref, buf_ref, newt_ref, o_ref, nw_ref, *, wb, n_new):
    nrow = 8 * N_HEADS
    srow = lax.rem(lax.broadcasted_iota(jnp.int32, (nrow, 1), 0), 8)
    i_past = lax.broadcasted_iota(jnp.int32, (nrow, wb), 1)
    dp = wb + srow - i_past
    ok_past = (dp >= 0) & (dp < WINDOW)
    i_new = lax.broadcasted_iota(jnp.int32, (nrow, LANES), 1) - (LANES - n_new)
    dn = srow - i_new
    ok_new = (i_new >= 0) & (dn >= 0) & (dn < WINDOW)
    lane = lax.broadcasted_iota(jnp.int32, (KV_ROW, LANES), 1)
    for r in range(q_ref.shape[0]):
        qall = _sample_q_rows(q_ref[r])
        buf = buf_ref[r]
        newt = newt_ref[r]
        s_past = jnp.dot(qall, buf[0:KV_WIDTH].astype(BF16), preferred_element_type=F32)
        s_past = jnp.where(ok_past, s_past, NEG_INF)
        s_new = jnp.dot(qall, newt[0:KV_WIDTH].astype(BF16), preferred_element_type=F32)
        s_new = jnp.where(ok_new, s_new, NEG_INF)
        o = _two_part_attention(s_past, s_new, buf[KV_WIDTH:KV_ROW].astype(BF16),
                                newt[KV_WIDTH:KV_ROW].astype(BF16))
        o_ref[r] = _sample_out(o)
        nw_ref[r] = pltpu.roll(buf, wb - n_new, 1)
        nw_ref[r, :, wb - LANES:wb] = jnp.where(lane >= LANES - n_new, newt,
                                                pltpu.roll(buf[:, wb - LANES:wb], LANES - n_new, 1))


def _win_sample(qrot8, win_t, kvw_newt, n_new):
    b, _, wb = win_t.shape
    rows = WIN_SAMPLE_ROWS if b % WIN_SAMPLE_ROWS == 0 else 1
    return pl.pallas_call(
        functools.partial(_win_sample_kernel, wb=wb, n_new=n_new), grid=(b // rows,),
        in_specs=[pl.BlockSpec((rows, 8, ATTN_WIDTH), lambda i: (i, 0, 0)),
                  pl.BlockSpec((rows, KV_ROW, wb), lambda i: (i, 0, 0)),
                  pl.BlockSpec((rows, KV_ROW, LANES), lambda i: (i, 0, 0))],
        out_specs=[pl.BlockSpec((rows, 8, ATTN_WIDTH), lambda i: (i, 0, 0)),
                   pl.BlockSpec((rows, KV_ROW, wb), lambda i: (i, 0, 0))],
        out_shape=[_sds((b, 8, ATTN_WIDTH)), _sds((b, KV_ROW, wb))],
        compiler_params=_cp(("arbitrary",)), name="win_sample")(qrot8, win_t, kvw_newt)


def _post_kernel(x_ref, or_ref, oc_ref, os_ref, ow_ref, g_ref, m2_ref, m3_ref, m4_ref, m5_ref,
                 gnr_ref, gna_ref, ln2_ref, fin_ref, eg_ref, wo_ref, wg_ref, wu_ref, wd_ref, y_ref,
                 *, ff_chunks, attn_t):
    g = g_ref[0]
    g_hi = g.astype(BF16)
    g_lo = (g - g_hi.astype(F32)).astype(BF16)
    gx = jnp.dot(jnp.concatenate([g_hi, g_lo], axis=-1), eg_ref[...], preferred_element_type=F32)
    o_s, o_w = (os_ref[0].T, ow_ref[0].T) if attn_t else (os_ref[0], ow_ref[0])
    attn = (gx[:, 0:ATTN_WIDTH] * oc_ref[0] + gx[:, ATTN_WIDTH:2 * ATTN_WIDTH] * o_s
            + gx[:, 2 * ATTN_WIDTH:3 * ATTN_WIDTH] * o_w)
    mix = (_bdot(_rms(or_ref[0], gnr_ref[...]), wo_ref[0:RNN_WIDTH, :])
           + _bdot(_rms(attn, gna_ref[...]), wo_ref[RNN_WIDTH:D_MODEL, :]))
    x1 = x_ref[0] + m2_ref[0] * mix
    h = (_rms(x1, ln2_ref[...]) * (1.0 + m4_ref[0]) + m3_ref[0]).astype(BF16)
    cw = D_FF // ff_chunks
    ff = jnp.zeros(x1.shape, F32)
    for c in range(ff_chunks):
        gate = jnp.dot(h, wg_ref[:, c * cw:(c + 1) * cw], preferred_element_type=F32)
        up = jnp.dot(h, wu_ref[:, c * cw:(c + 1) * cw], preferred_element_type=F32)
        act = gate * jax.nn.sigmoid(gate) * up
        ff = ff + _bdot(act, wd_ref[c * cw:(c + 1) * cw, :])
    y = x1 + m5_ref[0] * ff
    y_ref[0] = _rms(y, fin_ref[...])


def _post(x3, o_r, o_c, o_s, o_w, g, mods, gnr, gna, ln2, fin, egate, wo, wg, wu, wd, tm, attn_t, ff_chunks=2):
    bx, sx, _ = x3.shape
    sm = mods[0].shape[1]
    if sm == 1:
        mod_spec = pl.BlockSpec((1, 1, D_MODEL), lambda b, t: (b, 0, 0))
    else:
        mod_spec = pl.BlockSpec((1, tm, D_MODEL), lambda b, t: (b, t, 0))
    tile = lambda w: pl.BlockSpec((1, tm, w), lambda b, t: (b, t, 0))
    attn_tile = pl.BlockSpec((1, ATTN_WIDTH, tm), lambda b, t: (b, 0, t)) if attn_t else tile(ATTN_WIDTH)

    def const(shape):
        return pl.BlockSpec(shape, lambda b, t: (0,) * len(shape), pipeline_mode=pl.Buffered(1))

    return pl.pallas_call(
        functools.partial(_post_kernel, ff_chunks=ff_chunks, attn_t=attn_t), grid=(bx, sx // tm),
        in_specs=[tile(D_MODEL), tile(RNN_WIDTH), tile(ATTN_WIDTH), attn_tile, attn_tile, tile(LANES),
                  mod_spec, mod_spec, mod_spec, mod_spec,
                  const((1, RNN_WIDTH)), const((1, ATTN_WIDTH)), const((1, D_MODEL)), const((1, D_MODEL)),
                  const((2 * LANES, 3 * ATTN_WIDTH)), const((D_MODEL, D_MODEL)),
                  const((D_MODEL, D_FF)), const((D_MODEL, D_FF)), const((D_FF, D_MODEL))],
        out_specs=tile(D_MODEL), out_shape=_sds((bx, sx, D_MODEL)),
        compiler_params=_cp(("arbitrary", "arbitrary")), name="post")(
            x3, o_r, o_c, o_s, o_w, g, mods[2], mods[3], mods[4], mods[5], gnr, gna, ln2, fin, egate, wo, wg, wu, wd)


def _rope_tables(pos):
    inv = jnp.exp(jnp.arange(ROT_HALF, dtype=F32) * (-math.log(ROPE_THETA) / ROT_HALF))
    ang = pos.astype(F32)[:, None] * inv[None, :]
    cos, sin = jnp.cos(ang), jnp.sin(ang)
    n = pos.shape[0]
    one = jnp.ones((n, HEAD_DIM - ROT_DIM), F32)
    zero = jnp.zeros((n, HEAD_DIM - ROT_DIM), F32)
    z8 = jnp.zeros((n, ROT_HALF), F32)
    c = jnp.concatenate([cos, cos, one], axis=-1)
    s1 = jnp.concatenate([-sin, z8, zero], axis=-1)
    s2 = jnp.concatenate([z8, sin, zero], axis=-1)
    return tuple(jnp.concatenate([t, t], axis=-1) for t in (c, s1, s2))


def _blockdiag(w):
    nb, d, e = w.shape
    return jnp.einsum('nde,nm->ndme', w, jnp.eye(nb, dtype=w.dtype)).reshape(nb * d, nb * e)


def _prep_params(w_in, lru_wa, lru_wx, lru_ba, lru_bx, cmp_pe, cmp_w1, cmp_w2):
    w_in_p = jnp.pad(w_in, ((0, 0), (0, IN_COLS_PAD - w_in.shape[1]))).astype(BF16)
    wax = jnp.concatenate([_blockdiag(lru_wa), _blockdiag(lru_wx)], axis=1).astype(BF16)
    bax = jnp.concatenate([lru_ba, lru_bx])[None, :]
    pe_row = jnp.broadcast_to(jnp.transpose(cmp_pe, (1, 0, 2))[:, :, None, :],
                              (CMP_BLOCK, 2, N_KV_HEADS, HEAD_DIM)).reshape(CMP_BLOCK, KV_ROW)
    w2bd = _blockdiag(jnp.repeat(cmp_w2, N_KV_HEADS, axis=0)).astype(BF16)
    e = np.zeros((LANES, 3 * ATTN_WIDTH), np.float32)
    for br in range(3):
        for h in range(N_HEADS):
            e[br * N_HEADS + h, br * ATTN_WIDTH + h * HEAD_DIM:br * ATTN_WIDTH + (h + 1) * HEAD_DIM] = 1.0
    egate = jnp.asarray(np.concatenate([e, e], axis=0), BF16)
    return w_in_p, wax, bax, pe_row, cmp_w1.astype(BF16), w2bd, egate


def _expand_matrix(nbp, n_tiles, tk):
    blk = (np.arange(n_tiles)[:, None, None] * tk + np.arange(tk)[None, None, :]) // CMP_BLOCK
    return jnp.asarray(blk == np.arange(nbp)[None, :, None], BF16)


def _kv_from_t(kv_t):
    n, _, s = kv_t.shape
    return jnp.transpose(kv_t.reshape(n, 2, N_KV_HEADS, HEAD_DIM, s), (0, 4, 1, 2, 3))


def _kv_to_t(kv):
    n, s = kv.shape[:2]
    return jnp.transpose(kv, (0, 2, 3, 4, 1)).reshape(n, KV_ROW, s)


def _layer_prompt(x, mods, P, tm=512, tq=256):
    b, s, _ = x.shape
    cos, s1, s2 = _rope_tables(jnp.arange(s))
    xr, yr, q, _, kvc, ks, kw, kvc_t, kvs_t, kvw_t, g, q_t = _pre(
        x, mods[0], mods[1], P['ln1'], P['w_in'], cos, s1, s2, tm)
    o_r, h_new = _rglru_prompt(xr, yr, P['conv_w'], P['conv_b'], P['wax'], P['bax'], P['lam'])
    nb = s // CMP_BLOCK
    ckv = _compress_prompt(kvc, P['pe_row'], P['w1'], P['w2bd'])
    nbp = LANES * (-(-nb // LANES))
    o_c, sel_t = _cmp_select_t(q, ckv, nbp, 0, min(256, s))
    o_s_t = _flash_prompt(q_t, ks, kvs_t, sel_t, "sel", tq, min(512, s))
    o_w_t = _flash_prompt(q_t, kw, kvw_t, None, "win", tq, tq)
    y = _post(x, o_r, o_c, o_s_t, o_w_t, g, mods, P['gn_rnn'], P['gn_attn'], P['ln2'], P['final_g'], P['egate'],
              P['w_out'], P['w_gate'], P['w_up'], P['w_down'], tm, True)
    wlen = min(WINDOW, s)
    return y, (_kv_from_t(kvc_t), _kv_from_t(kvs_t), _kv_from_t(kvw_t[:, :, s - wlen:]),
               xr[:, s - (CONV_W - 1):], h_new[:, 0])


def _pad_rows8(a):
    return jnp.pad(a, ((0, 0), (0, 8 - a.shape[1]), (0, 0)))


def _layer_sample(x, mods_tok, cmp_pool, sel_pool, win_buf, conv_buf, h0, page_table, P):
    b, s, _ = x.shape
    n_pages = page_table.shape[1]
    past = n_pages * PAGE_SIZE
    nb = past // CMP_BLOCK + 1
    pos = past + jnp.arange(s)
    cos, s1, s2 = (jnp.tile(t, (b, 1)) for t in _rope_tables(pos))
    n = b * s
    flat = lambda a: a.reshape(1, n, a.shape[-1])
    outs = _pre(flat(x), mods_tok[0], mods_tok[1], P['ln1'], P['w_in'], cos, s1, s2, min(512, n))
    xr, yr, q, qrot, kvc = (o.reshape(b, s, o.shape[-1]) for o in outs[:5])
    kvc_t, kvs_t, kvw_t = (jnp.transpose(o[0].reshape(KV_ROW, b, s), (1, 0, 2)) for o in outs[7:10])
    lane_pad = lambda a: jnp.pad(a, ((0, 0), (0, 0), (LANES - s, 0)))
    g = outs[10]
    tmaj = lambda a: jnp.transpose(a, (1, 0, 2))
    o_r_t, h_new = _rglru_sample(tmaj(xr), tmaj(yr), tmaj(conv_buf), h0, P['conv_w'], P['conv_b'],
                                 P['wax'], P['bax'], P['lam'])
    o_r = tmaj(o_r_t)
    nbp = LANES * (-(-nb // LANES))
    ckv_p = _compress_sample(page_table, _kv_to_t(cmp_pool), _pad_rows8(kvc), P['pe_row'], P['w1'], P['w2bd'], nbp)
    q8, qrot8 = _pad_rows8(q), _pad_rows8(qrot)
    o_c8, sel = _cmp_select_sample(q8, ckv_p, nb, past)
    o_s8 = _sel_sample(page_table, _kv_to_t(sel_pool), qrot8, lane_pad(kvs_t), sel,
                       _expand_matrix(nbp, 1, past)[0], past, s)
    o_w8, new_win_t = _win_sample(qrot8, _kv_to_t(win_buf), lane_pad(kvw_t), s)
    y = _post(flat(x), flat(o_r), flat(o_c8[:, :s]), flat(o_s8[:, :s]), flat(o_w8[:, :s]), g, mods_tok,
              P['gn_rnn'], P['gn_attn'], P['ln2'], P['final_g'], P['egate'],
              P['w_out'], P['w_gate'], P['w_up'], P['w_down'], min(512, n), False)
    return y.reshape(b, s, D_MODEL), (_kv_from_t(kvc_t), _kv_from_t(kvs_t), _kv_from_t(new_win_t),
                                      xr[:, s - (CONV_W - 1):], h_new)


def kernel(x_prompt, x_sample, cache_cmp_kv, cache_sel_kv, state_win_kv, state_conv, state_lru_h, page_table,
           c_prompt, c_sample, ln1_g, ln2_g, w_ada, b_ada, w_in, conv_w, conv_b, lru_wa, lru_ba, lru_wx, lru_bx,
           lru_lambda, cmp_pe, cmp_w1, cmp_w2, gn_rnn, gn_attn, w_out, w_gate, w_up, w_down, final_g):
    depth = w_in.shape[0]
    assert depth == 1
    bp = x_prompt.shape[0]
    bs, ss, _ = x_sample.shape
    l = 0
    w_in_p, wax, bax, pe_row, w1, w2bd, egate = _prep_params(
        w_in[l], lru_wa[l], lru_wx[l], lru_ba[l], lru_bx[l], cmp_pe[l], cmp_w1[l], cmp_w2[l])
    P = {'ln1': ln1_g[l][None], 'ln2': ln2_g[l][None], 'w_in': w_in_p, 'conv_w': conv_w[l], 'conv_b': conv_b[l][None],
         'wax': wax, 'bax': bax, 'lam': lru_lambda[l][None], 'pe_row': pe_row, 'w1': w1, 'w2bd': w2bd,
         'gn_rnn': gn_rnn[l][None], 'gn_attn': gn_attn[l][None], 'egate': egate, 'final_g': final_g[None],
         'w_out': w_out[l].astype(BF16), 'w_gate': w_gate[l].astype(BF16), 'w_up': w_up[l].astype(BF16),
         'w_down': w_down[l].astype(BF16)}
    mods_all = _ada(jnp.concatenate([c_prompt, c_sample], axis=0), w_ada[l].astype(BF16), b_ada[l][None])
    mods_p = [m[:, None, :] for m in jnp.split(mods_all[:bp], 6, axis=-1)]
    mods_s = [jnp.repeat(m, ss, axis=0)[None] for m in jnp.split(mods_all[bp:], 6, axis=-1)]

    yp, st_p = _layer_prompt(x_prompt, mods_p, P)
    ys, st_s = _layer_sample(x_sample, mods_s, cache_cmp_kv[l], cache_sel_kv[l], state_win_kv[l],
                             state_conv[l], state_lru_h[l], page_table, P)

    return (yp, ys, st_p[0][None], st_s[0][None], st_p[1][None], st_s[1][None], st_p[2][None], st_s[2][None],
            st_p[3][None], st_s[3][None], st_p[4][None], st_s[4][None])
```

```python
import functools
import math

import numpy as np
import jax
import jax.numpy as jnp
from jax import lax
from jax.experimental import pallas as pl
from jax.experimental.pallas import tpu as pltpu

F32 = jnp.float32
BF16 = jnp.bfloat16

D_MODEL = 1024
HEAD_DIM = 64
N_HEADS = 8
N_KV_HEADS = 2
GROUP = N_HEADS // N_KV_HEADS
ATTN_WIDTH = N_HEADS * HEAD_DIM
KV_WIDTH = N_KV_HEADS * HEAD_DIM
KV_ROW = 2 * KV_WIDTH
CMP_BLOCK = 64
CMP_HIDDEN = 128
N_SEL = 16
WINDOW = 512
ROT_DIM = HEAD_DIM // 4
ROT_HALF = ROT_DIM // 2
ROPE_THETA = 500000.0
RNN_WIDTH = D_MODEL - ATTN_WIDTH
CONV_W = 4
LRU_C = 8.0
D_FF = 2816
PAGE_SIZE = 128
BLOCKS_PER_PAGE = PAGE_SIZE // CMP_BLOCK
ROW_PITCH = CMP_BLOCK + 8
LSTEP = 32
SCORE_LOOKAHEAD = {"sel": 4, "win": 8}
WIN_SAMPLE_ROWS = 8
SAMPLE_ROWS = 16
ATTN_SCALE = HEAD_DIM ** -0.5
LOG2_E = math.log2(math.e)
EPS = 1e-6
NEG_INF = -1e30
M_INIT = -1e29
FORCE_SCORE = 1e4
LANES = 128
IN_COLS_PAD = 19 * LANES
GATE_COL = 2 * RNN_WIDTH + ATTN_WIDTH + 3 * KV_ROW
VMEM_LIMIT = 56 * 1024 * 1024


def _cp(sem, vmem=VMEM_LIMIT):
    return pltpu.CompilerParams(dimension_semantics=sem, vmem_limit_bytes=vmem)


def _sds(shape, dt=F32):
    return jax.ShapeDtypeStruct(shape, dt)


def _rms(x, g):
    return x * lax.rsqrt(jnp.mean(x * x, axis=-1, keepdims=True) + EPS) * g


def _bdot(a, b):
    return jnp.dot(a.astype(BF16), b, preferred_element_type=F32)


def _dot_nt(a, b, precision=None):
    return lax.dot_general(a, b, (((1,), (1,)), ((), ())), precision=precision, preferred_element_type=F32)


def _ada_kernel(c_ref, w_ref, b_ref, o_ref):
    c = c_ref[...]
    o_ref[...] = _bdot(c * jax.nn.sigmoid(c), w_ref[...]) + b_ref[...]


def _ada(c_all, w_bf, b):
    n = c_all.shape[0]
    return pl.pallas_call(
        _ada_kernel, grid=(6,),
        in_specs=[pl.BlockSpec((n, D_MODEL), lambda j: (0, 0)),
                  pl.BlockSpec((D_MODEL, D_MODEL), lambda j: (0, j)),
                  pl.BlockSpec((1, D_MODEL), lambda j: (0, j))],
        out_specs=pl.BlockSpec((n, D_MODEL), lambda j: (0, j)),
        out_shape=_sds((n, 6 * D_MODEL)), compiler_params=_cp(("arbitrary",)), name="ada")(c_all, w_bf, b)


def _rope(v, cos, s1, s2):
    w = v.shape[-1]
    return v * cos + pltpu.roll(v, w - ROT_HALF, 1) * s1 + pltpu.roll(v, ROT_HALF, 1) * s2


def _pre_kernel(x_ref, shift_ref, scale_ref, ln_ref, w_ref, cos_ref, s1_ref, s2_ref,
                xr_ref, yr_ref, q_ref, qrot_ref, kvc_ref, ks_ref, kw_ref, kvct_ref, kvst_ref, kvwt_ref, g_ref, qt_ref):
    x = x_ref[0]
    h = _rms(x, ln_ref[...]) * (1.0 + scale_ref[0]) + shift_ref[0]
    z = _bdot(h, w_ref[...])
    xr_ref[0] = z[:, 0:RNN_WIDTH]
    yr_ref[0] = z[:, RNN_WIDTH:2 * RNN_WIDTH]
    q = z[:, 2 * RNN_WIDTH:2 * RNN_WIDTH + ATTN_WIDTH]
    q_ref[0] = q
    cos, s1, s2 = cos_ref[...], s1_ref[...], s2_ref[...]
    rep = ATTN_WIDTH // LANES
    qrot = _rope(q, jnp.concatenate([cos] * rep, -1), jnp.concatenate([s1] * rep, -1),
                 jnp.concatenate([s2] * rep, -1))
    qrot_ref[0] = qrot
    qt_ref[0] = (qrot * (ATTN_SCALE * LOG2_E)).T.astype(BF16)
    c0 = 2 * RNN_WIDTH + ATTN_WIDTH
    kvc = z[:, c0:c0 + KV_ROW]
    kvc_ref[0] = kvc
    kvct_ref[0] = kvc.T
    tm = x.shape[0]
    for i, kref, ref in ((1, ks_ref, kvst_ref), (2, kw_ref, kvwt_ref)):
        kv = z[:, c0 + i * KV_ROW:c0 + (i + 1) * KV_ROW]
        keys = _rope(kv[:, :KV_WIDTH], cos, s1, s2)
        if i == 1:
            blk = (pl.program_id(1) * tm + lax.broadcasted_iota(jnp.int32, (tm, HEAD_DIM), 0)) // CMP_BLOCK
            onehot = jnp.where(lax.broadcasted_iota(jnp.int32, (tm, HEAD_DIM), 1) == blk, 1.0, 0.0)
            kref[0] = jnp.concatenate([keys[:, :HEAD_DIM], onehot, keys[:, HEAD_DIM:], onehot], axis=-1)
        else:
            kref[0] = keys
        ref[0, 0:KV_WIDTH, :] = keys.T
        ref[0, KV_WIDTH:KV_ROW, :] = kv[:, KV_WIDTH:].T
    g_ref[0] = jax.nn.sigmoid(z[:, GATE_COL:GATE_COL + LANES])


def _pre(x3, shift, scale, ln, w_in_bf, cos, s1, s2, tm):
    bx, sx, _ = x3.shape
    sm = shift.shape[1]
    if sm == 1:
        mod_spec = pl.BlockSpec((1, 1, D_MODEL), lambda b, t: (b, 0, 0))
    else:
        mod_spec = pl.BlockSpec((1, tm, D_MODEL), lambda b, t: (b, t, 0))
    tab = pl.BlockSpec((tm, LANES), lambda b, t: (t, 0))
    row = lambda w: (pl.BlockSpec((1, tm, w), lambda b, t: (b, t, 0)), _sds((bx, sx, w)))
    col = (pl.BlockSpec((1, KV_ROW, tm), lambda b, t: (b, 0, t)), _sds((bx, KV_ROW, sx)))
    qt = (pl.BlockSpec((1, ATTN_WIDTH, tm), lambda b, t: (b, 0, t)), _sds((bx, ATTN_WIDTH, sx), BF16))
    outs = [row(RNN_WIDTH), row(RNN_WIDTH), row(ATTN_WIDTH), row(ATTN_WIDTH), row(KV_ROW), row(2 * KV_WIDTH),
            row(KV_WIDTH), col, col, col, row(LANES), qt]
    return pl.pallas_call(
        _pre_kernel, grid=(bx, sx // tm),
        in_specs=[pl.BlockSpec((1, tm, D_MODEL), lambda b, t: (b, t, 0)), mod_spec, mod_spec,
                  pl.BlockSpec((1, D_MODEL), lambda b, t: (0, 0)),
                  pl.BlockSpec((D_MODEL, IN_COLS_PAD), lambda b, t: (0, 0)), tab, tab, tab],
        out_specs=[o[0] for o in outs], out_shape=[o[1] for o in outs],
        compiler_params=_cp(("arbitrary", "arbitrary")), name="pre")(x3, shift, scale, ln, w_in_bf, cos, s1, s2)


def _lru_gates(xc, wax_ref, bax_ref, lam_ref):
    ra = _bdot(xc, wax_ref[...]) + bax_ref[...]
    r = jax.nn.sigmoid(ra[:, :RNN_WIDTH])
    i = jax.nn.sigmoid(ra[:, RNN_WIDTH:])
    nl = -lam_ref[...]
    softplus = jnp.maximum(nl, 0.0) + jnp.log(1.0 + jnp.exp(-jnp.abs(nl)))
    a = jnp.exp(-LRU_C * r * softplus)
    v = 1.0 - a * a
    return a, jnp.where(v > 0.0, v * lax.rsqrt(v), 0.0) * (i * xc)


def _scan_rows(a, b):
    t, c = a.shape
    row = lax.broadcasted_iota(jnp.int32, a.shape, 0)
    k = 1
    while k < t:
        if k < 8:
            a_s = jnp.where(row >= k, pltpu.roll(a, k, 0), 1.0)
            b_s = jnp.where(row >= k, pltpu.roll(b, k, 0), 0.0)
        else:
            a_s = jnp.concatenate([jnp.ones((k, c), F32), a[:t - k]], axis=0)
            b_s = jnp.concatenate([jnp.zeros((k, c), F32), b[:t - k]], axis=0)
        b = a * b_s + b
        a = a * a_s
        k *= 2
    return a, b


def _rglru_kernel(xr_ref, yr_ref, cw_ref, cb_ref, wax_ref, bax_ref, lam_ref, o_ref, hl_ref, tail, hc, *, t):
    @pl.when(pl.program_id(1) == 0)
    def _():
        tail[...] = jnp.zeros_like(tail)
        hc[...] = jnp.zeros_like(hc)

    x = xr_ref[0]
    prev = tail[...]
    row8 = lax.broadcasted_iota(jnp.int32, (8, RNN_WIDTH), 0)
    xc = cb_ref[...] + x * cw_ref[CONV_W - 1:CONV_W, :]
    for j in range(1, CONV_W):
        rolled = pltpu.roll(x, j, 0)
        head = jnp.where(row8 < j, pltpu.roll(prev, j, 0), rolled[0:8])
        xc = xc + jnp.concatenate([head, rolled[8:]], axis=0) * cw_ref[CONV_W - 1 - j:CONV_W - j, :]
    tail[...] = x[t - 8:t]
    a, b = _lru_gates(xc, wax_ref, bax_ref, lam_ref)
    ac, bc = _scan_rows(a, b)
    h = ac * hc[...] + bc
    hc[...] = h[t - 1:t, :]
    hl_ref[0] = h[t - 1:t, :]
    o_ref[0] = h * jax.nn.gelu(yr_ref[0])


def _rglru_prompt(xr, yr, cw, cb, wax, bax, lam, t=256):
    b, s, _ = xr.shape
    tile = pl.BlockSpec((1, t, RNN_WIDTH), lambda i, j: (i, j, 0))
    full = lambda shp: pl.BlockSpec(shp, lambda i, j: (0,) * len(shp))
    return pl.pallas_call(
        functools.partial(_rglru_kernel, t=t), grid=(b, s // t),
        in_specs=[tile, tile, full((CONV_W, RNN_WIDTH)), full((1, RNN_WIDTH)),
                  full((RNN_WIDTH, 2 * RNN_WIDTH)), full((1, 2 * RNN_WIDTH)), full((1, RNN_WIDTH))],
        out_specs=[tile, pl.BlockSpec((1, 1, RNN_WIDTH), lambda i, j: (i, 0, 0))],
        out_shape=[_sds((b, s, RNN_WIDTH)), _sds((b, 1, RNN_WIDTH))],
        scratch_shapes=[pltpu.VMEM((8, RNN_WIDTH), F32), pltpu.VMEM((1, RNN_WIDTH), F32)],
        compiler_params=_cp(("arbitrary", "arbitrary")), name="rglru_prompt")(xr, yr, cw, cb, wax, bax, lam)


def _rglru_sample_kernel(xr_ref, yr_ref, cp_ref, h0_ref, cw_ref, cb_ref, wax_ref, bax_ref, lam_ref,
                         o_ref, hl_ref, *, s, nb):
    xpad = [cp_ref[i] for i in range(CONV_W - 1)] + [xr_ref[i] for i in range(s)]
    xcs = []
    for j in range(s):
        xc = cb_ref[...]
        for k in range(CONV_W):
            xc = xc + xpad[j + k] * cw_ref[k:k + 1, :]
        xcs.append(xc)
    a, b = _lru_gates(jnp.concatenate(xcs, axis=0), wax_ref, bax_ref, lam_ref)
    h = h0_ref[...]
    for j in range(s):
        h = a[j * nb:(j + 1) * nb] * h + b[j * nb:(j + 1) * nb]
        o_ref[j] = h * jax.nn.gelu(yr_ref[j])
    hl_ref[...] = h


def _rglru_sample(xr_t, yr_t, cp_t, h0, cw, cb, wax, bax, lam):
    s, nb, _ = xr_t.shape
    return pl.pallas_call(
        functools.partial(_rglru_sample_kernel, s=s, nb=nb),
        out_shape=[_sds((s, nb, RNN_WIDTH)), _sds((nb, RNN_WIDTH))],
        compiler_params=pltpu.CompilerParams(vmem_limit_bytes=VMEM_LIMIT), name="rglru_sample")(
            xr_t, yr_t, cp_t, h0, cw, cb, wax, bax, lam)


def _build_w1bd(w1_ref, w1bd):
    w1bd[...] = jnp.zeros_like(w1bd)
    for c in range(2):
        for k in range(N_KV_HEADS):
            r0 = c * KV_WIDTH + k * HEAD_DIM
            f0 = (c * N_KV_HEADS + k) * CMP_HIDDEN
            w1bd[:, r0:r0 + HEAD_DIM, f0:f0 + CMP_HIDDEN] = w1_ref[c]


def _compress_blocks(get_rows, m, pe_ref, w1bd, w2_ref, acc):
    acc[...] = jnp.zeros_like(acc)
    step = 4

    def body(i, carry):
        l0 = pl.multiple_of(i * step, step)
        z = jnp.concatenate([get_rows(l0 + j) + pe_ref[pl.ds(l0 + j, 1), :] for j in range(step)], axis=1)
        acc[...] += _bdot(z, w1bd[pl.ds(l0, step)].reshape(step * KV_ROW, 4 * CMP_HIDDEN))
        return carry

    lax.fori_loop(0, CMP_BLOCK // step, body, 0)
    return _bdot(jax.nn.gelu(acc[...]), w2_ref[...])


def _compress_prompt_kernel(kv_ref, pe_ref, w1_ref, w2_ref, o_ref, w1bd, acc, *, nblk):
    @pl.when(pl.program_id(0) == 0)
    def _():
        _build_w1bd(w1_ref, w1bd)

    o_ref[0] = _compress_blocks(lambda l: kv_ref[0, :, l, :], nblk, pe_ref, w1bd, w2_ref, acc)


def _compress_prompt(kvc, pe_row, w1_bf, w2bd_bf):
    b, s, _ = kvc.shape
    nblk = s // CMP_BLOCK
    kv4 = kvc.reshape(b, nblk, CMP_BLOCK, KV_ROW)
    return pl.pallas_call(
        functools.partial(_compress_prompt_kernel, nblk=nblk), grid=(b,),
        in_specs=[pl.BlockSpec((1, nblk, CMP_BLOCK, KV_ROW), lambda i: (i, 0, 0, 0)),
                  pl.BlockSpec((CMP_BLOCK, KV_ROW), lambda i: (0, 0)),
                  pl.BlockSpec((2, CMP_BLOCK, HEAD_DIM, CMP_HIDDEN), lambda i: (0, 0, 0, 0)),
                  pl.BlockSpec((4 * CMP_HIDDEN, KV_ROW), lambda i: (0, 0))],
        out_specs=pl.BlockSpec((1, nblk, KV_ROW), lambda i: (i, 0, 0)),
        out_shape=_sds((b, nblk, KV_ROW)),
        scratch_shapes=[pltpu.VMEM((CMP_BLOCK, KV_ROW, 4 * CMP_HIDDEN), BF16),
                        pltpu.VMEM((nblk, 4 * CMP_HIDDEN), F32)],
        compiler_params=_cp(("arbitrary",)), name="compress_prompt")(kv4, pe_row, w1_bf, w2bd_bf)


def _page_copies(pt_ref, pool_ref, dst, sem, b, slot, n_pages):
    return [pltpu.make_async_copy(pool_ref.at[pt_ref[b, p]], dst(slot, p), sem.at[slot]) for p in range(n_pages)]


def _paged_prefetch(pt_ref, pool_ref, dst, sem, n_pages):
    b = pl.program_id(0)
    nb = pl.num_programs(0)
    slot = lax.rem(b, 2)

    @pl.when(b == 0)
    def _():
        for cp in _page_copies(pt_ref, pool_ref, dst, sem, 0, 0, n_pages):
            cp.start()

    @pl.when(b + 1 < nb)
    def _():
        for cp in _page_copies(pt_ref, pool_ref, dst, sem, b + 1, 1 - slot, n_pages):
            cp.start()

    for cp in _page_copies(pt_ref, pool_ref, dst, sem, b, slot, n_pages):
        cp.wait()
    return slot


def _compress_sample_kernel(pt_ref, pool_ref, new_ref, pe_ref, w1_ref, w2_ref, o_ref,
                            stage, sem, rows, w1s, acc, *, n_pages, m, nbp, n_rows):
    nblk = n_pages * BLOCKS_PER_PAGE
    n_iter = CMP_BLOCK // LSTEP
    assert n_pages % n_iter == 0
    pages_per_iter = n_pages // n_iter
    b = pl.program_id(0)
    cur = lax.rem(b, 2)
    nxt = 1 - cur

    def copies(row, slot):
        return _page_copies(pt_ref, pool_ref, lambda s, p: stage.at[s, p], sem, row, slot, n_pages)

    def half_to_rows(slot, p, c):
        r0 = pl.multiple_of(p * (BLOCKS_PER_PAGE * ROW_PITCH), 8)
        t = stage[slot, p, c * KV_WIDTH:(c + 1) * KV_WIDTH, :].T
        for j in range(BLOCKS_PER_PAGE):
            rows[slot, c, pl.ds(r0 + j * ROW_PITCH, CMP_BLOCK), :] = t[j * CMP_BLOCK:(j + 1) * CMP_BLOCK]

    def to_rows(slot, p):
        for c in range(2):
            half_to_rows(slot, p, c)

    @pl.when(b == 0)
    def _():
        w1s[...] = jnp.zeros_like(w1s)
        for c in range(2):
            for k in range(N_KV_HEADS):
                w1s[c, :, k * HEAD_DIM:(k + 1) * HEAD_DIM, k * CMP_HIDDEN:(k + 1) * CMP_HIDDEN] = w1_ref[c]
            for s in range(2):
                rows[s, c, nblk * ROW_PITCH:m * ROW_PITCH, :] = jnp.zeros(((m - nblk) * ROW_PITCH, KV_WIDTH), F32)
        for s in range(min(2, n_rows)):
            for cp in copies(s, s):
                cp.start()
        for cp in copies(0, 0):
            cp.wait()

        def first(p, carry):
            to_rows(0, p)
            return carry

        lax.fori_loop(0, n_pages, first, 0, unroll=4)

    @pl.when(b + 1 < n_rows)
    def _():
        for cp in copies(b + 1, nxt):
            cp.wait()

    @pl.when(b + 2 < n_rows)
    def _():
        for cp in copies(b + 2, cur):
            cp.start()

    for c in range(2):
        rows[cur, c, nblk * ROW_PITCH:nblk * ROW_PITCH + 8, :] = new_ref[0, :, c * KV_WIDTH:(c + 1) * KV_WIDTH]
    acc[...] = jnp.zeros_like(acc)

    def body(i, carry):
        l0 = pl.multiple_of(i * LSTEP, LSTEP)
        pe = pe_ref[pl.ds(l0, LSTEP), :]
        halves = [(pp, c) for pp in range(pages_per_iter) for c in range(2)] if n_rows > 1 else []
        per_gather = -(-len(halves) // LSTEP)
        prods = []
        for c in range(2):
            parts = []
            for j in range(LSTEP):
                parts.append(rows[cur, c, pl.ds(l0 + j, m, stride=ROW_PITCH), :]
                             + pe[j:j + 1, c * KV_WIDTH:(c + 1) * KV_WIDTH])
                if c == 1:
                    for pp, hc in halves[:per_gather]:
                        half_to_rows(nxt, i * pages_per_iter + pp, hc)
                    halves = halves[per_gather:]
            w = w1s[c, pl.ds(l0, LSTEP)].reshape(LSTEP * KV_WIDTH, N_KV_HEADS * CMP_HIDDEN)
            prods.append(_bdot(jnp.concatenate(parts, axis=1), w))
        for c in range(2):
            acc[c] += prods[c]
        return carry

    lax.fori_loop(0, n_iter, body, 0)
    hw = N_KV_HEADS * CMP_HIDDEN
    for c in range(2):
        o_ref[0, 0:m, c * KV_WIDTH:(c + 1) * KV_WIDTH] = _bdot(
            jax.nn.gelu(acc[c]), w2_ref[c * hw:(c + 1) * hw, c * KV_WIDTH:(c + 1) * KV_WIDTH])
    o_ref[0, m:nbp, :] = jnp.zeros((nbp - m, KV_ROW), F32)


def _compress_sample(page_table, pool_t, kvc_new8, pe_row, w1_bf, w2bd_bf, nbp):
    b, n_pages = page_table.shape
    m = n_pages * BLOCKS_PER_PAGE + 8
    grid_spec = pltpu.PrefetchScalarGridSpec(
        num_scalar_prefetch=1, grid=(b,),
        in_specs=[pl.BlockSpec(memory_space=pl.ANY),
                  pl.BlockSpec((1, 8, KV_ROW), lambda i, pt: (i, 0, 0)),
                  pl.BlockSpec((CMP_BLOCK, KV_ROW), lambda i, pt: (0, 0)),
                  pl.BlockSpec((2, CMP_BLOCK, HEAD_DIM, CMP_HIDDEN), lambda i, pt: (0, 0, 0, 0)),
                  pl.BlockSpec((4 * CMP_HIDDEN, KV_ROW), lambda i, pt: (0, 0))],
        out_specs=pl.BlockSpec((1, nbp, KV_ROW), lambda i, pt: (i, 0, 0)),
        scratch_shapes=[pltpu.VMEM((2, n_pages, KV_ROW, PAGE_SIZE), F32), pltpu.SemaphoreType.DMA((2,)),
                        pltpu.VMEM((2, 2, m * ROW_PITCH, KV_WIDTH), F32),
                        pltpu.VMEM((2, CMP_BLOCK, KV_WIDTH, N_KV_HEADS * CMP_HIDDEN), BF16),
                        pltpu.VMEM((2, m, N_KV_HEADS * CMP_HIDDEN), F32)])
    return pl.pallas_call(
        functools.partial(_compress_sample_kernel, n_pages=n_pages, m=m, nbp=nbp, n_rows=b),
        grid_spec=grid_spec, out_shape=_sds((b, nbp, KV_ROW)),
        compiler_params=_cp(("arbitrary",)), name="compress_sample")(page_table, pool_t, kvc_new8, pe_row, w1_bf, w2bd_bf)


def _head_lanes(q, h, dst, scale):
    src = q[:, (h // 2) * LANES:(h // 2 + 1) * LANES]
    if (h % 2) != dst:
        src = pltpu.roll(src, HEAD_DIM, 1)
    lane = lax.broadcasted_iota(jnp.int32, src.shape, 1)
    keep = (lane >= dst * HEAD_DIM) & (lane < (dst + 1) * HEAD_DIM)
    return jnp.where(keep, src * scale, 0.0)


def _arrange_q(q, h, scale):
    return _head_lanes(q, h, h // GROUP, scale)


def _gather_heads(o_list):
    parts = []
    for h, o in enumerate(o_list):
        kvh = h // GROUP
        parts.append(o[:, kvh * HEAD_DIM:(kvh + 1) * HEAD_DIM])
    return jnp.concatenate(parts, axis=-1)


def _select_mask_t(imp, qpos, nb):
    j = lax.broadcasted_iota(jnp.int32, imp.shape, 0)
    cur = qpos // CMP_BLOCK
    forced = (j == 0) | (j == cur) | (j == cur - 1)
    score = jnp.where(j > cur, -FORCE_SCORE, jnp.where(forced, FORCE_SCORE, imp))
    if imp.shape[0] > nb:
        score = jnp.where(j < nb, score, -3e38)
    sub = 8
    ngrp = imp.shape[0] // sub
    groups = [score[r * sub:(r + 1) * sub] for r in range(ngrp)]
    jsub = lax.broadcasted_iota(jnp.int32, groups[0].shape, 0)
    ranks = [jnp.zeros(groups[0].shape, F32) for _ in range(ngrp)]
    for i in range(nb):
        row = score[i:i + 1]
        for r in range(ngrp):
            if r * sub > i:
                hit = row >= groups[r]
            elif (r + 1) * sub - 1 <= i:
                hit = row > groups[r]
            else:
                hit = ((jsub + r * sub > i) & (row >= groups[r])) | (row > groups[r])
            ranks[r] = ranks[r] + jnp.where(hit, 1.0, 0.0)
    rank = jnp.concatenate(ranks, axis=0)
    return jnp.where((rank < float(min(N_SEL, nb))) & (j <= cur) & (j < nb), 1.0, 0.0)


def _cmp_kernel_t(q_ref, ckv_ref, oc_ref, selt_ref, *, tq, nb, nbp, pos0):
    qpos = pos0 + pl.program_id(1) * tq + lax.broadcasted_iota(jnp.int32, (nb, tq), 1)
    j = lax.broadcasted_iota(jnp.int32, (nb, tq), 0)
    m = (j + 1) * CMP_BLOCK - 1 <= qpos
    q = q_ref[0]
    ck = ckv_ref[0, :, 0:KV_WIDTH]
    cv = ckv_ref[0, :, KV_WIDTH:KV_ROW].astype(BF16)
    m4 = jnp.concatenate([m] * GROUP, axis=1)
    outs = []
    for kvh in range(N_KV_HEADS):
        qa = jnp.concatenate([_arrange_q(q, kvh * GROUP + g, ATTN_SCALE) for g in range(GROUP)], axis=0)
        s = jnp.where(m4, _dot_nt(ck, qa, precision=lax.Precision.HIGHEST), NEG_INF)
        e = jnp.exp(s - jnp.max(s, axis=0, keepdims=True))
        p = jnp.where(m4, e / jnp.sum(e, axis=0, keepdims=True), 0.0)
        imp = p[:, 0:tq]
        for g in range(1, GROUP):
            imp = imp + p[:, g * tq:(g + 1) * tq]
        o = lax.dot_general(p.astype(BF16), cv, (((0,), (0,)), ((), ())), preferred_element_type=F32)
        outs += [o[g * tq:(g + 1) * tq] for g in range(GROUP)]
        selt_ref[0, kvh, 0:nb, :] = _select_mask_t(imp, qpos, nb)
        selt_ref[0, kvh, nb:nbp, :] = jnp.zeros((nbp - nb, tq), F32)
    oc_ref[0] = _gather_heads(outs)


def _cmp_select_t(q, ckv, nbp, pos0, tq):
    b, sq, _ = q.shape
    nb = ckv.shape[1]
    return pl.pallas_call(
        functools.partial(_cmp_kernel_t, tq=tq, nb=nb, nbp=nbp, pos0=pos0), grid=(b, sq // tq),
        in_specs=[pl.BlockSpec((1, tq, ATTN_WIDTH), lambda i, t: (i, t, 0)),
                  pl.BlockSpec((1, nb, KV_ROW), lambda i, t: (i, 0, 0))],
        out_specs=[pl.BlockSpec((1, tq, ATTN_WIDTH), lambda i, t: (i, t, 0)),
                   pl.BlockSpec((1, N_KV_HEADS, nbp, tq), lambda i, t: (i, 0, 0, t))],
        out_shape=[_sds((b, sq, ATTN_WIDTH)), _sds((b, N_KV_HEADS, nbp, sq))],
        compiler_params=_cp(("arbitrary", "arbitrary")), name="cmp_select_t")(q, ckv)


def _cmp_sample_kernel(q_ref, ckv_ref, oc_ref, selt_ref, *, nb, nbr, pos0):
    nq = 8
    lanes = GROUP * nq
    j = lax.broadcasted_iota(jnp.int32, (nbr, lanes), 0)
    qpos = pos0 + lax.rem(lax.broadcasted_iota(jnp.int32, (nbr, lanes), 1), nq)
    m = ((j + 1) * CMP_BLOCK - 1 <= qpos) & (j < nb)
    imps = [[] for _ in range(N_KV_HEADS)]
    for i in range(q_ref.shape[0]):
        q = q_ref[i]
        ck = ckv_ref[i, 0:nbr, 0:KV_WIDTH]
        cv = ckv_ref[i, 0:nbr, KV_WIDTH:KV_ROW].astype(BF16)
        outs = []
        for kvh in range(N_KV_HEADS):
            qa = jnp.concatenate([_arrange_q(q, kvh * GROUP + g, ATTN_SCALE) for g in range(GROUP)], axis=0)
            s = jnp.where(m, _dot_nt(ck, qa, precision=lax.Precision.HIGHEST), NEG_INF)
            e = jnp.exp(s - jnp.max(s, axis=0, keepdims=True))
            p = jnp.where(m, e / jnp.sum(e, axis=0, keepdims=True), 0.0)
            imp = p[:, 0:nq]
            for g in range(1, GROUP):
                imp = imp + p[:, g * nq:(g + 1) * nq]
            imps[kvh].append(imp)
            o = lax.dot_general(p.astype(BF16), cv, (((0,), (0,)), ((), ())), preferred_element_type=F32)
            outs += [o[g * nq:(g + 1) * nq] for g in range(GROUP)]
        oc_ref[i] = _gather_heads(outs)
    qpos_all = pos0 + lax.rem(lax.broadcasted_iota(jnp.int32, (nbr, LANES), 1), nq)
    for kvh in range(N_KV_HEADS):
        selt_ref[0, kvh] = _select_mask_t(jnp.concatenate(imps[kvh], axis=1), qpos_all, nb)


def _cmp_select_sample(q8, ckv_p, nb, pos0):
    b = q8.shape[0]
    nbp = ckv_p.shape[1]
    nbr = 8 * (-(-nb // 8))
    rows = SAMPLE_ROWS
    assert b % rows == 0
    o_c, sel_t = pl.pallas_call(
        functools.partial(_cmp_sample_kernel, nb=nb, nbr=nbr, pos0=pos0), grid=(b // rows,),
        in_specs=[pl.BlockSpec((rows, 8, ATTN_WIDTH), lambda i: (i, 0, 0)),
                  pl.BlockSpec((rows, nbp, KV_ROW), lambda i: (i, 0, 0))],
        out_specs=[pl.BlockSpec((rows, 8, ATTN_WIDTH), lambda i: (i, 0, 0)),
                   pl.BlockSpec((1, N_KV_HEADS, nbr, LANES), lambda i: (i, 0, 0, 0))],
        out_shape=[_sds((b, 8, ATTN_WIDTH)), _sds((b // rows, N_KV_HEADS, nbr, LANES))],
        compiler_params=_cp(("arbitrary",)), name="cmp_select_sample")(q8, ckv_p)
    sel = jnp.transpose(sel_t.reshape(b // rows, N_KV_HEADS, nbr, rows, 8), (0, 3, 1, 4, 2))
    return o_c, jnp.pad(sel.reshape(b, N_KV_HEADS, 8, nbr), ((0, 0), (0, 0), (0, 0), (0, nbp - nbr)))


def _flash_kernel(tab_ref, q_ref, k_ref, vt_ref, *rest, tq, tk, mode):
    if mode == "sel":
        selt_ref, o_ref, qt, m_s, l_s, acc = rest
    else:
        o_ref, qt, m_s, l_s, acc = rest
    step_id = pl.program_id(1)
    qi = tab_ref[0, step_id]
    ki = tab_ref[1, step_id]
    flags = tab_ref[2, step_id]

    @pl.when((flags & 1) != 0)
    def _():
        zeros = jnp.zeros((HEAD_DIM, tq), BF16)
        for h in range(N_HEADS):
            kvh = h // GROUP
            qh = q_ref[0, h * HEAD_DIM:(h + 1) * HEAD_DIM, :]
            if mode == "sel":
                bias = ((selt_ref[0, kvh, 0:HEAD_DIM, :] - 1.0) * (-NEG_INF)).astype(BF16)
                cols = jnp.concatenate([qh, bias], axis=0)
            else:
                cols = jnp.concatenate([qh, zeros] if kvh == 0 else [zeros, qh], axis=0)
            qt[:, h * tq:(h + 1) * tq] = cols
        m_s[...] = jnp.full_like(m_s, M_INIT)
        l_s[...] = jnp.zeros_like(l_s)
        acc[...] = jnp.zeros_like(acc)

    def step(masked):
        vt = vt_ref[0].astype(BF16)
        if masked:
            d = (lax.broadcasted_iota(jnp.int32, (tk, tq), 0) - lax.broadcasted_iota(jnp.int32, (tk, tq), 1)
                 + (ki * tk - qi * tq))
            valid = (d <= 0) if mode == "sel" else (d <= 0) & (d > -WINDOW)
        if mode == "sel":
            ks = [k_ref[0, :, kvh * LANES:(kvh + 1) * LANES].astype(BF16) for kvh in range(N_KV_HEADS)]
        else:
            ks = [k_ref[0].astype(BF16)] * N_KV_HEADS

        def scores(h):
            return jnp.dot(ks[h // GROUP], qt[:, h * tq:(h + 1) * tq], preferred_element_type=F32)

        def accumulate(h, alpha, pv):
            rows = slice(h * LANES, (h + 1) * LANES)
            acc[rows] = alpha * acc[rows] + pv

        look = SCORE_LOOKAHEAD[mode]
        ahead = [scores(h) for h in range(look)]
        pending = None
        for h in range(N_HEADS):
            s = ahead.pop(0)
            if h + look < N_HEADS:
                ahead.append(scores(h + look))
            if masked:
                s = jnp.where(valid, s, NEG_INF)
            m_old = m_s[h, 0:1, :]
            m_new = jnp.maximum(m_old, jnp.max(s, axis=0, keepdims=True))
            alpha = jnp.exp2(m_old - m_new)
            p = jnp.exp2(s - m_new)
            l_s[h] = jnp.broadcast_to(alpha * l_s[h, 0:1, :] + jnp.sum(p, axis=0, keepdims=True), (8, tq))
            m_s[h] = jnp.broadcast_to(m_new, (8, tq))
            pv = jnp.dot(vt, p.astype(BF16), preferred_element_type=F32)
            if pending is not None:
                accumulate(*pending)
            pending = (h, alpha, pv)
        accumulate(*pending)

    @pl.when((flags & 4) != 0)
    def _():
        step(True)

    @pl.when((flags & 4) == 0)
    def _():
        step(False)

    @pl.when((flags & 2) != 0)
    def _():
        for h in range(N_HEADS):
            r0 = h * LANES + (h // GROUP) * HEAD_DIM
            o_ref[0, h * HEAD_DIM:(h + 1) * HEAD_DIM, :] = acc[r0:r0 + HEAD_DIM] / l_s[h, 0:1, :]


def _flash_schedule(s, tq, tk, mode):
    steps = []
    for qi in range(s // tq):
        q_lo, q_hi = qi * tq, qi * tq + tq - 1
        k_lo = 0 if mode == "sel" else max(q_lo - (WINDOW - 1), 0)
        tiles = list(range(k_lo // tk, q_hi // tk + 1))
        for ki in tiles:
            causal_edge = ki * tk + tk - 1 > q_lo
            window_edge = mode == "win" and q_hi - ki * tk >= WINDOW
            flags = (1 if ki == tiles[0] else 0) | (2 if ki == tiles[-1] else 0) | (4 if causal_edge or window_edge else 0)
            steps.append((qi, ki, flags))
    return jnp.asarray(np.array(steps, np.int32).T)


def _flash_prompt(q_t, k_rows, kv_t, sel_t, mode, tq, tk):
    b, _, s = q_t.shape
    if mode == "sel":
        assert s // CMP_BLOCK <= HEAD_DIM
    tab = _flash_schedule(s, tq, tk, mode)
    kw = k_rows.shape[-1]
    in_specs = [pl.BlockSpec((1, ATTN_WIDTH, tq), lambda i, p, t: (i, 0, t[0, p])),
                pl.BlockSpec((1, tk, kw), lambda i, p, t: (i, t[1, p], 0)),
                pl.BlockSpec((1, KV_WIDTH, tk), lambda i, p, t: (i, 1, t[1, p]))]
    args = [q_t, k_rows, kv_t]
    if mode == "sel":
        nbp = sel_t.shape[2]
        in_specs += [pl.BlockSpec((1, N_KV_HEADS, nbp, tq), lambda i, p, t: (i, 0, 0, t[0, p]))]
        args += [sel_t]
    grid_spec = pltpu.PrefetchScalarGridSpec(
        num_scalar_prefetch=1, grid=(b, tab.shape[1]), in_specs=in_specs,
        out_specs=pl.BlockSpec((1, ATTN_WIDTH, tq), lambda i, p, t: (i, 0, t[0, p])),
        scratch_shapes=[pltpu.VMEM((LANES, N_HEADS * tq), BF16), pltpu.VMEM((N_HEADS, 8, tq), F32),
                        pltpu.VMEM((N_HEADS, 8, tq), F32), pltpu.VMEM((N_HEADS * LANES, tq), F32)])
    return pl.pallas_call(
        functools.partial(_flash_kernel, tq=tq, tk=tk, mode=mode), grid_spec=grid_spec,
        out_shape=_sds((b, ATTN_WIDTH, s)),
        compiler_params=_cp(("arbitrary", "arbitrary")), name="flash_" + mode)(tab, *args)


def _sample_q_rows(q8):
    return jnp.concatenate([_arrange_q(q8, h, ATTN_SCALE) for h in range(N_HEADS)], axis=0).astype(BF16)


def _sample_out(o_all):
    return _gather_heads([o_all[h * 8:(h + 1) * 8] for h in range(N_HEADS)])


def _two_part_attention(s_past, s_new, vt_past, vt_new):
    mx = jnp.maximum(jnp.max(s_past, axis=-1, keepdims=True), jnp.max(s_new, axis=-1, keepdims=True))
    p_past = jnp.exp(s_past - mx)
    p_new = jnp.exp(s_new - mx)
    den = jnp.sum(p_past, axis=-1, keepdims=True) + jnp.sum(p_new, axis=-1, keepdims=True)
    return (_dot_nt(p_past.astype(BF16), vt_past) + _dot_nt(p_new.astype(BF16), vt_new)) / den


def _sel_sample_kernel(pt_ref, pool_ref, q_ref, newt_ref, sel_ref, e_ref, o_ref, buf, sem, kcomb,
                       *, n_pages, pos0, n_new):
    nkeys = n_pages * PAGE_SIZE
    nblk = nkeys // CMP_BLOCK
    nrow = 8 * N_HEADS

    @pl.when(pl.program_id(0) == 0)
    def _():
        kcomb[KV_WIDTH:KV_WIDTH + nblk, :] = e_ref[0:nblk, :]

    slot = _paged_prefetch(pt_ref, pool_ref, lambda s, p: buf.at[s, :, pl.ds(p * PAGE_SIZE, PAGE_SIZE)], sem, n_pages)
    qall = _sample_q_rows(q_ref[0])
    selrows = jnp.concatenate([sel_ref[0, h // GROUP] for h in range(N_HEADS)], axis=0)
    qpos = pos0 + lax.rem(lax.broadcasted_iota(jnp.int32, (nrow, 1), 0), 8)
    assert nkeys <= pos0
    bias = ((selrows[:, 0:nblk] - 1.0) * (-NEG_INF)).astype(BF16)
    kcomb[0:KV_WIDTH, :] = buf[slot, 0:KV_WIDTH, :].astype(BF16)
    s_past = jnp.dot(jnp.concatenate([qall, bias], axis=1), kcomb[...], preferred_element_type=F32)
    newt = newt_ref[0].astype(BF16)
    inew = lax.broadcasted_iota(jnp.int32, (nrow, LANES), 1) - (LANES - n_new)
    sel_new = selrows[:, nblk:nblk + 1] > 0.5
    s_new = jnp.dot(qall, newt[0:KV_WIDTH], preferred_element_type=F32)
    s_new = jnp.where(sel_new & (inew >= 0) & (pos0 + inew <= qpos), s_new, NEG_INF)
    o = _two_part_attention(s_past, s_new, buf[slot, KV_WIDTH:KV_ROW, :].astype(BF16), newt[KV_WIDTH:KV_ROW])
    o_ref[0] = _sample_out(o)


def _sel_sample(page_table, pool_t, qrot8, kvs_newt, sel, emat, pos0, n_new):
    b, n_pages = page_table.shape
    nbp = sel.shape[-1]
    nkeys = n_pages * PAGE_SIZE
    grid_spec = pltpu.PrefetchScalarGridSpec(
        num_scalar_prefetch=1, grid=(b,),
        in_specs=[pl.BlockSpec(memory_space=pl.ANY),
                  pl.BlockSpec((1, 8, ATTN_WIDTH), lambda i, pt: (i, 0, 0)),
                  pl.BlockSpec((1, KV_ROW, LANES), lambda i, pt: (i, 0, 0)),
                  pl.BlockSpec((1, N_KV_HEADS, 8, nbp), lambda i, pt: (i, 0, 0, 0)),
                  pl.BlockSpec((nbp, nkeys), lambda i, pt: (0, 0))],
        out_specs=pl.BlockSpec((1, 8, ATTN_WIDTH), lambda i, pt: (i, 0, 0)),
        scratch_shapes=[pltpu.VMEM((2, KV_ROW, nkeys), F32), pltpu.SemaphoreType.DMA((2,)),
                        pltpu.VMEM((KV_WIDTH + nkeys // CMP_BLOCK, nkeys), BF16)])
    return pl.pallas_call(
        functools.partial(_sel_sample_kernel, n_pages=n_pages, pos0=pos0, n_new=n_new),
        grid_spec=grid_spec, out_shape=_sds((b, 8, ATTN_WIDTH)),
        compiler_params=_cp(("arbitrary",)), name="sel_sample")(page_table, pool_t, qrot8, kvs_newt, sel, emat)


def _win_sample_kernel(q_ref, buf_ref, newt_ref, o_ref, nw_ref, *, wb, n_new):
    nrow = 8 * N_HEADS
    srow = lax.rem(lax.broadcasted_iota(jnp.int32, (nrow, 1), 0), 8)
    i_past = lax.broadcasted_iota(jnp.int32, (nrow, wb), 1)
    dp = wb + srow - i_past
    ok_past = (dp >= 0) & (dp < WINDOW)
    i_new = lax.broadcasted_iota(jnp.int32, (nrow, LANES), 1) - (LANES - n_new)
    dn = srow - i_new
    ok_new = (i_new >= 0) & (dn >= 0) & (dn < WINDOW)
    lane = lax.broadcasted_iota(jnp.int32, (KV_ROW, LANES), 1)
    for r in range(q_ref.shape[0]):
        qall = _sample_q_rows(q_ref[r])
        buf = buf_ref[r]
        newt = newt_ref[r]
        s_past = jnp.dot(qall, buf[0:KV_WIDTH].astype(BF16), preferred_element_type=F32)
        s_past = jnp.where(ok_past, s_past, NEG_INF)
        s_new = jnp.dot(qall, newt[0:KV_WIDTH].astype(BF16), preferred_element_type=F32)
        s_new = jnp.where(ok_new, s_new, NEG_INF)
        o = _two_part_attention(s_past, s_new, buf[KV_WIDTH:KV_ROW].astype(BF16),
                                newt[KV_WIDTH:KV_ROW].astype(BF16))
        o_ref[r] = _sample_out(o)
        nw_ref[r] = pltpu.roll(buf, wb - n_new, 1)
        nw_ref[r, :, wb - LANES:wb] = jnp.where(lane >= LANES - n_new, newt,
                                                pltpu.roll(buf[:, wb - LANES:wb], LANES - n_new, 1))


def _win_sample(qrot8, win_t, kvw_newt, n_new):
    b, _, wb = win_t.shape
    rows = WIN_SAMPLE_ROWS if b % WIN_SAMPLE_ROWS == 0 else 1
    return pl.pallas_call(
        functools.partial(_win_sample_kernel, wb=wb, n_new=n_new), grid=(b // rows,),
        in_specs=[pl.BlockSpec((rows, 8, ATTN_WIDTH), lambda i: (i, 0, 0)),
                  pl.BlockSpec((rows, KV_ROW, wb), lambda i: (i, 0, 0)),
                  pl.BlockSpec((rows, KV_ROW, LANES), lambda i: (i, 0, 0))],
        out_specs=[pl.BlockSpec((rows, 8, ATTN_WIDTH), lambda i: (i, 0, 0)),
                   pl.BlockSpec((rows, KV_ROW, wb), lambda i: (i, 0, 0))],
        out_shape=[_sds((b, 8, ATTN_WIDTH)), _sds((b, KV_ROW, wb))],
        compiler_params=_cp(("arbitrary",)), name="win_sample")(qrot8, win_t, kvw_newt)


def _post_kernel(x_ref, or_ref, oc_ref, os_ref, ow_ref, g_ref, m2_ref, m3_ref, m4_ref, m5_ref,
                 gnr_ref, gna_ref, ln2_ref, fin_ref, eg_ref, wo_ref, wg_ref, wu_ref, wd_ref, y_ref,
                 *, ff_chunks, attn_t):
    g = g_ref[0]
    g_hi = g.astype(BF16)
    g_lo = (g - g_hi.astype(F32)).astype(BF16)
    gx = jnp.dot(jnp.concatenate([g_hi, g_lo], axis=-1), eg_ref[...], preferred_element_type=F32)
    o_s, o_w = (os_ref[0].T, ow_ref[0].T) if attn_t else (os_ref[0], ow_ref[0])
    attn = (gx[:, 0:ATTN_WIDTH] * oc_ref[0] + gx[:, ATTN_WIDTH:2 * ATTN_WIDTH] * o_s
            + gx[:, 2 * ATTN_WIDTH:3 * ATTN_WIDTH] * o_w)
    mix = (_bdot(_rms(or_ref[0], gnr_ref[...]), wo_ref[0:RNN_WIDTH, :])
           + _bdot(_rms(attn, gna_ref[...]), wo_ref[RNN_WIDTH:D_MODEL, :]))
    x1 = x_ref[0] + m2_ref[0] * mix
    h = (_rms(x1, ln2_ref[...]) * (1.0 + m4_ref[0]) + m3_ref[0]).astype(BF16)
    cw = D_FF // ff_chunks
    ff = jnp.zeros(x1.shape, F32)
    for c in range(ff_chunks):
        gate = jnp.dot(h, wg_ref[:, c * cw:(c + 1) * cw], preferred_element_type=F32)
        up = jnp.dot(h, wu_ref[:, c * cw:(c + 1) * cw], preferred_element_type=F32)
        act = gate * jax.nn.sigmoid(gate) * up
        ff = ff + _bdot(act, wd_ref[c * cw:(c + 1) * cw, :])
    y = x1 + m5_ref[0] * ff
    y_ref[0] = _rms(y, fin_ref[...])


def _post(x3, o_r, o_c, o_s, o_w, g, mods, gnr, gna, ln2, fin, egate, wo, wg, wu, wd, tm, attn_t, ff_chunks=2):
    bx, sx, _ = x3.shape
    sm = mods[0].shape[1]
    if sm == 1:
        mod_spec = pl.BlockSpec((1, 1, D_MODEL), lambda b, t: (b, 0, 0))
    else:
        mod_spec = pl.BlockSpec((1, tm, D_MODEL), lambda b, t: (b, t, 0))
    tile = lambda w: pl.BlockSpec((1, tm, w), lambda b, t: (b, t, 0))
    attn_tile = pl.BlockSpec((1, ATTN_WIDTH, tm), lambda b, t: (b, 0, t)) if attn_t else tile(ATTN_WIDTH)

    def const(shape):
        return pl.BlockSpec(shape, lambda b, t: (0,) * len(shape), pipeline_mode=pl.Buffered(1))

    return pl.pallas_call(
        functools.partial(_post_kernel, ff_chunks=ff_chunks, attn_t=attn_t), grid=(bx, sx // tm),
        in_specs=[tile(D_MODEL), tile(RNN_WIDTH), tile(ATTN_WIDTH), attn_tile, attn_tile, tile(LANES),
                  mod_spec, mod_spec, mod_spec, mod_spec,
                  const((1, RNN_WIDTH)), const((1, ATTN_WIDTH)), const((1, D_MODEL)), const((1, D_MODEL)),
                  const((2 * LANES, 3 * ATTN_WIDTH)), const((D_MODEL, D_MODEL)),
                  const((D_MODEL, D_FF)), const((D_MODEL, D_FF)), const((D_FF, D_MODEL))],
        out_specs=tile(D_MODEL), out_shape=_sds((bx, sx, D_MODEL)),
        compiler_params=_cp(("arbitrary", "arbitrary")), name="post")(
            x3, o_r, o_c, o_s, o_w, g, mods[2], mods[3], mods[4], mods[5], gnr, gna, ln2, fin, egate, wo, wg, wu, wd)


def _rope_tables(pos):
    inv = jnp.exp(jnp.arange(ROT_HALF, dtype=F32) * (-math.log(ROPE_THETA) / ROT_HALF))
    ang = pos.astype(F32)[:, None] * inv[None, :]
    cos, sin = jnp.cos(ang), jnp.sin(ang)
    n = pos.shape[0]
    one = jnp.ones((n, HEAD_DIM - ROT_DIM), F32)
    zero = jnp.zeros((n, HEAD_DIM - ROT_DIM), F32)
    z8 = jnp.zeros((n, ROT_HALF), F32)
    c = jnp.concatenate([cos, cos, one], axis=-1)
    s1 = jnp.concatenate([-sin, z8, zero], axis=-1)
    s2 = jnp.concatenate([z8, sin, zero], axis=-1)
    return tuple(jnp.concatenate([t, t], axis=-1) for t in (c, s1, s2))


def _blockdiag(w):
    nb, d, e = w.shape
    return jnp.einsum('nde,nm->ndme', w, jnp.eye(nb, dtype=w.dtype)).reshape(nb * d, nb * e)


def _prep_params(w_in, lru_wa, lru_wx, lru_ba, lru_bx, cmp_pe, cmp_w1, cmp_w2):
    w_in_p = jnp.pad(w_in, ((0, 0), (0, IN_COLS_PAD - w_in.shape[1]))).astype(BF16)
    wax = jnp.concatenate([_blockdiag(lru_wa), _blockdiag(lru_wx)], axis=1).astype(BF16)
    bax = jnp.concatenate([lru_ba, lru_bx])[None, :]
    pe_row = jnp.broadcast_to(jnp.transpose(cmp_pe, (1, 0, 2))[:, :, None, :],
                              (CMP_BLOCK, 2, N_KV_HEADS, HEAD_DIM)).reshape(CMP_BLOCK, KV_ROW)
    w2bd = _blockdiag(jnp.repeat(cmp_w2, N_KV_HEADS, axis=0)).astype(BF16)
    e = np.zeros((LANES, 3 * ATTN_WIDTH), np.float32)
    for br in range(3):
        for h in range(N_HEADS):
            e[br * N_HEADS + h, br * ATTN_WIDTH + h * HEAD_DIM:br * ATTN_WIDTH + (h + 1) * HEAD_DIM] = 1.0
    egate = jnp.asarray(np.concatenate([e, e], axis=0), BF16)
    return w_in_p, wax, bax, pe_row, cmp_w1.astype(BF16), w2bd, egate


def _expand_matrix(nbp, n_tiles, tk):
    blk = (np.arange(n_tiles)[:, None, None] * tk + np.arange(tk)[None, None, :]) // CMP_BLOCK
    return jnp.asarray(blk == np.arange(nbp)[None, :, None], BF16)


def _kv_from_t(kv_t):
    n, _, s = kv_t.shape
    return jnp.transpose(kv_t.reshape(n, 2, N_KV_HEADS, HEAD_DIM, s), (0, 4, 1, 2, 3))


def _kv_to_t(kv):
    n, s = kv.shape[:2]
    return jnp.transpose(kv, (0, 2, 3, 4, 1)).reshape(n, KV_ROW, s)


def _layer_prompt(x, mods, P, tm=512, tq=256):
    b, s, _ = x.shape
    cos, s1, s2 = _rope_tables(jnp.arange(s))
    xr, yr, q, _, kvc, ks, kw, kvc_t, kvs_t, kvw_t, g, q_t = _pre(
        x, mods[0], mods[1], P['ln1'], P['w_in'], cos, s1, s2, tm)
    o_r, h_new = _rglru_prompt(xr, yr, P['conv_w'], P['conv_b'], P['wax'], P['bax'], P['lam'])
    nb = s // CMP_BLOCK
    ckv = _compress_prompt(kvc, P['pe_row'], P['w1'], P['w2bd'])
    nbp = LANES * (-(-nb // LANES))
    o_c, sel_t = _cmp_select_t(q, ckv, nbp, 0, min(256, s))
    o_s_t = _flash_prompt(q_t, ks, kvs_t, sel_t, "sel", tq, min(512, s))
    o_w_t = _flash_prompt(q_t, kw, kvw_t, None, "win", tq, tq)
    y = _post(x, o_r, o_c, o_s_t, o_w_t, g, mods, P['gn_rnn'], P['gn_attn'], P['ln2'], P['final_g'], P['egate'],
              P['w_out'], P['w_gate'], P['w_up'], P['w_down'], tm, True)
    wlen = min(WINDOW, s)
    return y, (_kv_from_t(kvc_t), _kv_from_t(kvs_t), _kv_from_t(kvw_t[:, :, s - wlen:]),
               xr[:, s - (CONV_W - 1):], h_new[:, 0])


def _pad_rows8(a):
    return jnp.pad(a, ((0, 0), (0, 8 - a.shape[1]), (0, 0)))


def _layer_sample(x, mods_tok, cmp_pool, sel_pool, win_buf, conv_buf, h0, page_table, P):
    b, s, _ = x.shape
    n_pages = page_table.shape[1]
    past = n_pages * PAGE_SIZE
    nb = past // CMP_BLOCK + 1
    pos = past + jnp.arange(s)
    cos, s1, s2 = (jnp.tile(t, (b, 1)) for t in _rope_tables(pos))
    n = b * s
    flat = lambda a: a.reshape(1, n, a.shape[-1])
    outs = _pre(flat(x), mods_tok[0], mods_tok[1], P['ln1'], P['w_in'], cos, s1, s2, min(512, n))
    xr, yr, q, qrot, kvc = (o.reshape(b, s, o.shape[-1]) for o in outs[:5])
    kvc_t, kvs_t, kvw_t = (jnp.transpose(o[0].reshape(KV_ROW, b, s), (1, 0, 2)) for o in outs[7:10])
    lane_pad = lambda a: jnp.pad(a, ((0, 0), (0, 0), (LANES - s, 0)))
    g = outs[10]
    tmaj = lambda a: jnp.transpose(a, (1, 0, 2))
    o_r_t, h_new = _rglru_sample(tmaj(xr), tmaj(yr), tmaj(conv_buf), h0, P['conv_w'], P['conv_b'],
                                 P['wax'], P['bax'], P['lam'])
    o_r = tmaj(o_r_t)
    nbp = LANES * (-(-nb // LANES))
    ckv_p = _compress_sample(page_table, _kv_to_t(cmp_pool), _pad_rows8(kvc), P['pe_row'], P['w1'], P['w2bd'], nbp)
    q8, qrot8 = _pad_rows8(q), _pad_rows8(qrot)
    o_c8, sel = _cmp_select_sample(q8, ckv_p, nb, past)
    o_s8 = _sel_sample(page_table, _kv_to_t(sel_pool), qrot8, lane_pad(kvs_t), sel,
                       _expand_matrix(nbp, 1, past)[0], past, s)
    o_w8, new_win_t = _win_sample(qrot8, _kv_to_t(win_buf), lane_pad(kvw_t), s)
    y = _post(flat(x), flat(o_r), flat(o_c8[:, :s]), flat(o_s8[:, :s]), flat(o_w8[:, :s]), g, mods_tok,
              P['gn_rnn'], P['gn_attn'], P['ln2'], P['final_g'], P['egate'],
              P['w_out'], P['w_gate'], P['w_up'], P['w_down'], min(512, n), False)
    return y.reshape(b, s, D_MODEL), (_kv_from_t(kvc_t), _kv_from_t(kvs_t), _kv_from_t(new_win_t),
                                      xr[:, s - (CONV_W - 1):], h_new)


def kernel(x_prompt, x_sample, cache_cmp_kv, cache_sel_kv, state_win_kv, state_conv, state_lru_h, page_table,
           c_prompt, c_sample, ln1_g, ln2_g, w_ada, b_ada, w_in, conv_w, conv_b, lru_wa, lru_ba, lru_wx, lru_bx,
           lru_lambda, cmp_pe, cmp_w1, cmp_w2, gn_rnn, gn_attn, w_out, w_gate, w_up, w_down, final_g):
    depth = w_in.shape[0]
    assert depth == 1
    bp = x_prompt.shape[0]
    bs, ss, _ = x_sample.shape
    l = 0
    w_in_p, wax, bax, pe_row, w1, w2bd, egate = _prep_params(
        w_in[l], lru_wa[l], lru_wx[l], lru_ba[l], lru_bx[l], cmp_pe[l], cmp_w1[l], cmp_w2[l])
    P = {'ln1': ln1_g[l][None], 'ln2': ln2_g[l][None], 'w_in': w_in_p, 'conv_w': conv_w[l], 'conv_b': conv_b[l][None],
         'wax': wax, 'bax': bax, 'lam': lru_lambda[l][None], 'pe_row': pe_row, 'w1': w1, 'w2bd': w2bd,
         'gn_rnn': gn_rnn[l][None], 'gn_attn': gn_attn[l][None], 'egate': egate, 'final_g': final_g[None],
         'w_out': w_out[l].astype(BF16), 'w_gate': w_gate[l].astype(BF16), 'w_up': w_up[l].astype(BF16),
         'w_down': w_down[l].astype(BF16)}
    mods_all = _ada(jnp.concatenate([c_prompt, c_sample], axis=0), w_ada[l].astype(BF16), b_ada[l][None])
    mods_p = [m[:, None, :] for m in jnp.split(mods_all[:bp], 6, axis=-1)]
    mods_s = [jnp.repeat(m, ss, axis=0)[None] for m in jnp.split(mods_all[bp:], 6, axis=-1)]

    yp, st_p = _layer_prompt(x_prompt, mods_p, P)
    ys, st_s = _layer_sample(x_sample, mods_s, cache_cmp_kv[l], cache_sel_kv[l], state_win_kv[l],
                             state_conv[l], state_lru_h[l], page_table, P)

    return (yp, ys, st_p[0][None], st_s[0][None], st_p[1][None], st_s[1][None], st_p[2][None], st_s[2][None],
            st_p[3][None], st_s[3][None], st_p[4][None], st_s[4][None])
```
